```python
import math
import numpy as np
import jax
import jax.numpy as jnp
from jax import lax

D_MODEL = 1024
BATCH = 8
SEQ = 4096
DEPTH = 4

GRID_W = 64
CTX_LEN = 256
EXPAND = 2
MIX_W = EXPAND * D_MODEL
BRANCH_W = MIX_W // 2
DA_HEADS = 8
DA_V = BRANCH_W // DA_HEADS
DA_QK = DA_V // 2
DA_QK_W = DA_HEADS * 2 * DA_QK
SC_WIDTH = 3
NA_HEADS = 16
NA_DIM = BRANCH_W // NA_HEADS
WIN_ROWS = 8
WIN_COLS = 16
COL_QBLOCK = 16
COL_BAND = 2 * WIN_COLS
GLA_HEADS = 4
GLA_K_W = BRANCH_W // 2
GLA_DK = GLA_K_W // GLA_HEADS
GLA_DV = BRANCH_W // GLA_HEADS
GLA_RANK = 16
GLA_TAU = 16.0
GLA_CHUNK = 64
QBLOCK = 128
ROPE_BASE = 10000.0
EPS = 1e-6
N_EVEN = (DEPTH + 1) // 2
N_ODD = DEPTH // 2
EVEN_SIZES = (DA_QK_W, BRANCH_W, DA_QK_W, BRANCH_W, BRANCH_W, BRANCH_W, BRANCH_W, BRANCH_W)
EVEN_CTX_N = 2
ODD_SIZES = (BRANCH_W, BRANCH_W, GLA_K_W, BRANCH_W, 2 * GLA_RANK, BRANCH_W, GLA_K_W, BRANCH_W, BRANCH_W)
ODD_CTX_N = 5

kernel_name = 'hybrid_diffattn_shortconv_natten_gla'


def rmsnorm(x, g):
    xf = x.astype(jnp.float32)
    y = xf * lax.rsqrt(jnp.mean(xf * xf, axis=-1, keepdims=True) + EPS)
    return (y * g.astype(jnp.float32)).astype(x.dtype)


def split_cols(p, sizes):
    idx = np.cumsum(sizes)[:-1].tolist()
    return jnp.split(p, idx, axis=-1)


def heads(t, hd):
    return t.reshape(t.shape[0], t.shape[1], -1, hd)


def to_bht(t, hd):
    return heads(t, hd).transpose(0, 2, 1, 3)


def flip_t(t):
    return None if t is None else t[:, :, ::-1]


def axial_rope(seq, dtype):
    half = DA_QK // 2
    inv = 1.0 / (ROPE_BASE ** (jnp.arange(0, half, 2, dtype=jnp.float32) / half))
    t = jnp.arange(seq)
    row = (t // GRID_W).astype(jnp.float32)[:, None] * inv
    col = (t % GRID_W).astype(jnp.float32)[:, None] * inv
    ang = jnp.concatenate([row, row, col, col], axis=-1)
    return (jnp.cos(ang).astype(dtype)[:, None, None, :], jnp.sin(ang).astype(dtype)[:, None, None, :])


def apply_rope(x, cos, sin):
    half = DA_QK // 2
    q4 = half // 2
    def rot(t):
        return jnp.concatenate([-t[..., q4:], t[..., :q4]], axis=-1)
    r = jnp.concatenate([rot(x[..., :half]), rot(x[..., half:])], axis=-1)
    return x * cos + r * sin


def query_blocks(fn, q):
    bsz, t = q.shape[:2]
    qb = jnp.moveaxis(q.reshape((bsz, t // QBLOCK, QBLOCK) + q.shape[2:]), 1, 0)
    ob = jnp.moveaxis(lax.map(fn, qb), 0, 1)
    return ob.reshape((bsz, t) + ob.shape[3:])


def diff_attention(q, k, v, lam):
    scale = DA_QK ** -0.5
    def block(qb):
        s = jnp.einsum('bqhmd,bkhmd->bhmqk', qb, k).astype(jnp.float32) * scale
        p = jax.nn.softmax(s, axis=-1)
        a = (p[:, :, 0] - lam * p[:, :, 1]).astype(v.dtype)
        return jnp.einsum('bhqk,bkhe->bqhe', a, v)
    return query_blocks(block, q)


def dense_attention(q, k, v):
    s = jnp.einsum('bqhd,bkhd->bhqk', q, k).astype(jnp.float32) * (q.shape[-1] ** -0.5)
    p = jax.nn.softmax(s, axis=-1).astype(v.dtype)
    return jnp.einsum('bhqk,bkhd->bqhd', p, v)


def short_conv(u, w):
    ch = u.shape[-1]
    return lax.conv_general_dilated(
        u, w[:, None, :].astype(u.dtype), window_strides=(1,),
        padding=[(SC_WIDTH // 2, SC_WIDTH // 2)],
        dimension_numbers=('NWC', 'WIO', 'NWC'), feature_group_count=ch)


def neighbourhood_attention(q, k, v, kc, vc, rpb):
    bsz, seq, nh, d = q.shape
    rows = seq // GRID_W
    wr = min(WIN_ROWS, rows)
    n_cb = GRID_W // COL_QBLOCK
    scale = d ** -0.5
    r = np.arange(rows)
    row_start = np.clip(r - wr // 2, 0, rows - wr)
    row_off = row_start[:, None] + np.arange(wr)[None, :] - r[:, None] + (WIN_ROWS - 1)
    qcol = np.arange(GRID_W).reshape(n_cb, COL_QBLOCK)
    band_start = np.clip(np.arange(n_cb) * COL_QBLOCK - WIN_COLS // 2, 0, GRID_W - COL_BAND)
    kcol = band_start[:, None] + np.arange(COL_BAND)[None, :]
    win_start = np.clip(qcol - WIN_COLS // 2, 0, GRID_W - WIN_COLS)
    kc3 = kcol[:, None, :]
    col_valid = (kc3 >= win_start[..., None]) & (kc3 < win_start[..., None] + WIN_COLS)
    col_off = np.clip(kc3 - qcol[..., None] + (WIN_COLS - 1), 0, 2 * WIN_COLS - 2)
    col_mask = jnp.asarray(col_valid)[:, :, None, :]
    rpb_cols = rpb[:, :, col_off].astype(jnp.float32)
    kg = k.reshape(bsz, rows, GRID_W, nh, d)
    vg = v.reshape(bsz, rows, GRID_W, nh, d)
    qg = jnp.moveaxis(q.reshape(bsz, rows, n_cb, COL_QBLOCK, nh, d), 1, 0)
    n_loc = wr * COL_BAND

    def row_block(inp):
        q_r, rs, ro = inp
        k_band = lax.dynamic_slice_in_dim(kg, rs, wr, axis=1)[:, :, kcol]
        v_band = lax.dynamic_slice_in_dim(vg, rs, wr, axis=1)[:, :, kcol]
        s_loc = jnp.einsum('bnqhd,brnjhd->bhnqrj', q_r, k_band).astype(jnp.float32) * scale
        bias = jnp.transpose(jnp.take(rpb_cols, ro, axis=1), (0, 2, 3, 1, 4))
        s_loc = jnp.where(col_mask, s_loc + bias, -jnp.inf).reshape(bsz, nh, n_cb, COL_QBLOCK, n_loc)
        s_ctx = jnp.einsum('bnqhd,bkhd->bhnqk', q_r, kc).astype(jnp.float32) * scale
        p = jax.nn.softmax(jnp.concatenate([s_loc, s_ctx], axis=-1), axis=-1).astype(v.dtype)
        p_loc = p[..., :n_loc].reshape(bsz, nh, n_cb, COL_QBLOCK, wr, COL_BAND)
        return (jnp.einsum('bhnqrj,brnjhd->bnqhd', p_loc, v_band)
                + jnp.einsum('bhnqk,bkhd->bnqhd', p[..., n_loc:], vc))

    out = lax.map(row_block, (qg, jnp.asarray(row_start, jnp.int32), jnp.asarray(row_off, jnp.int32)))
    return jnp.moveaxis(out, 0, 1).reshape(bsz, seq, nh, d)


def gla_log_gates(lr, gate_up, gate_bias):
    out = []
    for dirn, l in enumerate(jnp.split(lr, 2, axis=-1)):
        z = (l @ gate_up[dirn] + gate_bias[dirn]).astype(jnp.float32)
        out.append(to_bht(jax.nn.log_sigmoid(z) / GLA_TAU, GLA_DK))
    return out


def gla_scan(q, k, v, logg, s0):
    bsz, nh, t, _ = k.shape
    dv = v.shape[-1]
    n = t // GLA_CHUNK
    want = q is not None
    tri = jnp.tril(jnp.ones((GLA_CHUNK, GLA_CHUNK), bool))[:, :, None]

    def chunks(a):
        return jnp.moveaxis(a.reshape(bsz, nh, n, GLA_CHUNK, a.shape[-1]), 2, 0)

    def step(state, inp):
        if want:
            qc, kc, vc, gc = inp
        else:
            kc, vc, gc = inp
        b = jnp.cumsum(gc, axis=-2)
        b_end = b[:, :, -1:, :]
        new = (jnp.exp(b_end[:, :, 0, :])[..., None] * state
               + jnp.einsum('bhsd,bhse->bhde', kc * jnp.exp(b_end - b), vc))
        if not want:
            return new, None
        qf = qc.astype(jnp.float32)
        o_inter = jnp.einsum('bhtd,bhde->bhte', qf * jnp.exp(b), state)
        decay = jnp.exp(jnp.where(tri, b[:, :, :, None, :] - b[:, :, None, :, :], -jnp.inf))
        att = jnp.einsum('bhtd,bhsd,bhtsd->bhts', qf, kc.astype(jnp.float32), decay)
        o = o_inter + jnp.einsum('bhts,bhse->bhte', att, vc.astype(jnp.float32))
        return new, o.astype(v.dtype)

    xs = tuple(chunks(a) for a in ((q, k, v, logg) if want else (k, v, logg)))
    s_fin, o = lax.scan(step, s0, xs)
    if want:
        o = jnp.moveaxis(o, 0, 2).reshape(bsz, nh, t, dv)
    return s_fin, o


def gla_out(o, g):
    o = rmsnorm(o.transpose(0, 2, 1, 3), g)
    return o.reshape(o.shape[0], o.shape[1], -1)


def even_mixer(h, hc, w_in, lam_p, subln, conv_w, lam_init, ctx_out, cos, sin):
    bsz, seq, _ = h.shape
    clen = hc.shape[1]
    ak, av, aq, ag, bh, bb, bc, bg = split_cols(h @ w_in, EVEN_SIZES)
    csz = EVEN_SIZES if ctx_out else EVEN_SIZES[:EVEN_CTX_N]
    cp = split_cols(hc @ w_in[:, :sum(csz)], csz)
    lam = (jnp.exp(jnp.sum(lam_p[0] * lam_p[1])) - jnp.exp(jnp.sum(lam_p[2] * lam_p[3])) + lam_init).astype(jnp.float32)
    qk_shape = (DA_HEADS, 2, DA_QK)
    q = apply_rope(aq.reshape(bsz, seq, *qk_shape), cos, sin)
    k = apply_rope(ak.reshape(bsz, seq, *qk_shape), cos, sin)
    kc = cp[0].reshape(bsz, clen, *qk_shape)
    vc = heads(cp[1], DA_V)
    k_all = jnp.concatenate([k, kc], axis=1)
    v_all = jnp.concatenate([heads(av, DA_V), vc], axis=1)
    oa = rmsnorm(diff_attention(q, k_all, v_all, lam), subln) * (1.0 - lam_init)
    ua = oa.reshape(bsz, seq, BRANCH_W) * jax.nn.silu(ag)
    ub = bb * short_conv(bc * bh, conv_w) * jax.nn.silu(bg)
    u = jnp.concatenate([ua, ub], axis=-1)
    if not ctx_out:
        return u, None
    oac = rmsnorm(diff_attention(cp[2].reshape(bsz, clen, *qk_shape), kc, vc, lam), subln) * (1.0 - lam_init)
    uac = oac.reshape(bsz, clen, BRANCH_W) * jax.nn.silu(cp[3])
    ubc = cp[5] * short_conv(cp[6] * cp[4], conv_w) * jax.nn.silu(cp[7])
    return u, jnp.concatenate([uac, ubc], axis=-1)


def odd_mixer(h, hc, w_in, rpb, gate_up, gate_bias, gnorm, ctx_out):
    bsz, seq, _ = h.shape
    clen = hc.shape[1]
    ck, cv, dk, dv, dlr, cq, dq, cg, dg = split_cols(h @ w_in, ODD_SIZES)
    csz = ODD_SIZES if ctx_out else ODD_SIZES[:ODD_CTX_N]
    cp = split_cols(hc @ w_in[:, :sum(csz)], csz)
    kc_na, vc_na = heads(cp[0], NA_DIM), heads(cp[1], NA_DIM)
    o_na = neighbourhood_attention(heads(cq, NA_DIM), heads(ck, NA_DIM), heads(cv, NA_DIM), kc_na, vc_na, rpb)
    qscale = GLA_DK ** -0.5
    s0 = jnp.zeros((bsz, GLA_HEADS, GLA_DK, GLA_DV), jnp.float32)
    gcf, gcb = gla_log_gates(cp[4], gate_up, gate_bias)
    kch, vch = to_bht(cp[2], GLA_DK), to_bht(cp[3], GLA_DV)
    qch = to_bht(cp[6], GLA_DK) * qscale if ctx_out else None
    s_f, ocf = gla_scan(qch, kch, vch, gcf, s0)
    s_b, ocb = gla_scan(flip_t(qch), flip_t(kch), flip_t(vch), flip_t(gcb), s0)
    gf, gb = gla_log_gates(dlr, gate_up, gate_bias)
    qh, kh, vh = to_bht(dq, GLA_DK) * qscale, to_bht(dk, GLA_DK), to_bht(dv, GLA_DV)
    _, of = gla_scan(qh, kh, vh, gf, s_f)
    _, ob = gla_scan(flip_t(qh), flip_t(kh), flip_t(vh), flip_t(gb), s_b)
    o_gla = gla_out(of + flip_t(ob), gnorm)
    u = jnp.concatenate([o_na.reshape(bsz, seq, BRANCH_W) * jax.nn.silu(cg), o_gla * jax.nn.silu(dg)], axis=-1)
    if not ctx_out:
        return u, None
    o_na_c = dense_attention(heads(cp[5], NA_DIM), kc_na, vc_na)
    o_gla_c = gla_out(ocf + flip_t(ocb), gnorm)
    uc = jnp.concatenate([o_na_c.reshape(bsz, clen, BRANCH_W) * jax.nn.silu(cp[7]), o_gla_c * jax.nn.silu(cp[8])], axis=-1)
    return u, uc


def setup_inputs(seed: int = 0) -> dict:
    key = jax.random.key(seed)
    ks = jax.random.split(key, 18)
    def nrm(k, shape, s):
        return jax.random.normal(k, shape, jnp.float32) * s
    ew = sum(EVEN_SIZES)
    ow = sum(ODD_SIZES)
    return {
        'x': nrm(ks[0], (BATCH, SEQ, D_MODEL), 1.0),
        'c': nrm(ks[1], (BATCH, D_MODEL), 1.0),
        'ctx': nrm(ks[2], (BATCH, CTX_LEN, D_MODEL), 1.0),
        'c_ctx': nrm(ks[3], (D_MODEL,), 1.0),
        'w_mod': nrm(ks[4], (DEPTH, D_MODEL, 3 * D_MODEL), 0.5 * D_MODEL ** -0.5),
        'b_mod': nrm(ks[5], (DEPTH, 3 * D_MODEL), 0.02),
        'g_pre': 1.0 + nrm(ks[6], (DEPTH, D_MODEL), 0.02),
        'g_post': 1.0 + nrm(ks[7], (DEPTH, D_MODEL), 0.02),
        'w_out': nrm(ks[8], (DEPTH, MIX_W, D_MODEL), MIX_W ** -0.5),
        'ev_w_in': nrm(ks[9], (N_EVEN, D_MODEL, ew), D_MODEL ** -0.5),
        'ev_lambda': nrm(ks[10], (N_EVEN, 4, DA_QK), 0.1),
        'ev_subln': 1.0 + nrm(ks[11], (N_EVEN, DA_V), 0.02),
        'ev_conv': nrm(ks[12], (N_EVEN, SC_WIDTH, BRANCH_W), SC_WIDTH ** -0.5),
        'od_w_in': nrm(ks[13], (N_ODD, D_MODEL, ow), D_MODEL ** -0.5),
        'od_rpb': nrm(ks[14], (N_ODD, NA_HEADS, 2 * WIN_ROWS - 1, 2 * WIN_COLS - 1), 0.1),
        'od_gate_up': nrm(ks[15], (N_ODD, 2, GLA_RANK, GLA_K_W), GLA_RANK ** -0.5),
        'od_gate_bias': nrm(ks[16], (N_ODD, 2, GLA_K_W), 0.5),
        'od_gnorm': 1.0 + nrm(ks[17], (N_ODD, GLA_DV), 0.02),
    }


def reference(x, c, ctx, c_ctx, w_mod, b_mod, g_pre, g_post, w_out, ev_w_in, ev_lambda, ev_subln, ev_conv,
              od_w_in, od_rpb, od_gate_up, od_gate_bias, od_gnorm):
    seq = x.shape[1]
    cos, sin = axial_rope(seq, x.dtype)
    xc = ctx
    for l in range(DEPTH):
        last = l == DEPTH - 1
        mod = jax.nn.silu(c) @ w_mod[l] + b_mod[l]
        modc = jax.nn.silu(c_ctx) @ w_mod[l] + b_mod[l]
        shift, scale, gate = jnp.split(mod[:, None, :], 3, axis=-1)
        shift_c, scale_c, gate_c = jnp.split(modc, 3, axis=-1)
        h = rmsnorm(x, g_pre[l]) * (1.0 + scale) + shift
        hc = rmsnorm(xc, g_pre[l]) * (1.0 + scale_c) + shift_c
        j = l // 2
        if l % 2 == 0:
            lam_init = 0.8 - 0.6 * math.exp(-0.3 * l)
            u, uc = even_mixer(h, hc, ev_w_in[j], ev_lambda[j], ev_subln[j], ev_conv[j], lam_init, not last, cos, sin)
        else:
            u, uc = odd_mixer(h, hc, od_w_in[j], od_rpb[j], od_gate_up[j], od_gate_bias[j], od_gnorm[j], not last)
        x = x + gate * rmsnorm(u @ w_out[l], g_post[l])
        if not last:
            xc = xc + gate_c * rmsnorm(uc @ w_out[l], g_post[l])
    return x
```

```python
import functools
import math

import numpy as np
import jax
import jax.numpy as jnp
from jax import lax
from jax.experimental import pallas as pl
from jax.experimental.pallas import tpu as pltpu

F32 = jnp.float32
BF16 = jnp.bfloat16

LANES = 128
GRID_W = 64
EXPAND = 2
DA_HEADS = 8
DA_QK = 64
SC_WIDTH = 3
NA_HEADS = 16
NA_DIM = 64
WIN_ROWS = 8
WIN_COLS = 16
GLA_HEADS = 4
GLA_RANK = 16
GLA_TAU = 16.0
GLA_CHUNK = 64
ROPE_BASE = 10000.0
EPS = 1e-6
ROW_SPLIT = 4
ATT_TQ = 256
VMEM_LIMIT = 56 * 1024 * 1024


def _cparams(sem):
    return pltpu.CompilerParams(dimension_semantics=sem, vmem_limit_bytes=VMEM_LIMIT)


def _silu(t):
    return t * jax.nn.sigmoid(t)


def _mod_kernel(c_ref, w_ref, b_ref, o_ref):
    s = _silu(c_ref[...])
    o_ref[0] = jnp.dot(s, w_ref[0], preferred_element_type=F32) + b_ref[0]


def _modulation(cc, w_mod, b_mod):
    depth, d, d3 = w_mod.shape
    nrow = cc.shape[0]
    tn = d
    return pl.pallas_call(
        _mod_kernel,
        grid=(depth, d3 // tn),
        in_specs=[
            pl.BlockSpec((nrow, d), lambda l, j: (0, 0)),
            pl.BlockSpec((1, d, tn), lambda l, j: (l, 0, j)),
            pl.BlockSpec((1, 1, tn), lambda l, j: (l, 0, j)),
        ],
        out_specs=pl.BlockSpec((1, nrow, tn), lambda l, j: (l, 0, j)),
        out_shape=jax.ShapeDtypeStruct((depth, nrow, d3), F32),
        compiler_params=_cparams(("arbitrary", "arbitrary")),
        name="modulation",
    )(cc, w_mod, b_mod.reshape(depth, 1, d3))


def _inproj_kernel(*refs, tm, d, n_lat, rope_tiles, has_lr):
    if has_lr:
        x_ref, mb_ref, mc_ref, g_ref, w_ref, cos_ref, sin_ref, wlr_ref, o_ref, lr_ref, h_scr = refs
    else:
        x_ref, mb_ref, mc_ref, g_ref, w_ref, cos_ref, sin_ref, o_ref, h_scr = refs
    k = pl.program_id(1)
    j = pl.program_id(2)

    @pl.when(j == 0)
    def _():
        xv = x_ref[0]
        y = xv * lax.rsqrt(jnp.mean(xv * xv, axis=-1, keepdims=True) + EPS) * g_ref[...]
        row = k * tm + lax.broadcasted_iota(jnp.int32, (tm, 1), 0)
        is_ctx = row >= n_lat
        mb = mb_ref[0]
        shift = jnp.where(is_ctx, mc_ref[:, 0:d], mb[:, 0:d])
        scale = jnp.where(is_ctx, mc_ref[:, d:2 * d], mb[:, d:2 * d])
        h_scr[...] = (y * (1.0 + scale) + shift).astype(BF16)
        if has_lr:
            lr_ref[0] = jnp.dot(h_scr[...], wlr_ref[...], preferred_element_type=F32)

    acc = jnp.dot(h_scr[...], w_ref[...], preferred_element_type=F32)
    tn = acc.shape[1]
    if rope_tiles:
        is_rope = functools.reduce(jnp.logical_or, [j == t for t in rope_tiles])

        @pl.when(is_rope)
        def _():
            reps = tn // LANES
            cos = jnp.concatenate([cos_ref[...]] * reps, axis=1)
            sin = jnp.concatenate([sin_ref[...]] * reps, axis=1)
            lane = lax.broadcasted_iota(jnp.int32, (1, tn), 1)
            lo = (lane & 16) == 0
            partner = jnp.where(lo, pltpu.roll(acc, tn - 16, 1), pltpu.roll(acc, 16, 1))
            o_ref[0] = (acc * cos + partner * sin).astype(o_ref.dtype)

        @pl.when(jnp.logical_not(is_rope))
        def _():
            o_ref[0] = acc.astype(o_ref.dtype)
    else:
        o_ref[0] = acc.astype(o_ref.dtype)


def _inproj(xall, modb, modc, g, w, cos_t, sin_t, w_lr, *, n_lat, rope_tiles, tn):
    b, r, d = xall.shape
    nc = w.shape[1]
    tm = r // ROW_SPLIT
    has_lr = w_lr is not None
    in_specs = [
        pl.BlockSpec((1, tm, d), lambda bi, k, j: (bi, k, 0)),
        pl.BlockSpec((1, 1, 3 * d), lambda bi, k, j: (bi, 0, 0)),
        pl.BlockSpec((1, 3 * d), lambda bi, k, j: (0, 0)),
        pl.BlockSpec((1, d), lambda bi, k, j: (0, 0)),
        pl.BlockSpec((d, tn), lambda bi, k, j: (0, j)),
        pl.BlockSpec((tm, LANES), lambda bi, k, j: (k, 0)),
        pl.BlockSpec((tm, LANES), lambda bi, k, j: (k, 0)),
    ]
    args = [xall, modb, modc, g.reshape(1, d), w, cos_t, sin_t]
    out_specs = [pl.BlockSpec((1, tm, tn), lambda bi, k, j: (bi, k, j))]
    out_shape = [jax.ShapeDtypeStruct((b, r, nc), BF16)]
    if has_lr:
        in_specs.append(pl.BlockSpec((d, LANES), lambda bi, k, j: (0, 0)))
        args.append(w_lr)
        out_specs.append(pl.BlockSpec((1, tm, LANES), lambda bi, k, j: (bi, k, 0)))
        out_shape.append(jax.ShapeDtypeStruct((b, r, LANES), F32))
    res = pl.pallas_call(
        functools.partial(_inproj_kernel, tm=tm, d=d, n_lat=n_lat, rope_tiles=rope_tiles, has_lr=has_lr),
        grid=(b, ROW_SPLIT, nc // tn),
        in_specs=in_specs,
        out_specs=out_specs,
        out_shape=out_shape,
        scratch_shapes=[pltpu.VMEM((tm, d), BF16)],
        compiler_params=_cparams(("arbitrary", "arbitrary", "arbitrary")),
        name="inproj",
    )(*args)
    return res if has_lr else (res[0], None)


def _outproj_kernel(ua_ref, ub_ref, w1_ref, w2_ref, x_ref, mb_ref, mc_ref, g_ref, o_ref, *, tm, d, n_lat):
    k = pl.program_id(1)
    y = jnp.dot(ua_ref[0], w1_ref[...], preferred_element_type=F32)
    y = y + jnp.dot(ub_ref[0], w2_ref[...], preferred_element_type=F32)
    y = y * lax.rsqrt(jnp.mean(y * y, axis=-1, keepdims=True) + EPS) * g_ref[...]
    row = k * tm + lax.broadcasted_iota(jnp.int32, (tm, 1), 0)
    gate = jnp.where(row >= n_lat, mc_ref[:, 2 * d:3 * d], mb_ref[0][:, 2 * d:3 * d])
    o_ref[0] = x_ref[0] + gate * y


def _outproj(ua, ub, w, xall, modb, modc, g, *, n_lat):
    b, r, d = xall.shape
    bw = ua.shape[2]
    tm = r // ROW_SPLIT
    return pl.pallas_call(
        functools.partial(_outproj_kernel, tm=tm, d=d, n_lat=n_lat),
        grid=(b, ROW_SPLIT),
        in_specs=[
            pl.BlockSpec((1, tm, bw), lambda bi, k: (bi, k, 0)),
            pl.BlockSpec((1, tm, bw), lambda bi, k: (bi, k, 0)),
            pl.BlockSpec((bw, d), lambda bi, k: (0, 0)),
            pl.BlockSpec((bw, d), lambda bi, k: (1, 0)),
            pl.BlockSpec((1, tm, d), lambda bi, k: (bi, k, 0)),
            pl.BlockSpec((1, 1, 3 * d), lambda bi, k: (bi, 0, 0)),
            pl.BlockSpec((1, 3 * d), lambda bi, k: (0, 0)),
            pl.BlockSpec((1, d), lambda bi, k: (0, 0)),
        ],
        out_specs=pl.BlockSpec((1, tm, d), lambda bi, k: (bi, k, 0)),
        out_shape=jax.ShapeDtypeStruct((b, r, d), F32),
        input_output_aliases={4: 0},
        compiler_params=_cparams(("arbitrary", "arbitrary")),
        name="outproj",
    )(ua, ub, w, w, xall, modb, modc, g.reshape(1, d))


def _dattn_kernel(lam_ref, sub_ref, q_ref, k_ref, v_ref, g_ref, o_ref, *, n_lat, tq, lam_init):
    i = pl.program_id(2)
    lp = lam_ref[...]
    lam = (jnp.exp(jnp.sum(lp[0:1] * lp[1:2], axis=-1, keepdims=True))
           - jnp.exp(jnp.sum(lp[2:3] * lp[3:4], axis=-1, keepdims=True)) + lam_init)
    q = q_ref[0]
    lane = lax.broadcasted_iota(jnp.int32, (1, LANES), 1)
    zero = jnp.zeros_like(q)
    q0 = jnp.where(lane < DA_QK, q, zero)
    q1 = jnp.where(lane >= DA_QK, q, zero)
    nt = (((1,), (1,)), ((), ()))

    def attend(kk, vv):
        s0 = lax.dot_general(q0, kk, nt, preferred_element_type=F32)
        s1 = lax.dot_general(q1, kk, nt, preferred_element_type=F32)
        e0 = jnp.exp(s0 - jnp.max(s0, axis=-1, keepdims=True))
        e1 = jnp.exp(s1 - jnp.max(s1, axis=-1, keepdims=True))
        r0 = 1.0 / jnp.sum(e0, axis=-1, keepdims=True)
        r1 = lam / jnp.sum(e1, axis=-1, keepdims=True)
        a = (e0 * r0 - e1 * r1).astype(BF16)
        return jnp.dot(a, vv, preferred_element_type=F32)

    def finish(o):
        o = o * lax.rsqrt(jnp.mean(o * o, axis=-1, keepdims=True) + EPS) * sub_ref[...] * (1.0 - lam_init)
        o_ref[0] = (o * _silu(g_ref[0].astype(F32))).astype(o_ref.dtype)

    @pl.when(i < n_lat // tq)
    def _():
        finish(attend(k_ref[0], v_ref[0]))

    @pl.when(i >= n_lat // tq)
    def _():
        finish(attend(k_ref[0, n_lat:, :], v_ref[0, n_lat:, :]))


def _diff_attention(proj, lam_p, subln, *, n_lat, lam_init):
    b, r, _ = proj.shape
    h = DA_HEADS
    tq = ATT_TQ
    return pl.pallas_call(
        functools.partial(_dattn_kernel, n_lat=n_lat, tq=tq, lam_init=lam_init),
        grid=(b, h, r // tq),
        in_specs=[
            pl.BlockSpec(lam_p.shape, lambda bi, hi, i: (0, 0)),
            pl.BlockSpec((1, LANES), lambda bi, hi, i: (0, 0)),
            pl.BlockSpec((1, tq, LANES), lambda bi, hi, i: (bi, i, 2 * h + hi)),
            pl.BlockSpec((1, r, LANES), lambda bi, hi, i: (bi, 0, hi)),
            pl.BlockSpec((1, r, LANES), lambda bi, hi, i: (bi, 0, h + hi)),
            pl.BlockSpec((1, tq, LANES), lambda bi, hi, i: (bi, i, 3 * h + hi)),
        ],
        out_specs=pl.BlockSpec((1, tq, LANES), lambda bi, hi, i: (bi, i, hi)),
        out_shape=jax.ShapeDtypeStruct((b, r, h * LANES), BF16),
        compiler_params=_cparams(("arbitrary", "arbitrary", "arbitrary")),
        name="diff_attention",
    )(lam_p, subln.reshape(1, LANES), proj, proj, proj, proj)


def _sconv_kernel(h_ref, b_ref, c_ref, g_ref, w_ref, o_ref, *, n_lat):
    r = h_ref.shape[1]
    p = c_ref[0].astype(F32) * h_ref[0].astype(F32)
    row = lax.broadcasted_iota(jnp.int32, (r, 1), 0)
    prev = jnp.where((row == 0) | (row == n_lat), 0.0, pltpu.roll(p, 1, 0))
    nxt = jnp.where((row == n_lat - 1) | (row == r - 1), 0.0, pltpu.roll(p, r - 1, 0))
    w = w_ref[...]
    conv = w[0:1] * prev + w[1:2] * p + w[2:3] * nxt
    o_ref[0] = (b_ref[0].astype(F32) * conv * _silu(g_ref[0].astype(F32))).astype(o_ref.dtype)


def _short_conv(proj, conv_w, *, n_lat):
    b, r, _ = proj.shape
    nb = conv_w.shape[1] // LANES
    base = 4 * nb
    spec = lambda off: pl.BlockSpec((1, r, LANES), lambda bi, j: (bi, 0, off + j))
    return pl.pallas_call(
        functools.partial(_sconv_kernel, n_lat=n_lat),
        grid=(b, nb),
        in_specs=[spec(base), spec(base + nb), spec(base + 2 * nb), spec(base + 3 * nb),
                  pl.BlockSpec((SC_WIDTH, LANES), lambda bi, j: (0, j))],
        out_specs=pl.BlockSpec((1, r, LANES), lambda bi, j: (bi, 0, j)),
        out_shape=jax.ShapeDtypeStruct((b, r, nb * LANES), BF16),
        compiler_params=_cparams(("arbitrary", "arbitrary")),
        name="short_conv",
    )(proj, proj, proj, proj, conv_w)


def _natten_kernel(q_ref, k_ref, v_ref, g_ref, tb_ref, o_ref, *, n_lat, n_rows):
    lane = lax.broadcasted_iota(jnp.int32, (1, LANES), 1)
    head_masks = [lane < NA_DIM, lane >= NA_DIM]
    nt = (((1,), (1,)), ((), ()))
    win = WIN_ROWS * GRID_W
    kc = k_ref[0, n_lat:, :]
    vc = v_ref[0, n_lat:, :]

    def softmax_pv(parts):
        m = functools.reduce(jnp.maximum, [jnp.max(s, axis=-1, keepdims=True) for s, _ in parts])
        es = [jnp.exp(s - m) for s, _ in parts]
        l = functools.reduce(jnp.add, [jnp.sum(e, axis=-1, keepdims=True) for e in es])
        o = functools.reduce(jnp.add, [jnp.dot(e.astype(BF16), vv, preferred_element_type=F32)
                                       for e, (_, vv) in zip(es, parts)])
        return o / l

    def row_body(rr, carry):
        rs = jnp.clip(rr - WIN_ROWS // 2, 0, n_rows - WIN_ROWS)
        off = rs - rr + (WIN_ROWS - 1)
        q = q_ref[0, pl.ds(pl.multiple_of(rr * GRID_W, GRID_W), GRID_W), :]
        kw = k_ref[0, pl.ds(pl.multiple_of(rs * GRID_W, GRID_W), win), :]
        vw = v_ref[0, pl.ds(pl.multiple_of(rs * GRID_W, GRID_W), win), :]
        out = jnp.zeros((GRID_W, LANES), F32)
        for hl in range(2):
            qh = jnp.where(head_masks[hl], q, jnp.zeros_like(q))
            s_loc = lax.dot_general(qh, kw, nt, preferred_element_type=F32) + tb_ref[hl, off]
            s_ctx = lax.dot_general(qh, kc, nt, preferred_element_type=F32)
            o = softmax_pv([(s_loc, vw), (s_ctx, vc)])
            out = jnp.where(head_masks[hl], o, out)
        gt = g_ref[0, pl.ds(pl.multiple_of(rr * GRID_W, GRID_W), GRID_W), :].astype(F32)
        o_ref[0, pl.ds(pl.multiple_of(rr * GRID_W, GRID_W), GRID_W), :] = (out * _silu(gt)).astype(o_ref.dtype)
        return carry

    lax.fori_loop(0, n_rows, row_body, 0)

    qc = q_ref[0, n_lat:, :]
    outc = jnp.zeros(qc.shape, F32)
    for hl in range(2):
        qh = jnp.where(head_masks[hl], qc, jnp.zeros_like(qc))
        s = lax.dot_general(qh, kc, nt, preferred_element_type=F32)
        outc = jnp.where(head_masks[hl], softmax_pv([(s, vc)]), outc)
    o_ref[0, n_lat:, :] = (outc * _silu(g_ref[0, n_lat:, :].astype(F32))).astype(o_ref.dtype)


def _neighbourhood_attention(proj, tb, *, n_lat, cols):
    b, r, _ = proj.shape
    nb = NA_HEADS // 2
    kb, vb, qb, gb = (c // LANES for c in cols)
    spec = lambda off: pl.BlockSpec((1, r, LANES), lambda bi, j: (bi, 0, off + j))
    return pl.pallas_call(
        functools.partial(_natten_kernel, n_lat=n_lat, n_rows=n_lat // GRID_W),
        grid=(b, nb),
        in_specs=[spec(qb), spec(kb), spec(vb), spec(gb),
                  pl.BlockSpec((2,) + tb.shape[1:], lambda bi, j: (j, 0, 0, 0))],
        out_specs=pl.BlockSpec((1, r, LANES), lambda bi, j: (bi, 0, j)),
        out_shape=jax.ShapeDtypeStruct((b, r, nb * LANES), BF16),
        compiler_params=_cparams(("arbitrary", "arbitrary")),
        name="neighbourhood_attention",
    )(proj, proj, proj, proj, tb)


def _natten_bias_table(rpb):
    qcol = np.arange(GRID_W)[:, None]
    kcol = np.arange(GRID_W)[None, :]
    win_start = np.clip(qcol - WIN_COLS // 2, 0, GRID_W - WIN_COLS)
    valid = (kcol >= win_start) & (kcol < win_start + WIN_COLS)
    col_off = np.clip(kcol - qcol + (WIN_COLS - 1), 0, 2 * WIN_COLS - 2)
    row_idx = np.arange(WIN_ROWS)[:, None] + np.arange(WIN_ROWS)[None, :]
    t = rpb.astype(F32)[:, row_idx][:, :, :, col_off]
    t = jnp.where(jnp.asarray(valid)[None, None, None], t, -jnp.inf)
    t = jnp.transpose(t, (0, 1, 3, 2, 4))
    return t.reshape(rpb.shape[0], WIN_ROWS, GRID_W, WIN_ROWS * GRID_W)


def _gla_kernel(q_ref, k_ref, v_ref, g_ref, lr_ref, gu_ref, gb_ref, gn_ref, o_ref, of_scr, st_scr,
                *, n_lat, n_ctx, qscale):
    ch = GLA_CHUNK
    mid = ch // 2
    nt = (((1,), (1,)), ((), ()))
    tn = (((0,), (0,)), ((), ()))
    trow = lax.broadcasted_iota(jnp.int32, (ch, ch), 0)
    tcol = lax.broadcasted_iota(jnp.int32, (ch, ch), 1)
    srow = lax.broadcasted_iota(jnp.int32, (ch, 1), 0)

    def cumsum_rows(g, reverse):
        acc = g
        sh = 1
        while sh < ch:
            if reverse:
                acc = acc + jnp.where(srow < ch - sh, pltpu.roll(acc, ch - sh, 0), 0.0)
            else:
                acc = acc + jnp.where(srow >= sh, pltpu.roll(acc, sh, 0), 0.0)
            sh *= 2
        return acc

    def chunk(start, dirn):
        reverse = dirn == 1
        rows = pl.ds(start if isinstance(start, int) else pl.multiple_of(start, ch), ch)
        z = jnp.dot(lr_ref[0, rows, :].astype(BF16), gu_ref[dirn, 0], preferred_element_type=F32) + gb_ref[dirn, 0]
        g = jax.nn.log_sigmoid(z) * (1.0 / GLA_TAU)
        bcum = cumsum_rows(g, reverse)
        b_end = bcum[0:1] if reverse else bcum[ch - 1:ch]
        rho = bcum[mid:mid + 1]
        up = jnp.exp(bcum - rho)
        dn = jnp.exp(rho - bcum)
        qf = q_ref[0, rows, :].astype(F32) * qscale
        kf = k_ref[0, rows, :].astype(F32)
        vv = v_ref[0, rows, :]
        st = st_scr[...]
        o = lax.dot_general((qf * (up * jnp.exp(rho))).astype(BF16), st.astype(BF16), nt,
                            preferred_element_type=F32)
        att = lax.dot_general((qf * up).astype(BF16), (kf * dn).astype(BF16), nt, preferred_element_type=F32)
        keep = (tcol >= trow) if reverse else (tcol <= trow)
        att = jnp.where(keep, att, 0.0)
        o = o + jnp.dot(att.astype(BF16), vv, preferred_element_type=F32)
        kd = (kf * (dn * jnp.exp(b_end - rho))).astype(BF16)
        st_scr[...] = st * jnp.exp(b_end) + lax.dot_general(vv, kd, tn, preferred_element_type=F32)
        if not reverse:
            of_scr[rows, :] = o
        else:
            o = o + of_scr[rows, :]
            o = o * lax.rsqrt(jnp.mean(o * o, axis=-1, keepdims=True) + EPS) * gn_ref[...]
            o_ref[0, rows, :] = (o * _silu(g_ref[0, rows, :].astype(F32))).astype(o_ref.dtype)

    n_cc = n_ctx // ch
    n_lc = n_lat // ch
    st_scr[...] = jnp.zeros_like(st_scr)
    for c in range(n_cc):
        chunk(n_lat + c * ch, 0)
    lax.fori_loop(0, n_lc, lambda c, carry: (chunk(c * ch, 0), carry)[1], 0)
    st_scr[...] = jnp.zeros_like(st_scr)
    for c in range(n_cc - 1, -1, -1):
        chunk(n_lat + c * ch, 1)
    lax.fori_loop(0, n_lc, lambda c, carry: (chunk((n_lc - 1 - c) * ch, 1), carry)[1], 0)


def _gla(proj, lr, gu, gb, gnorm, *, n_lat, cols):
    b, r, _ = proj.shape
    nh = GLA_HEADS
    dk = LANES
    dv = 2 * LANES
    kcol, vcol, qcol, gcol = cols
    return pl.pallas_call(
        functools.partial(_gla_kernel, n_lat=n_lat, n_ctx=r - n_lat, qscale=dk ** -0.5),
        grid=(b, nh),
        in_specs=[
            pl.BlockSpec((1, r, dk), lambda bi, hi: (bi, 0, qcol // dk + hi)),
            pl.BlockSpec((1, r, dk), lambda bi, hi: (bi, 0, kcol // dk + hi)),
            pl.BlockSpec((1, r, dv), lambda bi, hi: (bi, 0, vcol // dv + hi)),
            pl.BlockSpec((1, r, dv), lambda bi, hi: (bi, 0, gcol // dv + hi)),
            pl.BlockSpec((1, r, LANES), lambda bi, hi: (bi, 0, 0)),
            pl.BlockSpec((2, 1, LANES, dk), lambda bi, hi: (0, hi, 0, 0)),
            pl.BlockSpec((2, 1, 1, dk), lambda bi, hi: (0, hi, 0, 0)),
            pl.BlockSpec((1, dv), lambda bi, hi: (0, 0)),
        ],
        out_specs=pl.BlockSpec((1, r, dv), lambda bi, hi: (bi, 0, hi)),
        out_shape=jax.ShapeDtypeStruct((b, r, nh * dv), BF16),
        scratch_shapes=[pltpu.VMEM((r, dv), F32), pltpu.VMEM((dv, dk), F32)],
        compiler_params=_cparams(("arbitrary", "arbitrary")),
        name="gla",
    )(proj, proj, proj, proj, lr, gu, gb, gnorm.reshape(1, dv))


def _rope_tables(n_lat, n_ctx):
    half = DA_QK // 2
    inv = 1.0 / (ROPE_BASE ** (jnp.arange(0, half, 2, dtype=F32) / half))
    t = jnp.arange(n_lat)
    row = (t // GRID_W).astype(F32)[:, None] * inv
    col = (t % GRID_W).astype(F32)[:, None] * inv
    ang = jnp.concatenate([row, row, col, col], axis=-1)
    sign = jnp.asarray(np.where((np.arange(DA_QK) & 16) == 0, -1.0, 1.0), F32)
    cos = jnp.concatenate([jnp.cos(ang), jnp.ones((n_ctx, DA_QK), F32)], axis=0)
    sin = jnp.concatenate([jnp.sin(ang) * sign, jnp.zeros((n_ctx, DA_QK), F32)], axis=0)
    return jnp.concatenate([cos, cos], axis=1), jnp.concatenate([sin, sin], axis=1)


def kernel(x, c, ctx, c_ctx, w_mod, b_mod, g_pre, g_post, w_out, ev_w_in, ev_lambda, ev_subln, ev_conv,
           od_w_in, od_rpb, od_gate_up, od_gate_bias, od_gnorm):
    b, s, d = x.shape
    n_ctx = ctx.shape[1]
    depth = w_mod.shape[0]
    bw = EXPAND * d // 2
    assert bw == DA_HEADS * LANES == NA_HEADS * NA_DIM and (s + n_ctx) % (ROW_SPLIT * 16) == 0
    assert s % ATT_TQ == 0 and n_ctx % ATT_TQ == 0 and s % GRID_W == 0 and n_ctx % GLA_CHUNK == 0

    xall = jnp.concatenate([x, ctx], axis=1)
    nrow = -(-(b + 1) // 8) * 8
    cc = jnp.zeros((nrow, d), F32).at[:b].set(c).at[b].set(c_ctx)
    mod = _modulation(cc, w_mod, b_mod)
    cos_t, sin_t = _rope_tables(s, n_ctx)

    gk = bw // 2
    od_lr0 = 3 * bw + gk
    od_cols = np.cumsum([0, bw, bw, gk, bw, bw, gk, bw])
    na_cols = (int(od_cols[0]), int(od_cols[1]), int(od_cols[4]), int(od_cols[6]))
    gla_cols = (int(od_cols[2]), int(od_cols[3]), int(od_cols[5]), int(od_cols[7]))

    for l in range(depth):
        j = l // 2
        modb = mod[l, :b].reshape(b, 1, 3 * d)
        modc = mod[l, b:b + 1]
        if l % 2 == 0:
            lam_init = 0.8 - 0.6 * math.exp(-0.3 * l)
            w = ev_w_in[j]
            w = w.at[:, 2 * bw:3 * bw].multiply(DA_QK ** -0.5).astype(BF16)
            proj, _ = _inproj(xall, modb, modc, g_pre[l], w, cos_t, sin_t, None,
                              n_lat=s, rope_tiles=(0, 2), tn=bw)
            ua = _diff_attention(proj, ev_lambda[j], ev_subln[j], n_lat=s, lam_init=lam_init)
            ub = _short_conv(proj, ev_conv[j], n_lat=s)
        else:
            w = od_w_in[j]
            w_lr = jnp.pad(w[:, od_lr0:od_lr0 + 2 * GLA_RANK], ((0, 0), (0, LANES - 2 * GLA_RANK))).astype(BF16)
            w = jnp.concatenate([w[:, :od_lr0], w[:, od_lr0 + 2 * GLA_RANK:]], axis=1)
            w = w.at[:, na_cols[2]:na_cols[2] + bw].multiply(NA_DIM ** -0.5).astype(BF16)
            proj, lr = _inproj(xall, modb, modc, g_pre[l], w, cos_t, sin_t, w_lr,
                               n_lat=s, rope_tiles=(), tn=bw)
            tb = _natten_bias_table(od_rpb[j])
            ua = _neighbourhood_attention(proj, tb, n_lat=s, cols=na_cols)
            gu = jnp.zeros((2, GLA_HEADS, LANES, LANES), F32)
            gup = od_gate_up[j].reshape(2, GLA_RANK, GLA_HEADS, LANES).transpose(0, 2, 1, 3)
            gu = gu.at[0, :, :GLA_RANK].set(gup[0]).at[1, :, GLA_RANK:2 * GLA_RANK].set(gup[1]).astype(BF16)
            gb = od_gate_bias[j].reshape(2, GLA_HEADS, 1, LANES)
            ub = _gla(proj, lr, gu, gb, od_gnorm[j], n_lat=s, cols=gla_cols)
        xall = _outproj(ua, ub, w_out[l].astype(BF16), xall, modb, modc, g_post[l], n_lat=s)
    return xall[:, :s]
```

```python
import functools
import math

import numpy as np
import jax
import jax.numpy as jnp
from jax import lax
from jax.experimental import pallas as pl
from jax.experimental.pallas import tpu as pltpu

F32 = jnp.float32
BF16 = jnp.bfloat16

LANES = 128
GRID_W = 64
EXPAND = 2
DA_HEADS = 8
DA_QK = 64
SC_WIDTH = 3
NA_HEADS = 16
NA_DIM = 64
WIN_ROWS = 8
WIN_COLS = 16
GLA_HEADS = 4
GLA_RANK = 16
GLA_TAU = 16.0
GLA_CHUNK = 64
ROPE_BASE = 10000.0
EPS = 1e-6
ROW_SPLIT = 4
ATT_TQ = 256
NA_GROUP = 4
GLA_GROUP = 4
VMEM_LIMIT = 56 * 1024 * 1024


def _cparams(sem):
    return pltpu.CompilerParams(dimension_semantics=sem, vmem_limit_bytes=VMEM_LIMIT)


def _silu(t):
    return t * jax.nn.sigmoid(t)


def _mod_kernel(c_ref, w_ref, b_ref, o_ref):
    s = _silu(c_ref[...])
    o_ref[0] = jnp.dot(s, w_ref[0], preferred_element_type=F32) + b_ref[0]


def _modulation(cc, w_mod, b_mod):
    depth, d, d3 = w_mod.shape
    nrow = cc.shape[0]
    tn = d
    return pl.pallas_call(
        _mod_kernel,
        grid=(depth, d3 // tn),
        in_specs=[
            pl.BlockSpec((nrow, d), lambda l, j: (0, 0)),
            pl.BlockSpec((1, d, tn), lambda l, j: (l, 0, j)),
            pl.BlockSpec((1, 1, tn), lambda l, j: (l, 0, j)),
        ],
        out_specs=pl.BlockSpec((1, nrow, tn), lambda l, j: (l, 0, j)),
        out_shape=jax.ShapeDtypeStruct((depth, nrow, d3), F32),
        compiler_params=_cparams(("arbitrary", "arbitrary")),
        name="modulation",
    )(cc, w_mod, b_mod.reshape(depth, 1, d3))


def _inproj_kernel(*refs, tm, d, n_lat, rope_tiles, has_lr):
    x_ref, mb_ref, mc_ref, g_ref, w_ref = refs[:5]
    rest = list(refs[5:])
    cos_ref, sin_ref = (rest.pop(0), rest.pop(0)) if rope_tiles else (None, None)
    wlr_ref = rest.pop(0) if has_lr else None
    o_ref = rest.pop(0)
    lr_ref = rest.pop(0) if has_lr else None
    h_scr = rest.pop(0)
    k = pl.program_id(1)
    j = pl.program_id(2)

    @pl.when(j == 0)
    def _():
        xv = x_ref[0]
        y = xv * lax.rsqrt(jnp.mean(xv * xv, axis=-1, keepdims=True) + EPS) * g_ref[...]
        row = k * tm + lax.broadcasted_iota(jnp.int32, (tm, 1), 0)
        is_ctx = row >= n_lat
        mb = mb_ref[0]
        shift = jnp.where(is_ctx, mc_ref[:, 0:d], mb[:, 0:d])
        scale = jnp.where(is_ctx, mc_ref[:, d:2 * d], mb[:, d:2 * d])
        h_scr[...] = (y * (1.0 + scale) + shift).astype(BF16)
        if has_lr:
            lr_ref[0] = jnp.dot(h_scr[...], wlr_ref[...], preferred_element_type=F32)

    acc = jnp.dot(h_scr[...], w_ref[...], preferred_element_type=F32)
    tn = acc.shape[1]
    if rope_tiles:
        is_rope = functools.reduce(jnp.logical_or, [j == t for t in rope_tiles])

        @pl.when(is_rope)
        def _():
            reps = tn // LANES
            cos = jnp.concatenate([cos_ref[0]] * reps, axis=1)
            sin = jnp.concatenate([sin_ref[0]] * reps, axis=1)
            lane = lax.broadcasted_iota(jnp.int32, (1, tn), 1)
            lo = (lane & 16) == 0
            partner = jnp.where(lo, pltpu.roll(acc, tn - 16, 1), pltpu.roll(acc, 16, 1))
            o_ref[0] = (acc * cos + partner * sin).astype(o_ref.dtype)

        @pl.when(jnp.logical_not(is_rope))
        def _():
            o_ref[0] = acc.astype(o_ref.dtype)
    else:
        o_ref[0] = acc.astype(o_ref.dtype)


def _inproj(xall, modb, modc, g, w, rope, w_lr, *, n_lat, tn):
    b, r, d = xall.shape
    nc = w.shape[1]
    tm = r // ROW_SPLIT
    has_lr = w_lr is not None
    in_specs = [
        pl.BlockSpec((1, tm, d), lambda bi, k, j: (bi, k, 0)),
        pl.BlockSpec((1, 1, 3 * d), lambda bi, k, j: (bi, 0, 0)),
        pl.BlockSpec((1, 3 * d), lambda bi, k, j: (0, 0)),
        pl.BlockSpec((1, d), lambda bi, k, j: (0, 0)),
        pl.BlockSpec((d, tn), lambda bi, k, j: (0, j)),
    ]
    args = [xall, modb, modc, g.reshape(1, d), w]
    rope_tiles = ()
    if rope is not None:
        cos_t, sin_t, k_tile, q_tile = rope
        rope_tiles = (k_tile, q_tile)
        table = pl.BlockSpec((1, tm, LANES), lambda bi, k, j: (jnp.where(j == q_tile, 1, 0), k, 0))
        in_specs += [table, table]
        args += [cos_t, sin_t]
    out_specs = [pl.BlockSpec((1, tm, tn), lambda bi, k, j: (bi, k, j))]
    out_shape = [jax.ShapeDtypeStruct((b, r, nc), BF16)]
    if has_lr:
        in_specs.append(pl.BlockSpec((d, LANES), lambda bi, k, j: (0, 0)))
        args.append(w_lr)
        out_specs.append(pl.BlockSpec((1, tm, LANES), lambda bi, k, j: (bi, k, 0)))
        out_shape.append(jax.ShapeDtypeStruct((b, r, LANES), F32))
    res = pl.pallas_call(
        functools.partial(_inproj_kernel, tm=tm, d=d, n_lat=n_lat, rope_tiles=rope_tiles, has_lr=has_lr),
        grid=(b, ROW_SPLIT, nc // tn),
        in_specs=in_specs,
        out_specs=out_specs,
        out_shape=out_shape,
        scratch_shapes=[pltpu.VMEM((tm, d), BF16)],
        compiler_params=_cparams(("arbitrary", "arbitrary", "arbitrary")),
        name="inproj",
    )(*args)
    return res if has_lr else (res[0], None)


def _outproj_kernel(ua_ref, ub_ref, w1_ref, w2_ref, x_ref, mb_ref, mc_ref, g_ref, o_ref, *, tm, d, n_lat):
    k = pl.program_id(1)
    y = jnp.dot(ua_ref[0], w1_ref[...], preferred_element_type=F32)
    y = y + jnp.dot(ub_ref[0], w2_ref[...], preferred_element_type=F32)
    y = y * lax.rsqrt(jnp.mean(y * y, axis=-1, keepdims=True) + EPS) * g_ref[...]
    row = k * tm + lax.broadcasted_iota(jnp.int32, (tm, 1), 0)
    gate = jnp.where(row >= n_lat, mc_ref[:, 2 * d:3 * d], mb_ref[0][:, 2 * d:3 * d])
    o_ref[0] = x_ref[0] + gate * y


def _outproj(ua, ub, w, xall, modb, modc, g, *, n_lat):
    b, r, d = xall.shape
    bw = ua.shape[2]
    tm = r // ROW_SPLIT
    return pl.pallas_call(
        functools.partial(_outproj_kernel, tm=tm, d=d, n_lat=n_lat),
        grid=(b, ROW_SPLIT),
        in_specs=[
            pl.BlockSpec((1, tm, bw), lambda bi, k: (bi, k, 0)),
            pl.BlockSpec((1, tm, bw), lambda bi, k: (bi, k, 0)),
            pl.BlockSpec((bw, d), lambda bi, k: (0, 0)),
            pl.BlockSpec((bw, d), lambda bi, k: (1, 0)),
            pl.BlockSpec((1, tm, d), lambda bi, k: (bi, k, 0)),
            pl.BlockSpec((1, 1, 3 * d), lambda bi, k: (bi, 0, 0)),
            pl.BlockSpec((1, 3 * d), lambda bi, k: (0, 0)),
            pl.BlockSpec((1, d), lambda bi, k: (0, 0)),
        ],
        out_specs=pl.BlockSpec((1, tm, d), lambda bi, k: (bi, k, 0)),
        out_shape=jax.ShapeDtypeStruct((b, r, d), F32),
        input_output_aliases={4: 0},
        compiler_params=_cparams(("arbitrary", "arbitrary")),
        name="outproj",
    )(ua, ub, w, w, xall, modb, modc, g.reshape(1, d))


def _dattn_kernel(lam_ref, sub_ref, q_ref, k_ref, vt_ref, g_ref, o_ref, *, n_lat, tq, lam_init):
    i = pl.program_id(2)
    lp = lam_ref[...]
    lam = (jnp.exp(jnp.sum(lp[0:1] * lp[1:2], axis=-1, keepdims=True))
           - jnp.exp(jnp.sum(lp[2:3] * lp[3:4], axis=-1, keepdims=True)) + lam_init)
    q = q_ref[0]
    lane = lax.broadcasted_iota(jnp.int32, (1, LANES), 1)
    zero = jnp.zeros_like(q)
    qs = [jnp.where(lane < DA_QK, q, zero), jnp.where(lane >= DA_QK, q, zero)]
    nt = (((1,), (1,)), ((), ()))

    def attend(kk, vt):
        st = [lax.dot_general(kk, qm, nt, preferred_element_type=F32) for qm in qs]
        es = [jnp.exp2(s - jnp.max(s, axis=0, keepdims=True)) for s in st]
        ls = [jnp.sum(e, axis=0, keepdims=True) for e in es]
        ot = [jnp.dot(vt, e.astype(BF16), preferred_element_type=F32) for e in es]
        return (ot[0] * (1.0 / ls[0]) - ot[1] * (lam / ls[1])).T

    def finish(o):
        o = o * lax.rsqrt(jnp.mean(o * o, axis=-1, keepdims=True) + EPS) * sub_ref[...] * (1.0 - lam_init)
        o_ref[0] = (o * _silu(g_ref[0].astype(F32))).astype(o_ref.dtype)

    @pl.when(i < n_lat // tq)
    def _():
        finish(attend(k_ref[0], vt_ref[0, 0]))

    @pl.when(i >= n_lat // tq)
    def _():
        finish(attend(k_ref[0, n_lat:, :], vt_ref[0, 0, :, n_lat:]))


def _diff_attention(proj, vt, lam_p, subln, *, n_lat, lam_init):
    b, r, _ = proj.shape
    h = DA_HEADS
    tq = ATT_TQ
    return pl.pallas_call(
        functools.partial(_dattn_kernel, n_lat=n_lat, tq=tq, lam_init=lam_init),
        grid=(b, h, r // tq),
        in_specs=[
            pl.BlockSpec(lam_p.shape, lambda bi, hi, i: (0, 0)),
            pl.BlockSpec((1, LANES), lambda bi, hi, i: (0, 0)),
            pl.BlockSpec((1, tq, LANES), lambda bi, hi, i: (bi, i, 2 * h + hi)),
            pl.BlockSpec((1, r, LANES), lambda bi, hi, i: (bi, 0, hi)),
            pl.BlockSpec((1, 1, LANES, r), lambda bi, hi, i: (bi, hi, 0, 0)),
            pl.BlockSpec((1, tq, LANES), lambda bi, hi, i: (bi, i, 3 * h + hi)),
        ],
        out_specs=pl.BlockSpec((1, tq, LANES), lambda bi, hi, i: (bi, i, hi)),
        out_shape=jax.ShapeDtypeStruct((b, r, h * LANES), BF16),
        compiler_params=_cparams(("arbitrary", "arbitrary", "arbitrary")),
        name="diff_attention",
    )(lam_p, subln.reshape(1, LANES), proj, proj, vt, proj)


def _sconv_kernel(h_ref, b_ref, c_ref, g_ref, w_ref, o_ref, *, n_lat):
    r = h_ref.shape[1]
    p = c_ref[0].astype(F32) * h_ref[0].astype(F32)
    row = lax.broadcasted_iota(jnp.int32, (r, 1), 0)
    prev = jnp.where((row == 0) | (row == n_lat), 0.0, pltpu.roll(p, 1, 0))
    nxt = jnp.where((row == n_lat - 1) | (row == r - 1), 0.0, pltpu.roll(p, r - 1, 0))
    w = w_ref[...]
    conv = w[0:1] * prev + w[1:2] * p + w[2:3] * nxt
    o_ref[0] = (b_ref[0].astype(F32) * conv * _silu(g_ref[0].astype(F32))).astype(o_ref.dtype)


def _short_conv(proj, conv_w, *, n_lat):
    b, r, _ = proj.shape
    nb = conv_w.shape[1] // LANES
    base = 4 * nb
    spec = lambda off: pl.BlockSpec((1, r, LANES), lambda bi, j: (bi, 0, off + j))
    return pl.pallas_call(
        functools.partial(_sconv_kernel, n_lat=n_lat),
        grid=(b, nb),
        in_specs=[spec(base), spec(base + nb), spec(base + 2 * nb), spec(base + 3 * nb),
                  pl.BlockSpec((SC_WIDTH, LANES), lambda bi, j: (0, j))],
        out_specs=pl.BlockSpec((1, r, LANES), lambda bi, j: (bi, 0, j)),
        out_shape=jax.ShapeDtypeStruct((b, r, nb * LANES), BF16),
        compiler_params=_cparams(("arbitrary", "arbitrary")),
        name="short_conv",
    )(proj, proj, proj, proj, conv_w)


def _natten_kernel(q_ref, k_ref, v_ref, g_ref, tb_ref, o_ref, *, n_lat, n_rows):
    lane = lax.broadcasted_iota(jnp.int32, (1, LANES), 1)
    first = lane < NA_DIM
    nt = (((1,), (1,)), ((), ()))
    win = WIN_ROWS * GRID_W
    kc = k_ref[0, n_lat:, :]
    vc = v_ref[0, n_lat:, :]

    def stack_heads(q):
        zero = jnp.zeros_like(q)
        return jnp.concatenate([jnp.where(first, q, zero), jnp.where(first, zero, q)], axis=0)

    def softmax(parts):
        m = functools.reduce(jnp.maximum, [jnp.max(s, axis=-1, keepdims=True) for s in parts])
        es = [jnp.exp(s - m) for s in parts]
        l = functools.reduce(jnp.add, [jnp.sum(e, axis=-1, keepdims=True) for e in es])
        return [e.astype(BF16) for e in es], 1.0 / l

    def unstack_heads(o2):
        n = o2.shape[0] // 2
        return jnp.where(first, o2[:n], o2[n:])

    def group_body(gi, carry):
        rows = [gi * NA_GROUP + t for t in range(NA_GROUP)]
        starts = [jnp.clip(rr - WIN_ROWS // 2, 0, n_rows - WIN_ROWS) for rr in rows]
        qsl = [pl.ds(pl.multiple_of(rr * GRID_W, GRID_W), GRID_W) for rr in rows]
        ksl = [pl.ds(pl.multiple_of(rs * GRID_W, GRID_W), win) for rs in starts]
        scores = []
        for rr, rs, qs, ks in zip(rows, starts, qsl, ksl):
            q2 = stack_heads(q_ref[0, qs, :])
            s_loc = lax.dot_general(q2, k_ref[0, ks, :], nt, preferred_element_type=F32)
            s_loc = s_loc + tb_ref[0, rs - rr + (WIN_ROWS - 1)]
            s_ctx = lax.dot_general(q2, kc, nt, preferred_element_type=F32)
            scores.append((s_loc, s_ctx))
        weights = [softmax(list(sc)) for sc in scores]
        for (es, rl), qs, ks in zip(weights, qsl, ksl):
            o2 = jnp.dot(es[0], v_ref[0, ks, :], preferred_element_type=F32)
            o2 = o2 + jnp.dot(es[1], vc, preferred_element_type=F32)
            out = unstack_heads(o2 * rl)
            o_ref[0, qs, :] = (out * _silu(g_ref[0, qs, :].astype(F32))).astype(o_ref.dtype)
        return carry

    lax.fori_loop(0, n_rows // NA_GROUP, group_body, 0)

    s = lax.dot_general(stack_heads(q_ref[0, n_lat:, :]), kc, nt, preferred_element_type=F32)
    es, rl = softmax([s])
    outc = unstack_heads(jnp.dot(es[0], vc, preferred_element_type=F32) * rl)
    o_ref[0, n_lat:, :] = (outc * _silu(g_ref[0, n_lat:, :].astype(F32))).astype(o_ref.dtype)


def _neighbourhood_attention(proj, tb, *, n_lat, cols):
    b, r, _ = proj.shape
    nb = NA_HEADS // 2
    kb, vb, qb, gb = (c // LANES for c in cols)
    spec = lambda off: pl.BlockSpec((1, r, LANES), lambda bi, j: (bi, 0, off + j))
    return pl.pallas_call(
        functools.partial(_natten_kernel, n_lat=n_lat, n_rows=n_lat // GRID_W),
        grid=(b, nb),
        in_specs=[spec(qb), spec(kb), spec(vb), spec(gb),
                  pl.BlockSpec((1,) + tb.shape[1:], lambda bi, j: (j, 0, 0, 0))],
        out_specs=pl.BlockSpec((1, r, LANES), lambda bi, j: (bi, 0, j)),
        out_shape=jax.ShapeDtypeStruct((b, r, nb * LANES), BF16),
        compiler_params=_cparams(("arbitrary", "arbitrary")),
        name="neighbourhood_attention",
    )(proj, proj, proj, proj, tb)


def _natten_bias_table(rpb):
    qcol = np.arange(GRID_W)[:, None]
    kcol = np.arange(GRID_W)[None, :]
    win_start = np.clip(qcol - WIN_COLS // 2, 0, GRID_W - WIN_COLS)
    valid = (kcol >= win_start) & (kcol < win_start + WIN_COLS)
    col_off = np.clip(kcol - qcol + (WIN_COLS - 1), 0, 2 * WIN_COLS - 2)
    row_idx = np.arange(WIN_ROWS)[:, None] + np.arange(WIN_ROWS)[None, :]
    t = rpb.astype(F32)[:, row_idx][:, :, :, col_off]
    t = jnp.where(jnp.asarray(valid)[None, None, None], t, -jnp.inf)
    nh = rpb.shape[0]
    t = jnp.transpose(t, (0, 1, 3, 2, 4)).reshape(nh // 2, 2, WIN_ROWS, GRID_W, WIN_ROWS * GRID_W)
    return jnp.transpose(t, (0, 2, 1, 3, 4)).reshape(nh // 2, WIN_ROWS, 2 * GRID_W, WIN_ROWS * GRID_W)


def _gla_kernel(q_ref, k_ref, v_ref, g_ref, lr_ref, gu_ref, gb_ref, gn_ref, o_ref, of_scr, st_scr,
                *, n_lat, n_ctx, qscale):
    ch = GLA_CHUNK
    mid = ch // 2
    nt = (((1,), (1,)), ((), ()))
    tn = (((0,), (0,)), ((), ()))
    trow = lax.broadcasted_iota(jnp.int32, (ch, ch), 0)
    tcol = lax.broadcasted_iota(jnp.int32, (ch, ch), 1)
    srow = lax.broadcasted_iota(jnp.int32, (ch, 1), 0)

    def cumsum_rows(g, reverse):
        acc = g
        sh = 1
        while sh < ch:
            if reverse:
                acc = acc + jnp.where(srow < ch - sh, pltpu.roll(acc, ch - sh, 0), 0.0)
            else:
                acc = acc + jnp.where(srow >= sh, pltpu.roll(acc, sh, 0), 0.0)
            sh *= 2
        return acc

    def group(starts, dirn):
        reverse = dirn == 1
        keep = (tcol >= trow) if reverse else (tcol <= trow)
        rows_l = [pl.ds(s if isinstance(s, int) else pl.multiple_of(s, ch), ch) for s in starts]
        zs = [jnp.dot(lr_ref[0, rows, :].astype(BF16), gu_ref[dirn, 0], preferred_element_type=F32)
              for rows in rows_l]
        ops = []
        for rows, z in zip(rows_l, zs):
            g = jax.nn.log_sigmoid(z + gb_ref[dirn, 0]) * (1.0 / GLA_TAU)
            bcum = cumsum_rows(g, reverse)
            b_end = bcum[0:1] if reverse else bcum[ch - 1:ch]
            rho = bcum[mid:mid + 1]
            up = jnp.exp(bcum - rho)
            dn = jnp.exp(rho - bcum)
            qf = q_ref[0, rows, :].astype(F32) * qscale
            kf = k_ref[0, rows, :].astype(F32)
            ops.append(dict(
                q_state=(qf * (up * jnp.exp(rho))).astype(BF16),
                q_local=(qf * up).astype(BF16),
                k_local=(kf * dn).astype(BF16),
                k_state=(kf * (dn * jnp.exp(b_end - rho))).astype(BF16),
                decay=jnp.exp(b_end)))
        atts = [lax.dot_general(o["q_local"], o["k_local"], nt, preferred_element_type=F32) for o in ops]
        st_adds = [lax.dot_general(v_ref[0, rows, :], o["k_state"], tn, preferred_element_type=F32)
                   for rows, o in zip(rows_l, ops)]
        o_locals = [jnp.dot(jnp.where(keep, att, 0.0).astype(BF16), v_ref[0, rows, :], preferred_element_type=F32)
                    for rows, att in zip(rows_l, atts)]
        st = st_scr[...]
        states = []
        for o, st_add in zip(ops, st_adds):
            states.append(st.astype(BF16))
            st = st * o["decay"] + st_add
        st_scr[...] = st
        outs = [o_local + lax.dot_general(o["q_state"], s_in, nt, preferred_element_type=F32)
                for o, o_local, s_in in zip(ops, o_locals, states)]
        for rows, o in zip(rows_l, outs):
            if dirn == 0:
                of_scr[rows, :] = o
            else:
                o = o + of_scr[rows, :]
                o = o * lax.rsqrt(jnp.mean(o * o, axis=-1, keepdims=True) + EPS) * gn_ref[...]
                o_ref[0, rows, :] = (o * _silu(g_ref[0, rows, :].astype(F32))).astype(o_ref.dtype)

    n_cc = n_ctx // ch
    n_lg = n_lat // (ch * GLA_GROUP)
    gsz = ch * GLA_GROUP

    def latent_group(gi, dirn):
        if dirn == 0:
            group([gi * gsz + t * ch for t in range(GLA_GROUP)], 0)
        else:
            group([(n_lg - 1 - gi) * gsz + t * ch for t in range(GLA_GROUP - 1, -1, -1)], 1)

    st_scr[...] = jnp.zeros_like(st_scr)
    group([n_lat + c * ch for c in range(n_cc)], 0)
    lax.fori_loop(0, n_lg, lambda gi, carry: (latent_group(gi, 0), carry)[1], 0)
    st_scr[...] = jnp.zeros_like(st_scr)
    group([n_lat + c * ch for c in range(n_cc - 1, -1, -1)], 1)
    lax.fori_loop(0, n_lg, lambda gi, carry: (latent_group(gi, 1), carry)[1], 0)


def _gla(proj, lr, gu, gb, gnorm, *, n_lat, cols):
    b, r, _ = proj.shape
    nh = GLA_HEADS
    dk = LANES
    dv = 2 * LANES
    kcol, vcol, qcol, gcol = cols
    return pl.pallas_call(
        functools.partial(_gla_kernel, n_lat=n_lat, n_ctx=r - n_lat, qscale=dk ** -0.5),
        grid=(b, nh),
        in_specs=[
            pl.BlockSpec((1, r, dk), lambda bi, hi: (bi, 0, qcol // dk + hi)),
            pl.BlockSpec((1, r, dk), lambda bi, hi: (bi, 0, kcol // dk + hi)),
            pl.BlockSpec((1, r, dv), lambda bi, hi: (bi, 0, vcol // dv + hi)),
            pl.BlockSpec((1, r, dv), lambda bi, hi: (bi, 0, gcol // dv + hi)),
            pl.BlockSpec((1, r, LANES), lambda bi, hi: (bi, 0, 0)),
            pl.BlockSpec((2, 1, LANES, dk), lambda bi, hi: (0, hi, 0, 0)),
            pl.BlockSpec((2, 1, 1, dk), lambda bi, hi: (0, hi, 0, 0)),
            pl.BlockSpec((1, dv), lambda bi, hi: (0, 0)),
        ],
        out_specs=pl.BlockSpec((1, r, dv), lambda bi, hi: (bi, 0, hi)),
        out_shape=jax.ShapeDtypeStruct((b, r, nh * dv), BF16),
        scratch_shapes=[pltpu.VMEM((r, dv), F32), pltpu.VMEM((dv, dk), F32)],
        compiler_params=_cparams(("arbitrary", "arbitrary")),
        name="gla",
    )(proj, proj, proj, proj, lr, gu, gb, gnorm.reshape(1, dv))


def _rope_tables(n_lat, n_ctx):
    half = DA_QK // 2
    inv = 1.0 / (ROPE_BASE ** (jnp.arange(0, half, 2, dtype=F32) / half))
    t = jnp.arange(n_lat)
    row = (t // GRID_W).astype(F32)[:, None] * inv
    col = (t % GRID_W).astype(F32)[:, None] * inv
    ang = jnp.concatenate([row, row, col, col], axis=-1)
    sign = jnp.asarray(np.where((np.arange(DA_QK) & 16) == 0, -1.0, 1.0), F32)
    cos = jnp.concatenate([jnp.cos(ang), jnp.ones((n_ctx, DA_QK), F32)], axis=0)
    sin = jnp.concatenate([jnp.sin(ang) * sign, jnp.zeros((n_ctx, DA_QK), F32)], axis=0)
    cos = jnp.concatenate([cos, cos], axis=1)
    sin = jnp.concatenate([sin, sin], axis=1)
    log2e = math.log2(math.e)
    return jnp.stack([cos, cos * log2e]), jnp.stack([sin, sin * log2e])


def kernel(x, c, ctx, c_ctx, w_mod, b_mod, g_pre, g_post, w_out, ev_w_in, ev_lambda, ev_subln, ev_conv,
           od_w_in, od_rpb, od_gate_up, od_gate_bias, od_gnorm):
    b, s, d = x.shape
    n_ctx = ctx.shape[1]
    depth = w_mod.shape[0]
    bw = EXPAND * d // 2
    assert bw == DA_HEADS * LANES == NA_HEADS * NA_DIM and (s + n_ctx) % (ROW_SPLIT * 16) == 0
    assert s % ATT_TQ == 0 and n_ctx % ATT_TQ == 0 and s % GRID_W == 0 and n_ctx % GLA_CHUNK == 0

    xall = jnp.concatenate([x, ctx], axis=1)
    nrow = -(-(b + 1) // 8) * 8
    cc = jnp.zeros((nrow, d), F32).at[:b].set(c).at[b].set(c_ctx)
    mod = _modulation(cc, w_mod, b_mod)
    cos_t, sin_t = _rope_tables(s, n_ctx)

    gk = bw // 2
    od_lr0 = 3 * bw + gk
    od_cols = np.cumsum([0, bw, bw, gk, bw, bw, gk, bw])
    na_cols = (int(od_cols[0]), int(od_cols[1]), int(od_cols[4]), int(od_cols[6]))
    gla_cols = (int(od_cols[2]), int(od_cols[3]), int(od_cols[5]), int(od_cols[7]))

    for l in range(depth):
        j = l // 2
        modb = mod[l, :b].reshape(b, 1, 3 * d)
        modc = mod[l, b:b + 1]
        if l % 2 == 0:
            lam_init = 0.8 - 0.6 * math.exp(-0.3 * l)
            w = ev_w_in[j]
            w = w.at[:, 2 * bw:3 * bw].multiply(DA_QK ** -0.5).astype(BF16)
            proj, _ = _inproj(xall, modb, modc, g_pre[l], w, (cos_t, sin_t, 0, 2), None, n_lat=s, tn=bw)
            vt = proj[:, :, bw:2 * bw].reshape(b, s + n_ctx, DA_HEADS, LANES).transpose(0, 2, 3, 1)
            ua = _diff_attention(proj, vt, ev_lambda[j], ev_subln[j], n_lat=s, lam_init=lam_init)
            ub = _short_conv(proj, ev_conv[j], n_lat=s)
        else:
            w = od_w_in[j]
            w_lr = jnp.pad(w[:, od_lr0:od_lr0 + 2 * GLA_RANK], ((0, 0), (0, LANES - 2 * GLA_RANK))).astype(BF16)
            w = jnp.concatenate([w[:, :od_lr0], w[:, od_lr0 + 2 * GLA_RANK:]], axis=1)
            w = w.at[:, na_cols[2]:na_cols[2] + bw].multiply(NA_DIM ** -0.5).astype(BF16)
            proj, lr = _inproj(xall, modb, modc, g_pre[l], w, None, w_lr, n_lat=s, tn=bw)
            tb = _natten_bias_table(od_rpb[j])
            ua = _neighbourhood_attention(proj, tb, n_lat=s, cols=na_cols)
            gu = jnp.zeros((2, GLA_HEADS, LANES, LANES), F32)
            gup = od_gate_up[j].reshape(2, GLA_RANK, GLA_HEADS, LANES).transpose(0, 2, 1, 3)
            gu = gu.at[0, :, :GLA_RANK].set(gup[0]).at[1, :, GLA_RANK:2 * GLA_RANK].set(gup[1]).astype(BF16)
            gb = od_gate_bias[j].reshape(2, GLA_HEADS, 1, LANES)
            ub = _gla(proj, lr, gu, gb, od_gnorm[j], n_lat=s, cols=gla_cols)
        xall = _outproj(ua, ub, w_out[l].astype(BF16), xall, modb, modc, g_post[l], n_lat=s)
    return xall[:, :s]
```

```python
import functools
import math

import numpy as np
import jax
import jax.numpy as jnp
from jax import lax
from jax.experimental import pallas as pl
from jax.experimental.pallas import tpu as pltpu

F32 = jnp.float32
BF16 = jnp.bfloat16

LANES = 128
GRID_W = 64
EXPAND = 2
DA_HEADS = 8
DA_QK = 64
SC_WIDTH = 3
NA_HEADS = 16
NA_DIM = 64
WIN_ROWS = 8
WIN_COLS = 16
GLA_HEADS = 4
GLA_RANK = 16
GLA_TAU = 16.0
GLA_CHUNK = 64
ROPE_BASE = 10000.0
EPS = 1e-6
ROW_SPLIT = 4
ROPE_ROWS = 64
ATT_TQ = 256
ATT_KCHUNK = 1024
NA_GROUP = 4
GLA_GROUP = 4
VMEM_LIMIT = 56 * 1024 * 1024


def _cparams(sem):
    return pltpu.CompilerParams(dimension_semantics=sem, vmem_limit_bytes=VMEM_LIMIT)


def _silu(t):
    return t * jax.nn.sigmoid(t)


def _mod_kernel(c_ref, w_ref, b_ref, o_ref):
    s = _silu(c_ref[...])
    o_ref[0] = jnp.dot(s, w_ref[0], preferred_element_type=F32) + b_ref[0]


def _modulation(cc, w_mod, b_mod):
    depth, d, d3 = w_mod.shape
    nrow = cc.shape[0]
    tn = d
    return pl.pallas_call(
        _mod_kernel,
        grid=(depth, d3 // tn),
        in_specs=[
            pl.BlockSpec((nrow, d), lambda l, j: (0, 0)),
            pl.BlockSpec((1, d, tn), lambda l, j: (l, 0, j)),
            pl.BlockSpec((1, 1, tn), lambda l, j: (l, 0, j)),
        ],
        out_specs=pl.BlockSpec((1, nrow, tn), lambda l, j: (l, 0, j)),
        out_shape=jax.ShapeDtypeStruct((depth, nrow, d3), F32),
        compiler_params=_cparams(("arbitrary", "arbitrary")),
        name="modulation",
    )(cc, w_mod, b_mod.reshape(depth, 1, d3))


def _inproj_kernel(*refs, tm, d, n_lat, rope_tiles, has_lr):
    x_ref, mb_ref, mc_ref, g_ref, w_ref = refs[:5]
    rest = list(refs[5:])
    cos_ref, sin_ref = (rest.pop(0), rest.pop(0)) if rope_tiles else (None, None)
    wlr_ref = rest.pop(0) if has_lr else None
    o_ref = rest.pop(0)
    lr_ref = rest.pop(0) if has_lr else None
    h_scr = rest.pop(0)
    acc_scr = rest.pop(0) if rope_tiles else None
    k = pl.program_id(1)
    j = pl.program_id(2)
    n_blk = pl.num_programs(1)

    @pl.when(j == 0)
    def _():
        xv = x_ref[0]
        xn = xv * lax.rsqrt(jnp.mean(xv * xv, axis=-1, keepdims=True) + EPS)
        mb = mb_ref[0]
        gain_b = g_ref[...] * (1.0 + mb[:, d:2 * d])
        shift_b = mb[:, 0:d]

        def emit(gain, shift):
            h_scr[...] = (xn * gain + shift).astype(BF16)
            if has_lr:
                lr_ref[0] = jnp.dot(h_scr[...], wlr_ref[...], preferred_element_type=F32)

        pl.when(k < n_blk - 1)(lambda: emit(gain_b, shift_b))

        @pl.when(k == n_blk - 1)
        def _():
            is_ctx = k * tm + lax.broadcasted_iota(jnp.int32, (tm, 1), 0) >= n_lat
            gain_c = g_ref[...] * (1.0 + mc_ref[:, d:2 * d])
            emit(jnp.where(is_ctx, gain_c, gain_b), jnp.where(is_ctx, mc_ref[:, 0:d], shift_b))

    def project():
        return jnp.dot(h_scr[...], w_ref[...], preferred_element_type=F32)

    if rope_tiles:
        is_rope = functools.reduce(jnp.logical_or, [j == t for t in rope_tiles])

        @pl.when(is_rope)
        def _():
            acc_scr[...] = project()
            tn = acc_scr.shape[1]

            def rows_body(c, carry):
                rows = pl.ds(pl.multiple_of(c * ROPE_ROWS, ROPE_ROWS), ROPE_ROWS)
                cos = cos_ref[0, rows, :]
                sin = sin_ref[0, rows, :]
                for hd in range(tn // LANES):
                    a = acc_scr[rows, hd * LANES:(hd + 1) * LANES]
                    o_ref[0, rows, hd * LANES:(hd + 1) * LANES] = (
                        a * cos + pltpu.roll(a, LANES // 2, 1) * sin).astype(o_ref.dtype)
                return carry

            lax.fori_loop(0, tm // ROPE_ROWS, rows_body, 0, unroll=True)

        @pl.when(jnp.logical_not(is_rope))
        def _():
            o_ref[0] = project().astype(o_ref.dtype)
    else:
        o_ref[0] = project().astype(o_ref.dtype)


def _inproj(xall, modb, modc, g, w, rope, w_lr, *, n_lat, tn):
    b, r, d = xall.shape
    nc = w.shape[1]
    tm = r // ROW_SPLIT
    has_lr = w_lr is not None
    in_specs = [
        pl.BlockSpec((1, tm, d), lambda bi, k, j: (bi, k, 0)),
        pl.BlockSpec((1, 1, 3 * d), lambda bi, k, j: (bi, 0, 0)),
        pl.BlockSpec((1, 3 * d), lambda bi, k, j: (0, 0)),
        pl.BlockSpec((1, d), lambda bi, k, j: (0, 0)),
        pl.BlockSpec((d, tn), lambda bi, k, j: (0, j)),
    ]
    args = [xall, modb, modc, g.reshape(1, d), w]
    rope_tiles = ()
    if rope is not None:
        cos_t, sin_t, k_tile, q_tile = rope
        rope_tiles = (k_tile, q_tile)
        table = pl.BlockSpec((1, tm, LANES), lambda bi, k, j: (jnp.where(j == q_tile, 1, 0), k, 0))
        in_specs += [table, table]
        args += [cos_t, sin_t]
    out_specs = [pl.BlockSpec((1, tm, tn), lambda bi, k, j: (bi, k, j))]
    out_shape = [jax.ShapeDtypeStruct((b, r, nc), BF16)]
    if has_lr:
        in_specs.append(pl.BlockSpec((d, LANES), lambda bi, k, j: (0, 0)))
        args.append(w_lr)
        out_specs.append(pl.BlockSpec((1, tm, LANES), lambda bi, k, j: (bi, k, 0)))
        out_shape.append(jax.ShapeDtypeStruct((b, r, LANES), F32))
    res = pl.pallas_call(
        functools.partial(_inproj_kernel, tm=tm, d=d, n_lat=n_lat, rope_tiles=rope_tiles, has_lr=has_lr),
        grid=(b, ROW_SPLIT, nc // tn),
        in_specs=in_specs,
        out_specs=out_specs,
        out_shape=out_shape,
        scratch_shapes=[pltpu.VMEM((tm, d), BF16)] + ([pltpu.VMEM((tm, tn), F32)] if rope_tiles else []),
        compiler_params=_cparams(("arbitrary", "arbitrary", "arbitrary")),
        name="inproj",
    )(*args)
    return res if has_lr else (res[0], None)


def _outproj_kernel(ua_ref, ub_ref, w1_ref, w2_ref, x_ref, mb_ref, mc_ref, g_ref, o_ref, *, tm, d, n_lat):
    k = pl.program_id(1)
    y = jnp.dot(ua_ref[0], w1_ref[...], preferred_element_type=F32)
    y = y + jnp.dot(ub_ref[0], w2_ref[...], preferred_element_type=F32)
    y = y * lax.rsqrt(jnp.mean(y * y, axis=-1, keepdims=True) + EPS) * g_ref[...]
    row = k * tm + lax.broadcasted_iota(jnp.int32, (tm, 1), 0)
    gate = jnp.where(row >= n_lat, mc_ref[:, 2 * d:3 * d], mb_ref[0][:, 2 * d:3 * d])
    o_ref[0] = x_ref[0] + gate * y


def _outproj(ua, ub, w, xall, modb, modc, g, *, n_lat, latent_only):
    b, r, d = xall.shape
    bw = ua.shape[2]
    tm = n_lat // ROW_SPLIT if latent_only else r // ROW_SPLIT
    return pl.pallas_call(
        functools.partial(_outproj_kernel, tm=tm, d=d, n_lat=n_lat),
        grid=(b, ROW_SPLIT),
        in_specs=[
            pl.BlockSpec((1, tm, bw), lambda bi, k: (bi, k, 0)),
            pl.BlockSpec((1, tm, bw), lambda bi, k: (bi, k, 0)),
            pl.BlockSpec((bw, d), lambda bi, k: (0, 0)),
            pl.BlockSpec((bw, d), lambda bi, k: (1, 0)),
            pl.BlockSpec((1, tm, d), lambda bi, k: (bi, k, 0)),
            pl.BlockSpec((1, 1, 3 * d), lambda bi, k: (bi, 0, 0)),
            pl.BlockSpec((1, 3 * d), lambda bi, k: (0, 0)),
            pl.BlockSpec((1, d), lambda bi, k: (0, 0)),
        ],
        out_specs=pl.BlockSpec((1, tm, d), lambda bi, k: (bi, k, 0)),
        out_shape=jax.ShapeDtypeStruct((b, n_lat if latent_only else r, d), F32),
        input_output_aliases={} if latent_only else {4: 0},
        compiler_params=_cparams(("arbitrary", "arbitrary")),
        name="outproj",
    )(ua, ub, w, w, xall, modb, modc, g.reshape(1, d))


def _dattn_lambda(lam_ref, lam_init):
    lp = lam_ref[...]
    return (jnp.exp(jnp.sum(lp[0:1] * lp[1:2], axis=-1, keepdims=True))
            - jnp.exp(jnp.sum(lp[2:3] * lp[3:4], axis=-1, keepdims=True)) + lam_init)


def _dattn_split_maps(q):
    first = (lax.broadcasted_iota(jnp.int32, (1, LANES), 1) & (DA_QK // 2)) == 0
    zero = jnp.zeros_like(q)
    return jnp.concatenate([jnp.where(first, q, zero), jnp.where(first, zero, q)], axis=0)


def _dattn_finish(o, sub_ref, g_ref, o_ref, lam_init):
    o = o * lax.rsqrt(jnp.mean(o * o, axis=-1, keepdims=True) + EPS) * sub_ref[...] * (1.0 - lam_init)
    o_ref[0] = (o * _silu(g_ref[0].astype(F32))).astype(o_ref.dtype)


def _dattn_kernel(lam_ref, sub_ref, qn_ref, k_ref, vt_ref, g_ref, o_ref, sa_ref, ma_ref, sb_ref, mb_ref,
                  *, nq, lam_init):
    t = pl.program_id(2)
    lam = _dattn_lambda(lam_ref, lam_init)
    nt = (((1,), (1,)), ((), ()))
    tq = qn_ref.shape[1]

    def step(cur, prev):
        r = k_ref.shape[1]
        bounds = list(range(0, r - ATT_KCHUNK + 1, ATT_KCHUNK)) + [r]
        if cur is not None:
            q2 = _dattn_split_maps(qn_ref[0])
        m_cur = l_prev = o_prev = None
        for lo, hi in zip(bounds[:-1], bounds[1:]):
            if cur is not None:
                s = lax.dot_general(k_ref[0, lo:hi, :], q2, nt, preferred_element_type=F32)
                cur[0][lo:hi, :] = s
                m = jnp.max(s, axis=0, keepdims=True)
                m_cur = m if m_cur is None else jnp.maximum(m_cur, m)
            if prev is not None:
                e = jnp.exp2(prev[0][lo:hi, :] - prev[1][...])
                l = jnp.sum(e, axis=0, keepdims=True)
                o = jnp.dot(vt_ref[0, 0, :, lo:hi], e.astype(BF16), preferred_element_type=F32)
                l_prev = l if l_prev is None else l_prev + l
                o_prev = o if o_prev is None else o_prev + o
        if cur is not None:
            cur[1][...] = m_cur
        if prev is not None:
            o = o_prev * (1.0 / l_prev)
            _dattn_finish((o[:, :tq] - lam * o[:, tq:]).T, sub_ref, g_ref, o_ref, lam_init)

    even = t % 2 == 0
    buf_a = (sa_ref, ma_ref)
    buf_b = (sb_ref, mb_ref)
    pl.when(t == 0)(lambda: step(buf_a, None))
    pl.when((t > 0) & (t < nq) & even)(lambda: step(buf_a, buf_b))
    pl.when((t < nq) & jnp.logical_not(even))(lambda: step(buf_b, buf_a))
    pl.when(t == nq)(lambda: step(None, buf_a if (nq - 1) % 2 == 0 else buf_b))


def _dattn_ctx_kernel(lam_ref, sub_ref, q_ref, k_ref, v_ref, g_ref, ua_ref, o_ref, *, lam_init):
    del ua_ref
    lam = _dattn_lambda(lam_ref, lam_init)
    nt = (((1,), (1,)), ((), ()))
    n = q_ref.shape[1]
    s = lax.dot_general(_dattn_split_maps(q_ref[0]), k_ref[0], nt, preferred_element_type=F32)
    e = jnp.exp2(s - jnp.max(s, axis=-1, keepdims=True))
    p = e / jnp.sum(e, axis=-1, keepdims=True)
    a = (p[:n] - lam * p[n:]).astype(BF16)
    _dattn_finish(jnp.dot(a, v_ref[0], preferred_element_type=F32), sub_ref, g_ref, o_ref, lam_init)


def _diff_attention(proj, vt, lam_p, subln, *, n_lat, lam_init):
    b, r, _ = proj.shape
    h = DA_HEADS
    tq = ATT_TQ
    nq = n_lat // tq
    n_ctx = r - n_lat
    sub = subln.reshape(1, LANES)
    small = [pl.BlockSpec(lam_p.shape, lambda *_: (0, 0)), pl.BlockSpec((1, LANES), lambda *_: (0, 0))]
    ua = pl.pallas_call(
        functools.partial(_dattn_kernel, nq=nq, lam_init=lam_init),
        grid=(b, h, nq + 1),
        in_specs=small + [
            pl.BlockSpec((1, tq, LANES), lambda bi, hi, t: (bi, jnp.minimum(t, nq - 1), 2 * h + hi)),
            pl.BlockSpec((1, r, LANES), lambda bi, hi, t: (bi, 0, hi)),
            pl.BlockSpec((1, 1, LANES, r), lambda bi, hi, t: (bi, hi, 0, 0)),
            pl.BlockSpec((1, tq, LANES), lambda bi, hi, t: (bi, jnp.maximum(t - 1, 0), 3 * h + hi)),
        ],
        out_specs=pl.BlockSpec((1, tq, LANES), lambda bi, hi, t: (bi, jnp.maximum(t - 1, 0), hi)),
        out_shape=jax.ShapeDtypeStruct((b, r, h * LANES), BF16),
        scratch_shapes=[pltpu.VMEM((r, 2 * tq), F32), pltpu.VMEM((1, 2 * tq), F32)] * 2,
        compiler_params=_cparams(("arbitrary", "arbitrary", "arbitrary")),
        name="diff_attention",
    )(lam_p, sub, proj, proj, vt, proj)
    cb = n_lat // n_ctx
    ctx_rows = lambda off: pl.BlockSpec((1, n_ctx, LANES), lambda bi, hi: (bi, cb, off + hi))
    return pl.pallas_call(
        functools.partial(_dattn_ctx_kernel, lam_init=lam_init),
        grid=(b, h),
        in_specs=small + [ctx_rows(2 * h), ctx_rows(0), ctx_rows(h), ctx_rows(3 * h),
                          pl.BlockSpec(memory_space=pl.ANY)],
        out_specs=ctx_rows(0),
        out_shape=jax.ShapeDtypeStruct(ua.shape, ua.dtype),
        input_output_aliases={6: 0},
        compiler_params=_cparams(("arbitrary", "arbitrary")),
        name="diff_attention_ctx",
    )(lam_p, sub, proj, proj, proj, proj, ua)


def _sconv_kernel(h_ref, b_ref, c_ref, g_ref, w_ref, o_ref, *, n_lat):
    r = h_ref.shape[1]
    p = c_ref[0].astype(F32) * h_ref[0].astype(F32)
    row = lax.broadcasted_iota(jnp.int32, (r, 1), 0)
    prev = jnp.where((row == 0) | (row == n_lat), 0.0, pltpu.roll(p, 1, 0))
    nxt = jnp.where((row == n_lat - 1) | (row == r - 1), 0.0, pltpu.roll(p, r - 1, 0))
    w = w_ref[...]
    conv = w[0:1] * prev + w[1:2] * p + w[2:3] * nxt
    o_ref[0] = (b_ref[0].astype(F32) * conv * _silu(g_ref[0].astype(F32))).astype(o_ref.dtype)


def _short_conv(proj, conv_w, *, n_lat):
    b, r, _ = proj.shape
    nb = conv_w.shape[1] // LANES
    base = 4 * nb
    spec = lambda off: pl.BlockSpec((1, r, LANES), lambda bi, j: (bi, 0, off + j))
    return pl.pallas_call(
        functools.partial(_sconv_kernel, n_lat=n_lat),
        grid=(b, nb),
        in_specs=[spec(base), spec(base + nb), spec(base + 2 * nb), spec(base + 3 * nb),
                  pl.BlockSpec((SC_WIDTH, LANES), lambda bi, j: (0, j))],
        out_specs=pl.BlockSpec((1, r, LANES), lambda bi, j: (bi, 0, j)),
        out_shape=jax.ShapeDtypeStruct((b, r, nb * LANES), BF16),
        compiler_params=_cparams(("arbitrary", "arbitrary")),
        name="short_conv",
    )(proj, proj, proj, proj, conv_w)


def _natten_kernel(q_ref, k_ref, v_ref, g_ref, tb_ref, o_ref, *, n_lat, n_rows):
    lane = lax.broadcasted_iota(jnp.int32, (1, LANES), 1)
    first = lane < NA_DIM
    nt = (((1,), (1,)), ((), ()))
    win = WIN_ROWS * GRID_W
    kc = k_ref[0, n_lat:, :]
    vc = v_ref[0, n_lat:, :]

    def stack_heads(q):
        q = q * (NA_DIM ** -0.5)
        zero = jnp.zeros_like(q)
        return jnp.concatenate([jnp.where(first, q, zero), jnp.where(first, zero, q)], axis=0)

    def softmax(parts):
        m = functools.reduce(jnp.maximum, [jnp.max(s, axis=-1, keepdims=True) for s in parts])
        es = [jnp.exp(s - m) for s in parts]
        l = functools.reduce(jnp.add, [jnp.sum(e, axis=-1, keepdims=True) for e in es])
        return [e.astype(BF16) for e in es], 1.0 / l

    def unstack_heads(o2):
        n = o2.shape[0] // 2
        return jnp.where(first, o2[:n], o2[n:])

    def group_body(gi, carry):
        rows = [gi * NA_GROUP + t for t in range(NA_GROUP)]
        starts = [jnp.clip(rr - WIN_ROWS // 2, 0, n_rows - WIN_ROWS) for rr in rows]
        qsl = [pl.ds(pl.multiple_of(rr * GRID_W, GRID_W), GRID_W) for rr in rows]
        ksl = [pl.ds(pl.multiple_of(rs * GRID_W, GRID_W), win) for rs in starts]
        scores = []
        for rr, rs, qs, ks in zip(rows, starts, qsl, ksl):
            q2 = stack_heads(q_ref[0, qs, :])
            s_loc = lax.dot_general(q2, k_ref[0, ks, :], nt, preferred_element_type=F32)
            s_loc = s_loc + tb_ref[0, rs - rr + (WIN_ROWS - 1)]
            s_ctx = lax.dot_general(q2, kc, nt, preferred_element_type=F32)
            scores.append((s_loc, s_ctx))
        weights = [softmax(list(sc)) for sc in scores]
        for (es, rl), qs, ks in zip(weights, qsl, ksl):
            o2 = jnp.dot(es[0], v_ref[0, ks, :], preferred_element_type=F32)
            o2 = o2 + jnp.dot(es[1], vc, preferred_element_type=F32)
            out = unstack_heads(o2 * rl)
            o_ref[0, qs, :] = (out * _silu(g_ref[0, qs, :].astype(F32))).astype(o_ref.dtype)
        return carry

    lax.fori_loop(0, n_rows // NA_GROUP, group_body, 0)

    s = lax.dot_general(stack_heads(q_ref[0, n_lat:, :]), kc, nt, preferred_element_type=F32)
    es, rl = softmax([s])
    outc = unstack_heads(jnp.dot(es[0], vc, preferred_element_type=F32) * rl)
    o_ref[0, n_lat:, :] = (outc * _silu(g_ref[0, n_lat:, :].astype(F32))).astype(o_ref.dtype)


def _neighbourhood_attention(proj, tb, *, n_lat, cols):
    b, r, _ = proj.shape
    nb = NA_HEADS // 2
    kb, vb, qb, gb = (c // LANES for c in cols)
    spec = lambda off: pl.BlockSpec((1, r, LANES), lambda bi, j: (bi, 0, off + j))
    return pl.pallas_call(
        functools.partial(_natten_kernel, n_lat=n_lat, n_rows=n_lat // GRID_W),
        grid=(b, nb),
        in_specs=[spec(qb), spec(kb), spec(vb), spec(gb),
                  pl.BlockSpec((1,) + tb.shape[1:], lambda bi, j: (j, 0, 0, 0))],
        out_specs=pl.BlockSpec((1, r, LANES), lambda bi, j: (bi, 0, j)),
        out_shape=jax.ShapeDtypeStruct((b, r, nb * LANES), BF16),
        compiler_params=_cparams(("arbitrary", "arbitrary")),
        name="neighbourhood_attention",
    )(proj, proj, proj, proj, tb)


def _natten_bias_table(rpb):
    qcol = np.arange(GRID_W)[:, None]
    kcol = np.arange(GRID_W)[None, :]
    win_start = np.clip(qcol - WIN_COLS // 2, 0, GRID_W - WIN_COLS)
    valid = (kcol >= win_start) & (kcol < win_start + WIN_COLS)
    pad = GRID_W - WIN_COLS
    padded = jnp.pad(rpb.astype(F32), ((0, 0), (0, 0), (pad, pad)))
    toe = jnp.stack([padded[:, :, GRID_W - 1 - q:2 * GRID_W - 1 - q] for q in range(GRID_W)], axis=2)
    toe = jnp.where(jnp.asarray(valid)[None, None], toe, -jnp.inf)
    t = jnp.stack([toe[:, off:off + WIN_ROWS] for off in range(WIN_ROWS)], axis=1)
    nh = rpb.shape[0]
    t = jnp.transpose(t, (0, 1, 3, 2, 4)).reshape(nh // 2, 2, WIN_ROWS, GRID_W, WIN_ROWS * GRID_W)
    return jnp.transpose(t, (0, 2, 1, 3, 4)).reshape(nh // 2, WIN_ROWS, 2 * GRID_W, WIN_ROWS * GRID_W)


def _gla_kernel(q_ref, k_ref, v_ref, g_ref, lr_ref, gu_ref, gb_ref, gn_ref, o_ref, of_scr, st_scr,
                *, n_lat, n_ctx, qscale):
    ch = GLA_CHUNK
    mid = ch // 2
    nt = (((1,), (1,)), ((), ()))
    tn = (((0,), (0,)), ((), ()))
    trow = lax.broadcasted_iota(jnp.int32, (ch, ch), 0)
    tcol = lax.broadcasted_iota(jnp.int32, (ch, ch), 1)
    srow = lax.broadcasted_iota(jnp.int32, (ch, 1), 0)

    def cumsum_rows(g, reverse):
        acc = g
        sh = 1
        while sh < ch:
            if reverse:
                acc = acc + jnp.where(srow < ch - sh, pltpu.roll(acc, ch - sh, 0), 0.0)
            else:
                acc = acc + jnp.where(srow >= sh, pltpu.roll(acc, sh, 0), 0.0)
            sh *= 2
        return acc

    def group(starts, dirn):
        reverse = dirn == 1
        keep = (tcol >= trow) if reverse else (tcol <= trow)
        rows_l = [pl.ds(s if isinstance(s, int) else pl.multiple_of(s, ch), ch) for s in starts]
        zs = [jnp.dot(lr_ref[0, rows, :].astype(BF16), gu_ref[dirn, 0], preferred_element_type=F32)
              for rows in rows_l]
        ops = []
        for rows, z in zip(rows_l, zs):
            g = jax.nn.log_sigmoid(z + gb_ref[dirn, 0]) * (1.0 / GLA_TAU)
            bcum = cumsum_rows(g, reverse)
            b_end = bcum[0:1] if reverse else bcum[ch - 1:ch]
            rho = bcum[mid:mid + 1]
            up = jnp.exp(bcum - rho)
            dn = jnp.exp(rho - bcum)
            qf = q_ref[0, rows, :].astype(F32) * qscale
            kf = k_ref[0, rows, :].astype(F32)
            ops.append(dict(
                q_state=(qf * (up * jnp.exp(rho))).astype(BF16),
                q_local=(qf * up).astype(BF16),
                k_local=(kf * dn).astype(BF16),
                k_state=(kf * (dn * jnp.exp(b_end - rho))).astype(BF16),
                decay=jnp.exp(b_end)))
        atts = [lax.dot_general(o["q_local"], o["k_local"], nt, preferred_element_type=F32) for o in ops]
        st_adds = [lax.dot_general(v_ref[0, rows, :], o["k_state"], tn, preferred_element_type=F32)
                   for rows, o in zip(rows_l, ops)]
        o_locals = [jnp.dot(jnp.where(keep, att, 0.0).astype(BF16), v_ref[0, rows, :], preferred_element_type=F32)
                    for rows, att in zip(rows_l, atts)]
        st = st_scr[...]
        states = []
        for o, st_add in zip(ops, st_adds):
            states.append(st.astype(BF16))
            st = st * o["decay"] + st_add
        st_scr[...] = st
        outs = [o_local + lax.dot_general(o["q_state"], s_in, nt, preferred_element_type=F32)
                for o, o_local, s_in in zip(ops, o_locals, states)]
        for rows, o in zip(rows_l, outs):
            if dirn == 0:
                of_scr[rows, :] = o
            else:
                o = o + of_scr[rows, :]
                o = o * lax.rsqrt(jnp.mean(o * o, axis=-1, keepdims=True) + EPS) * gn_ref[...]
                o_ref[0, rows, :] = (o * _silu(g_ref[0, rows, :].astype(F32))).astype(o_ref.dtype)

    n_cc = n_ctx // ch
    n_lg = n_lat // (ch * GLA_GROUP)
    gsz = ch * GLA_GROUP

    def latent_group(gi, dirn):
        if dirn == 0:
            group([gi * gsz + t * ch for t in range(GLA_GROUP)], 0)
        else:
            group([(n_lg - 1 - gi) * gsz + t * ch for t in range(GLA_GROUP - 1, -1, -1)], 1)

    st_scr[...] = jnp.zeros_like(st_scr)
    group([n_lat + c * ch for c in range(n_cc)], 0)
    lax.fori_loop(0, n_lg, lambda gi, carry: (latent_group(gi, 0), carry)[1], 0)
    st_scr[...] = jnp.zeros_like(st_scr)
    group([n_lat + c * ch for c in range(n_cc - 1, -1, -1)], 1)
    lax.fori_loop(0, n_lg, lambda gi, carry: (latent_group(gi, 1), carry)[1], 0)


def _gla(proj, lr, gu, gb, gnorm, *, n_lat, cols):
    b, r, _ = proj.shape
    nh = GLA_HEADS
    dk = LANES
    dv = 2 * LANES
    kcol, vcol, qcol, gcol = cols
    return pl.pallas_call(
        functools.partial(_gla_kernel, n_lat=n_lat, n_ctx=r - n_lat, qscale=dk ** -0.5),
        grid=(b, nh),
        in_specs=[
            pl.BlockSpec((1, r, dk), lambda bi, hi: (bi, 0, qcol // dk + hi)),
            pl.BlockSpec((1, r, dk), lambda bi, hi: (bi, 0, kcol // dk + hi)),
            pl.BlockSpec((1, r, dv), lambda bi, hi: (bi, 0, vcol // dv + hi)),
            pl.BlockSpec((1, r, dv), lambda bi, hi: (bi, 0, gcol // dv + hi)),
            pl.BlockSpec((1, r, LANES), lambda bi, hi: (bi, 0, 0)),
            pl.BlockSpec((2, 1, LANES, dk), lambda bi, hi: (0, hi, 0, 0)),
            pl.BlockSpec((2, 1, 1, dk), lambda bi, hi: (0, hi, 0, 0)),
            pl.BlockSpec((1, dv), lambda bi, hi: (0, 0)),
        ],
        out_specs=pl.BlockSpec((1, r, dv), lambda bi, hi: (bi, 0, hi)),
        out_shape=jax.ShapeDtypeStruct((b, r, nh * dv), BF16),
        scratch_shapes=[pltpu.VMEM((r, dv), F32), pltpu.VMEM((dv, dk), F32)],
        compiler_params=_cparams(("arbitrary", "arbitrary")),
        name="gla",
    )(proj, proj, proj, proj, lr, gu, gb, gnorm.reshape(1, dv))


def _rope_tables(n_lat, n_ctx):
    half = DA_QK // 2
    inv = 1.0 / (ROPE_BASE ** (jnp.arange(0, half, 2, dtype=F32) / half))
    t = jnp.arange(n_lat)
    row = (t // GRID_W).astype(F32)[:, None] * inv
    col = (t % GRID_W).astype(F32)[:, None] * inv
    ang = jnp.concatenate([row, row, col, col], axis=-1)
    lane = np.arange(LANES)
    dim = 32 * ((lane >> 4) & 1) + 16 * (lane >> 6) + (lane & 15)
    sign = jnp.asarray(np.where((lane >> 6) == 0, -1.0, 1.0), F32)
    cos = jnp.concatenate([jnp.cos(ang)[:, dim], jnp.ones((n_ctx, LANES), F32)], axis=0)
    sin = jnp.concatenate([jnp.sin(ang)[:, dim] * sign, jnp.zeros((n_ctx, LANES), F32)], axis=0)
    qf = math.log2(math.e) * DA_QK ** -0.5
    return jnp.stack([cos, cos * qf]), jnp.stack([sin, sin * qf])


def _pair_layout(w):
    d = w.shape[0]
    w = w.reshape(d, -1, 2, 2, 2, DA_QK // 4)
    return jnp.transpose(w, (0, 1, 4, 2, 3, 5)).reshape(d, -1)


def kernel(x, c, ctx, c_ctx, w_mod, b_mod, g_pre, g_post, w_out, ev_w_in, ev_lambda, ev_subln, ev_conv,
           od_w_in, od_rpb, od_gate_up, od_gate_bias, od_gnorm):
    b, s, d = x.shape
    n_ctx = ctx.shape[1]
    depth = w_mod.shape[0]
    bw = EXPAND * d // 2
    assert bw == DA_HEADS * LANES == NA_HEADS * NA_DIM and (s + n_ctx) % (ROW_SPLIT * 16) == 0
    assert s % ATT_TQ == 0 and n_ctx == ATT_TQ and s % GRID_W == 0 and n_ctx % GLA_CHUNK == 0
    tm = (s + n_ctx) // ROW_SPLIT
    assert tm % ROPE_ROWS == 0 and (ROW_SPLIT - 1) * tm <= s and s % (ROW_SPLIT * 16) == 0

    xall = jnp.concatenate([x, ctx], axis=1)
    nrow = -(-(b + 1) // 8) * 8
    cc = jnp.zeros((nrow, d), F32).at[:b].set(c).at[b].set(c_ctx)
    mod = _modulation(cc, w_mod, b_mod)
    cos_t, sin_t = _rope_tables(s, n_ctx)

    gk = bw // 2
    od_lr0 = 3 * bw + gk
    od_cols = np.cumsum([0, bw, bw, gk, bw, bw, gk, bw])
    na_cols = (int(od_cols[0]), int(od_cols[1]), int(od_cols[4]), int(od_cols[6]))
    gla_cols = (int(od_cols[2]), int(od_cols[3]), int(od_cols[5]), int(od_cols[7]))

    for l in range(depth):
        j = l // 2
        modb = mod[l, :b].reshape(b, 1, 3 * d)
        modc = mod[l, b:b + 1]
        if l % 2 == 0:
            lam_init = 0.8 - 0.6 * math.exp(-0.3 * l)
            w = ev_w_in[j]
            w = jnp.concatenate([_pair_layout(w[:, :bw]), w[:, bw:2 * bw], _pair_layout(w[:, 2 * bw:3 * bw]),
                                 w[:, 3 * bw:]], axis=1).astype(BF16)
            proj, _ = _inproj(xall, modb, modc, g_pre[l], w, (cos_t, sin_t, 0, 2), None, n_lat=s, tn=bw)
            vt = proj[:, :, bw:2 * bw].reshape(b, s + n_ctx, DA_HEADS, LANES).transpose(0, 2, 3, 1)
            ua = _diff_attention(proj, vt, ev_lambda[j], ev_subln[j], n_lat=s, lam_init=lam_init)
            ub = _short_conv(proj, ev_conv[j], n_lat=s)
        else:
            w = od_w_in[j]
            w_lr = jnp.pad(w[:, od_lr0:od_lr0 + 2 * GLA_RANK], ((0, 0), (0, LANES - 2 * GLA_RANK))).astype(BF16)
            w = jnp.concatenate([w[:, :od_lr0], w[:, od_lr0 + 2 * GLA_RANK:]], axis=1).astype(BF16)
            proj, lr = _inproj(xall, modb, modc, g_pre[l], w, None, w_lr, n_lat=s, tn=bw)
            tb = _natten_bias_table(od_rpb[j])
            ua = _neighbourhood_attention(proj, tb, n_lat=s, cols=na_cols)
            gu = jnp.zeros((2, GLA_HEADS, LANES, LANES), F32)
            gup = od_gate_up[j].reshape(2, GLA_RANK, GLA_HEADS, LANES).transpose(0, 2, 1, 3)
            gu = gu.at[0, :, :GLA_RANK].set(gup[0]).at[1, :, GLA_RANK:2 * GLA_RANK].set(gup[1]).astype(BF16)
            gb = od_gate_bias[j].reshape(2, GLA_HEADS, 1, LANES)
            ub = _gla(proj, lr, gu, gb, od_gnorm[j], n_lat=s, cols=gla_cols)
        xall = _outproj(ua, ub, w_out[l].astype(BF16), xall, modb, modc, g_post[l], n_lat=s,
                        latent_only=l == depth - 1)
    return xall
```

```python
import functools
import math

import numpy as np
import jax
import jax.numpy as jnp
from jax import lax
from jax.experimental import pallas as pl
from jax.experimental.pallas import tpu as pltpu

F32 = jnp.float32
BF16 = jnp.bfloat16

LANES = 128
GRID_W = 64
EXPAND = 2
DA_HEADS = 8
DA_QK = 64
SC_WIDTH = 3
NA_HEADS = 16
NA_DIM = 64
WIN_ROWS = 8
WIN_COLS = 16
GLA_HEADS = 4
GLA_RANK = 16
GLA_TAU = 16.0
GLA_CHUNK = 64
ROPE_BASE = 10000.0
EPS = 1e-6
ROW_SPLIT = 4
ROPE_ROWS = 64
ATT_TQ = 256
ATT_KCHUNK = 256
NA_GROUP = 4
GLA_GROUP = 4
GLA_HEADS_PER_STEP = 2
VMEM_LIMIT = 56 * 1024 * 1024


def _cparams(sem):
    return pltpu.CompilerParams(dimension_semantics=sem, vmem_limit_bytes=VMEM_LIMIT)


def _silu(t):
    return t * jax.nn.sigmoid(t)


def _mod_kernel(c_ref, w_ref, b_ref, o_ref):
    s = _silu(c_ref[...])
    o_ref[0] = jnp.dot(s, w_ref[0], preferred_element_type=F32) + b_ref[0]


def _modulation(cc, w_mod, b_mod):
    depth, d, d3 = w_mod.shape
    nrow = cc.shape[0]
    tn = d
    return pl.pallas_call(
        _mod_kernel,
        grid=(depth, d3 // tn),
        in_specs=[
            pl.BlockSpec((nrow, d), lambda l, j: (0, 0)),
            pl.BlockSpec((1, d, tn), lambda l, j: (l, 0, j)),
            pl.BlockSpec((1, 1, tn), lambda l, j: (l, 0, j)),
        ],
        out_specs=pl.BlockSpec((1, nrow, tn), lambda l, j: (l, 0, j)),
        out_shape=jax.ShapeDtypeStruct((depth, nrow, d3), F32),
        compiler_params=_cparams(("arbitrary", "arbitrary")),
        name="modulation",
    )(cc, w_mod, b_mod.reshape(depth, 1, d3))


def _inproj_kernel(*refs, tm, d, n_lat, rope_tiles, has_lr):
    x_ref, mb_ref, mc_ref, g_ref, w_ref = refs[:5]
    rest = list(refs[5:])
    cos_ref, sin_ref = (rest.pop(0), rest.pop(0)) if rope_tiles else (None, None)
    wlr_ref = rest.pop(0) if has_lr else None
    o_ref = rest.pop(0)
    lr_ref = rest.pop(0) if has_lr else None
    h_scr = rest.pop(0)
    acc_scr = rest.pop(0) if rope_tiles else None
    k = pl.program_id(1)
    j = pl.program_id(2)
    n_blk = pl.num_programs(1)

    @pl.when(j == 0)
    def _():
        xv = x_ref[0]
        xn = xv * lax.rsqrt(jnp.mean(xv * xv, axis=-1, keepdims=True) + EPS)
        mb = mb_ref[0]
        gain_b = g_ref[...] * (1.0 + mb[:, d:2 * d])
        shift_b = mb[:, 0:d]

        def emit(gain, shift):
            h_scr[...] = (xn * gain + shift).astype(BF16)
            if has_lr:
                lr_ref[0] = jnp.dot(h_scr[...], wlr_ref[...], preferred_element_type=F32)

        pl.when(k < n_blk - 1)(lambda: emit(gain_b, shift_b))

        @pl.when(k == n_blk - 1)
        def _():
            is_ctx = k * tm + lax.broadcasted_iota(jnp.int32, (tm, 1), 0) >= n_lat
            gain_c = g_ref[...] * (1.0 + mc_ref[:, d:2 * d])
            emit(jnp.where(is_ctx, gain_c, gain_b), jnp.where(is_ctx, mc_ref[:, 0:d], shift_b))

    def project():
        return jnp.dot(h_scr[...], w_ref[...], preferred_element_type=F32)

    if rope_tiles:
        is_rope = functools.reduce(jnp.logical_or, [j == t for t in rope_tiles])

        @pl.when(is_rope)
        def _():
            acc_scr[...] = project()
            tn = acc_scr.shape[1]

            def rows_body(c, carry):
                rows = pl.ds(pl.multiple_of(c * ROPE_ROWS, ROPE_ROWS), ROPE_ROWS)
                cos = cos_ref[0, rows, :]
                sin = sin_ref[0, rows, :]
                for hd in range(tn // LANES):
                    a = acc_scr[rows, hd * LANES:(hd + 1) * LANES]
                    o_ref[0, rows, hd * LANES:(hd + 1) * LANES] = (
                        a * cos + pltpu.roll(a, LANES // 2, 1) * sin).astype(o_ref.dtype)
                return carry

            lax.fori_loop(0, tm // ROPE_ROWS, rows_body, 0, unroll=True)

        @pl.when(jnp.logical_not(is_rope))
        def _():
            o_ref[0] = project().astype(o_ref.dtype)
    else:
        o_ref[0] = project().astype(o_ref.dtype)


def _inproj(xall, modb, modc, g, w, rope, w_lr, *, n_lat, tn):
    b, r, d = xall.shape
    nc = w.shape[1]
    tm = r // ROW_SPLIT
    has_lr = w_lr is not None
    in_specs = [
        pl.BlockSpec((1, tm, d), lambda bi, k, j: (bi, k, 0)),
        pl.BlockSpec((1, 1, 3 * d), lambda bi, k, j: (bi, 0, 0)),
        pl.BlockSpec((1, 3 * d), lambda bi, k, j: (0, 0)),
        pl.BlockSpec((1, d), lambda bi, k, j: (0, 0)),
        pl.BlockSpec((d, tn), lambda bi, k, j: (0, j)),
    ]
    args = [xall, modb, modc, g.reshape(1, d), w]
    rope_tiles = ()
    if rope is not None:
        cos_t, sin_t, k_tile, q_tile = rope
        rope_tiles = (k_tile, q_tile)
        table = pl.BlockSpec((1, tm, LANES), lambda bi, k, j: (jnp.where(j == q_tile, 1, 0), k, 0))
        in_specs += [table, table]
        args += [cos_t, sin_t]
    out_specs = [pl.BlockSpec((1, tm, tn), lambda bi, k, j: (bi, k, j))]
    out_shape = [jax.ShapeDtypeStruct((b, r, nc), BF16)]
    if has_lr:
        in_specs.append(pl.BlockSpec((d, LANES), lambda bi, k, j: (0, 0)))
        args.append(w_lr)
        out_specs.append(pl.BlockSpec((1, tm, LANES), lambda bi, k, j: (bi, k, 0)))
        out_shape.append(jax.ShapeDtypeStruct((b, r, LANES), F32))
    res = pl.pallas_call(
        functools.partial(_inproj_kernel, tm=tm, d=d, n_lat=n_lat, rope_tiles=rope_tiles, has_lr=has_lr),
        grid=(b, ROW_SPLIT, nc // tn),
        in_specs=in_specs,
        out_specs=out_specs,
        out_shape=out_shape,
        scratch_shapes=[pltpu.VMEM((tm, d), BF16)] + ([pltpu.VMEM((tm, tn), F32)] if rope_tiles else []),
        compiler_params=_cparams(("arbitrary", "arbitrary", "arbitrary")),
        name="inproj",
    )(*args)
    return res if has_lr else (res[0], None)


def _outproj_kernel(ua_ref, ub_ref, w1_ref, w2_ref, x_ref, mb_ref, mc_ref, g_ref, o_ref, *, tm, d, n_lat):
    k = pl.program_id(1)
    y = jnp.dot(ua_ref[0], w1_ref[...], preferred_element_type=F32)
    y = y + jnp.dot(ub_ref[0], w2_ref[...], preferred_element_type=F32)
    y = y * lax.rsqrt(jnp.mean(y * y, axis=-1, keepdims=True) + EPS) * g_ref[...]
    row = k * tm + lax.broadcasted_iota(jnp.int32, (tm, 1), 0)
    gate = jnp.where(row >= n_lat, mc_ref[:, 2 * d:3 * d], mb_ref[0][:, 2 * d:3 * d])
    o_ref[0] = x_ref[0] + gate * y


def _outproj(ua, ub, w, xall, modb, modc, g, *, n_lat, latent_only):
    b, r, d = xall.shape
    bw = ua.shape[2]
    tm = n_lat // ROW_SPLIT if latent_only else r // ROW_SPLIT
    return pl.pallas_call(
        functools.partial(_outproj_kernel, tm=tm, d=d, n_lat=n_lat),
        grid=(b, ROW_SPLIT),
        in_specs=[
            pl.BlockSpec((1, tm, bw), lambda bi, k: (bi, k, 0)),
            pl.BlockSpec((1, tm, bw), lambda bi, k: (bi, k, 0)),
            pl.BlockSpec((bw, d), lambda bi, k: (0, 0)),
            pl.BlockSpec((bw, d), lambda bi, k: (1, 0)),
            pl.BlockSpec((1, tm, d), lambda bi, k: (bi, k, 0)),
            pl.BlockSpec((1, 1, 3 * d), lambda bi, k: (bi, 0, 0)),
            pl.BlockSpec((1, 3 * d), lambda bi, k: (0, 0)),
            pl.BlockSpec((1, d), lambda bi, k: (0, 0)),
        ],
        out_specs=pl.BlockSpec((1, tm, d), lambda bi, k: (bi, k, 0)),
        out_shape=jax.ShapeDtypeStruct((b, n_lat if latent_only else r, d), F32),
        input_output_aliases={} if latent_only else {4: 0},
        compiler_params=_cparams(("arbitrary", "arbitrary")),
        name="outproj",
    )(ua, ub, w, w, xall, modb, modc, g.reshape(1, d))


def _dattn_lambda(lam_ref, lam_init):
    lp = lam_ref[...]
    return (jnp.exp(jnp.sum(lp[0:1] * lp[1:2], axis=-1, keepdims=True))
            - jnp.exp(jnp.sum(lp[2:3] * lp[3:4], axis=-1, keepdims=True)) + lam_init)


def _dattn_split_maps(q):
    first = (lax.broadcasted_iota(jnp.int32, (1, LANES), 1) & (DA_QK // 2)) == 0
    zero = jnp.zeros_like(q)
    return jnp.concatenate([jnp.where(first, q, zero), jnp.where(first, zero, q)], axis=0)


def _dattn_finish(o, sub_ref, g_ref, o_ref, lam_init):
    o = o * lax.rsqrt(jnp.mean(o * o, axis=-1, keepdims=True) + EPS) * sub_ref[...] * (1.0 - lam_init)
    o_ref[0] = (o * _silu(g_ref[0].astype(F32))).astype(o_ref.dtype)


def _dattn_kernel(lam_ref, sub_ref, qn_ref, k_ref, vt_ref, g_ref, o_ref, sa_ref, ma_ref, sb_ref, mb_ref,
                  acc_ref, den_ref, *, n_blocks, lam_init):
    t = pl.program_id(0)
    lam = _dattn_lambda(lam_ref, lam_init)
    nt = (((1,), (1,)), ((), ()))
    tq = qn_ref.shape[1]

    def step(cur, prev, fin):
        if fin:
            o = acc_ref[...] * (1.0 / den_ref[...])
            _dattn_finish((o[:, :tq] - lam * o[:, tq:]).T, sub_ref, g_ref, o_ref, lam_init)
        r = k_ref.shape[1]
        bounds = list(range(0, r - ATT_KCHUNK + 1, ATT_KCHUNK)) + [r]
        if cur is not None:
            q2 = _dattn_split_maps(qn_ref[0])
        m_cur = l_prev = o_prev = None
        for lo, hi in zip(bounds[:-1], bounds[1:]):
            if cur is not None:
                s = lax.dot_general(k_ref[0, lo:hi, :], q2, nt, preferred_element_type=F32)
                cur[0][lo:hi, :] = s
                m = jnp.max(s, axis=0, keepdims=True)
                m_cur = m if m_cur is None else jnp.maximum(m_cur, m)
            if prev is not None:
                e = jnp.exp2(prev[0][lo:hi, :] - prev[1][...])
                l = jnp.sum(e, axis=0, keepdims=True)
                o = jnp.dot(vt_ref[0, 0, :, lo:hi], e.astype(BF16), preferred_element_type=F32)
                l_prev = l if l_prev is None else l_prev + l
                o_prev = o if o_prev is None else o_prev + o
        if cur is not None:
            cur[1][...] = m_cur
        if prev is not None:
            acc_ref[...] = o_prev
            den_ref[...] = l_prev

    bufs = ((sa_ref, ma_ref), (sb_ref, mb_ref))
    last = bufs[(n_blocks - 1) % 2]
    pl.when(t == 0)(lambda: step(bufs[0], None, False))
    pl.when(t == 1)(lambda: step(bufs[1], bufs[0], False))
    pl.when((t >= 2) & (t < n_blocks) & (t % 2 == 0))(lambda: step(bufs[0], bufs[1], True))
    pl.when((t >= 2) & (t < n_blocks) & (t % 2 == 1))(lambda: step(bufs[1], bufs[0], True))
    pl.when(t == n_blocks)(lambda: step(None, last, True))
    pl.when(t == n_blocks + 1)(lambda: step(None, None, True))


def _dattn_ctx_kernel(lam_ref, sub_ref, q_ref, k_ref, v_ref, g_ref, ua_ref, o_ref, *, lam_init):
    del ua_ref
    lam = _dattn_lambda(lam_ref, lam_init)
    nt = (((1,), (1,)), ((), ()))
    n = q_ref.shape[1]
    s = lax.dot_general(_dattn_split_maps(q_ref[0]), k_ref[0], nt, preferred_element_type=F32)
    e = jnp.exp2(s - jnp.max(s, axis=-1, keepdims=True))
    p = e / jnp.sum(e, axis=-1, keepdims=True)
    a = (p[:n] - lam * p[n:]).astype(BF16)
    _dattn_finish(jnp.dot(a, v_ref[0], preferred_element_type=F32), sub_ref, g_ref, o_ref, lam_init)


def _diff_attention(proj, vt, lam_p, subln, *, n_lat, lam_init):
    b, r, _ = proj.shape
    h = DA_HEADS
    tq = ATT_TQ
    nq = n_lat // tq
    n_ctx = r - n_lat
    sub = subln.reshape(1, LANES)
    small = [pl.BlockSpec(lam_p.shape, lambda *_: (0, 0)), pl.BlockSpec((1, LANES), lambda *_: (0, 0))]
    n_blocks = b * h * nq

    def unit(t, lag):
        u = jnp.clip(t - lag, 0, n_blocks - 1)
        return u // (h * nq), (u // nq) % h, u % nq

    def q_map(t):
        bi, hi, i = unit(t, 0)
        return bi, i, 2 * h + hi

    def k_map(t):
        bi, hi, _ = unit(t, 0)
        return bi, 0, hi

    def vt_map(t):
        bi, hi, _ = unit(t, 1)
        return bi, hi, 0, 0

    def g_map(t):
        bi, hi, i = unit(t, 2)
        return bi, i, 3 * h + hi

    def o_map(t):
        bi, hi, i = unit(t, 2)
        return bi, i, hi

    ua = pl.pallas_call(
        functools.partial(_dattn_kernel, n_blocks=n_blocks, lam_init=lam_init),
        grid=(n_blocks + 2,),
        in_specs=small + [
            pl.BlockSpec((1, tq, LANES), q_map),
            pl.BlockSpec((1, r, LANES), k_map),
            pl.BlockSpec((1, 1, LANES, r), vt_map),
            pl.BlockSpec((1, tq, LANES), g_map),
        ],
        out_specs=pl.BlockSpec((1, tq, LANES), o_map),
        out_shape=jax.ShapeDtypeStruct((b, r, h * LANES), BF16),
        scratch_shapes=[pltpu.VMEM((r, 2 * tq), F32), pltpu.VMEM((1, 2 * tq), F32)] * 2
        + [pltpu.VMEM((LANES, 2 * tq), F32), pltpu.VMEM((1, 2 * tq), F32)],
        compiler_params=_cparams(("arbitrary",)),
        name="diff_attention",
    )(lam_p, sub, proj, proj, vt, proj)
    cb = n_lat // n_ctx
    ctx_rows = lambda off: pl.BlockSpec((1, n_ctx, LANES), lambda bi, hi: (bi, cb, off + hi))
    return pl.pallas_call(
        functools.partial(_dattn_ctx_kernel, lam_init=lam_init),
        grid=(b, h),
        in_specs=small + [ctx_rows(2 * h), ctx_rows(0), ctx_rows(h), ctx_rows(3 * h),
                          pl.BlockSpec(memory_space=pl.ANY)],
        out_specs=ctx_rows(0),
        out_shape=jax.ShapeDtypeStruct(ua.shape, ua.dtype),
        input_output_aliases={6: 0},
        compiler_params=_cparams(("arbitrary", "arbitrary")),
        name="diff_attention_ctx",
    )(lam_p, sub, proj, proj, proj, proj, ua)


def _sconv_kernel(h_ref, b_ref, c_ref, g_ref, w_ref, o_ref, *, n_lat):
    r = h_ref.shape[1]
    p = c_ref[0].astype(F32) * h_ref[0].astype(F32)
    row = lax.broadcasted_iota(jnp.int32, (r, 1), 0)
    prev = jnp.where((row == 0) | (row == n_lat), 0.0, pltpu.roll(p, 1, 0))
    nxt = jnp.where((row == n_lat - 1) | (row == r - 1), 0.0, pltpu.roll(p, r - 1, 0))
    w = w_ref[...]
    conv = w[0:1] * prev + w[1:2] * p + w[2:3] * nxt
    o_ref[0] = (b_ref[0].astype(F32) * conv * _silu(g_ref[0].astype(F32))).astype(o_ref.dtype)


def _short_conv(proj, conv_w, *, n_lat):
    b, r, _ = proj.shape
    nb = conv_w.shape[1] // LANES
    base = 4 * nb
    spec = lambda off: pl.BlockSpec((1, r, LANES), lambda bi, j: (bi, 0, off + j))
    return pl.pallas_call(
        functools.partial(_sconv_kernel, n_lat=n_lat),
        grid=(b, nb),
        in_specs=[spec(base), spec(base + nb), spec(base + 2 * nb), spec(base + 3 * nb),
                  pl.BlockSpec((SC_WIDTH, LANES), lambda bi, j: (0, j))],
        out_specs=pl.BlockSpec((1, r, LANES), lambda bi, j: (bi, 0, j)),
        out_shape=jax.ShapeDtypeStruct((b, r, nb * LANES), BF16),
        compiler_params=_cparams(("arbitrary", "arbitrary")),
        name="short_conv",
    )(proj, proj, proj, proj, conv_w)


def _natten_kernel(q_ref, k_ref, v_ref, g_ref, tb_ref, o_ref, *, n_lat, n_rows):
    lane = lax.broadcasted_iota(jnp.int32, (1, LANES), 1)
    first = lane < NA_DIM
    nt = (((1,), (1,)), ((), ()))
    win = WIN_ROWS * GRID_W
    kc = k_ref[0, n_lat:, :]
    vc = v_ref[0, n_lat:, :]

    def stack_heads(q):
        q = q * (NA_DIM ** -0.5)
        zero = jnp.zeros_like(q)
        return jnp.concatenate([jnp.where(first, q, zero), jnp.where(first, zero, q)], axis=0)

    def softmax(parts):
        m = functools.reduce(jnp.maximum, [jnp.max(s, axis=-1, keepdims=True) for s in parts])
        es = [jnp.exp(s - m) for s in parts]
        l = functools.reduce(jnp.add, [jnp.sum(e, axis=-1, keepdims=True) for e in es])
        return [e.astype(BF16) for e in es], 1.0 / l

    def unstack_heads(o2):
        n = o2.shape[0] // 2
        return jnp.where(first, o2[:n], o2[n:])

    def group_body(gi, carry):
        rows = [gi * NA_GROUP + t for t in range(NA_GROUP)]
        starts = [jnp.clip(rr - WIN_ROWS // 2, 0, n_rows - WIN_ROWS) for rr in rows]
        qsl = [pl.ds(pl.multiple_of(rr * GRID_W, GRID_W), GRID_W) for rr in rows]
        ksl = [pl.ds(pl.multiple_of(rs * GRID_W, GRID_W), win) for rs in starts]
        scores = []
        for rr, rs, qs, ks in zip(rows, starts, qsl, ksl):
            q2 = stack_heads(q_ref[0, qs, :])
            s_loc = lax.dot_general(q2, k_ref[0, ks, :], nt, preferred_element_type=F32)
            s_loc = s_loc + tb_ref[0, rs - rr + (WIN_ROWS - 1)]
            s_ctx = lax.dot_general(q2, kc, nt, preferred_element_type=F32)
            scores.append((s_loc, s_ctx))
        weights = [softmax(list(sc)) for sc in scores]
        for (es, rl), qs, ks in zip(weights, qsl, ksl):
            o2 = jnp.dot(es[0], v_ref[0, ks, :], preferred_element_type=F32)
            o2 = o2 + jnp.dot(es[1], vc, preferred_element_type=F32)
            out = unstack_heads(o2 * rl)
            o_ref[0, qs, :] = (out * _silu(g_ref[0, qs, :].astype(F32))).astype(o_ref.dtype)
        return carry

    lax.fori_loop(0, n_rows // NA_GROUP, group_body, 0)

    s = lax.dot_general(stack_heads(q_ref[0, n_lat:, :]), kc, nt, preferred_element_type=F32)
    es, rl = softmax([s])
    outc = unstack_heads(jnp.dot(es[0], vc, preferred_element_type=F32) * rl)
    o_ref[0, n_lat:, :] = (outc * _silu(g_ref[0, n_lat:, :].astype(F32))).astype(o_ref.dtype)


def _neighbourhood_attention(proj, tb, *, n_lat, cols):
    b, r, _ = proj.shape
    nb = NA_HEADS // 2
    kb, vb, qb, gb = (c // LANES for c in cols)
    spec = lambda off: pl.BlockSpec((1, r, LANES), lambda bi, j: (bi, 0, off + j))
    return pl.pallas_call(
        functools.partial(_natten_kernel, n_lat=n_lat, n_rows=n_lat // GRID_W),
        grid=(b, nb),
        in_specs=[spec(qb), spec(kb), spec(vb), spec(gb),
                  pl.BlockSpec((1,) + tb.shape[1:], lambda bi, j: (j, 0, 0, 0))],
        out_specs=pl.BlockSpec((1, r, LANES), lambda bi, j: (bi, 0, j)),
        out_shape=jax.ShapeDtypeStruct((b, r, nb * LANES), BF16),
        compiler_params=_cparams(("arbitrary", "arbitrary")),
        name="neighbourhood_attention",
    )(proj, proj, proj, proj, tb)


def _natten_bias_table(rpb):
    qcol = np.arange(GRID_W)[:, None]
    kcol = np.arange(GRID_W)[None, :]
    win_start = np.clip(qcol - WIN_COLS // 2, 0, GRID_W - WIN_COLS)
    valid = (kcol >= win_start) & (kcol < win_start + WIN_COLS)
    pad = GRID_W - WIN_COLS
    padded = jnp.pad(rpb.astype(F32), ((0, 0), (0, 0), (pad, pad)))
    toe = jnp.stack([padded[:, :, GRID_W - 1 - q:2 * GRID_W - 1 - q] for q in range(GRID_W)], axis=2)
    toe = jnp.where(jnp.asarray(valid)[None, None], toe, -jnp.inf)
    t = jnp.stack([toe[:, off:off + WIN_ROWS] for off in range(WIN_ROWS)], axis=1)
    nh = rpb.shape[0]
    t = jnp.transpose(t, (0, 1, 3, 2, 4)).reshape(nh // 2, 2, WIN_ROWS, GRID_W, WIN_ROWS * GRID_W)
    return jnp.transpose(t, (0, 2, 1, 3, 4)).reshape(nh // 2, WIN_ROWS, 2 * GRID_W, WIN_ROWS * GRID_W)


def _gla_kernel(q_ref, k_ref, v_ref, g_ref, lr_ref, gu_ref, gb_ref, gn_ref, o_ref, of_scr, st_scr,
                *, n_lat, n_ctx, qscale):
    ch = GLA_CHUNK
    mid = ch // 2
    n_heads, dv, dk = st_scr.shape
    nt = (((1,), (1,)), ((), ()))
    tn = (((0,), (0,)), ((), ()))
    trow = lax.broadcasted_iota(jnp.int32, (ch, ch), 0)
    tcol = lax.broadcasted_iota(jnp.int32, (ch, ch), 1)
    srow = lax.broadcasted_iota(jnp.int32, (ch, 1), 0)

    def cumsum_rows(g, reverse):
        acc = g
        sh = 1
        while sh < ch:
            if reverse:
                acc = acc + jnp.where(srow < ch - sh, pltpu.roll(acc, ch - sh, 0), 0.0)
            else:
                acc = acc + jnp.where(srow >= sh, pltpu.roll(acc, sh, 0), 0.0)
            sh *= 2
        return acc

    def group(starts, dirn):
        reverse = dirn == 1
        keep = (tcol >= trow) if reverse else (tcol <= trow)
        rows_l = [pl.ds(s if isinstance(s, int) else pl.multiple_of(s, ch), ch) for s in starts]
        items = [(rows, hh, slice(hh * dk, (hh + 1) * dk), slice(hh * dv, (hh + 1) * dv))
                 for rows in rows_l for hh in range(n_heads)]
        zs = [jnp.dot(lr_ref[0, rows, :].astype(BF16), gu_ref[dirn, hh], preferred_element_type=F32)
              for rows, hh, _, _ in items]
        ops = []
        for (rows, hh, kc, _), z in zip(items, zs):
            g = jax.nn.log_sigmoid(z + gb_ref[dirn, hh]) * (1.0 / GLA_TAU)
            bcum = cumsum_rows(g, reverse)
            b_end = bcum[0:1] if reverse else bcum[ch - 1:ch]
            rho = bcum[mid:mid + 1]
            up = jnp.exp(bcum - rho)
            dn = jnp.exp(rho - bcum)
            qf = q_ref[0, rows, kc].astype(F32) * qscale
            kf = k_ref[0, rows, kc].astype(F32)
            ops.append(dict(
                q_state=(qf * (up * jnp.exp(rho))).astype(BF16),
                q_local=(qf * up).astype(BF16),
                k_local=(kf * dn).astype(BF16),
                k_state=(kf * (dn * jnp.exp(b_end - rho))).astype(BF16),
                decay=jnp.exp(b_end)))
        atts = [lax.dot_general(o["q_local"], o["k_local"], nt, preferred_element_type=F32) for o in ops]
        st_adds = [lax.dot_general(v_ref[0, rows, vc], o["k_state"], tn, preferred_element_type=F32)
                   for (rows, _, _, vc), o in zip(items, ops)]
        o_locals = [jnp.dot(jnp.where(keep, att, 0.0).astype(BF16), v_ref[0, rows, vc], preferred_element_type=F32)
                    for (rows, _, _, vc), att in zip(items, atts)]
        sts = [st_scr[hh] for hh in range(n_heads)]
        states = []
        for (_, hh, _, _), o, st_add in zip(items, ops, st_adds):
            states.append(sts[hh].astype(BF16))
            sts[hh] = sts[hh] * o["decay"] + st_add
        for hh in range(n_heads):
            st_scr[hh] = sts[hh]
        outs = [o_local + lax.dot_general(o["q_state"], s_in, nt, preferred_element_type=F32)
                for o, o_local, s_in in zip(ops, o_locals, states)]
        for (rows, hh, _, vc), o in zip(items, outs):
            if dirn == 0:
                of_scr[rows, vc] = o
            else:
                o = o + of_scr[rows, vc]
                o = o * lax.rsqrt(jnp.mean(o * o, axis=-1, keepdims=True) + EPS) * gn_ref[...]
                o_ref[0, rows, vc] = (o * _silu(g_ref[0, rows, vc].astype(F32))).astype(o_ref.dtype)

    n_cc = n_ctx // ch
    n_lg = n_lat // (ch * GLA_GROUP)
    gsz = ch * GLA_GROUP

    def latent_group(gi, dirn):
        if dirn == 0:
            group([gi * gsz + t * ch for t in range(GLA_GROUP)], 0)
        else:
            group([(n_lg - 1 - gi) * gsz + t * ch for t in range(GLA_GROUP - 1, -1, -1)], 1)

    st_scr[...] = jnp.zeros_like(st_scr)
    group([n_lat + c * ch for c in range(n_cc)], 0)
    lax.fori_loop(0, n_lg, lambda gi, carry: (latent_group(gi, 0), carry)[1], 0)
    st_scr[...] = jnp.zeros_like(st_scr)
    group([n_lat + c * ch for c in range(n_cc - 1, -1, -1)], 1)
    lax.fori_loop(0, n_lg, lambda gi, carry: (latent_group(gi, 1), carry)[1], 0)


def _gla(proj, lr, gu, gb, gnorm, *, n_lat, cols):
    b, r, _ = proj.shape
    nh = GLA_HEADS
    dk = LANES
    dv = 2 * LANES
    kcol, vcol, qcol, gcol = cols
    hs = GLA_HEADS_PER_STEP
    wk, wv = hs * dk, hs * dv
    return pl.pallas_call(
        functools.partial(_gla_kernel, n_lat=n_lat, n_ctx=r - n_lat, qscale=dk ** -0.5),
        grid=(b, nh // hs),
        in_specs=[
            pl.BlockSpec((1, r, wk), lambda bi, hi: (bi, 0, qcol // wk + hi)),
            pl.BlockSpec((1, r, wk), lambda bi, hi: (bi, 0, kcol // wk + hi)),
            pl.BlockSpec((1, r, wv), lambda bi, hi: (bi, 0, vcol // wv + hi)),
            pl.BlockSpec((1, r, wv), lambda bi, hi: (bi, 0, gcol // wv + hi)),
            pl.BlockSpec((1, r, LANES), lambda bi, hi: (bi, 0, 0)),
            pl.BlockSpec((2, hs, LANES, dk), lambda bi, hi: (0, hi, 0, 0)),
            pl.BlockSpec((2, hs, 1, dk), lambda bi, hi: (0, hi, 0, 0)),
            pl.BlockSpec((1, dv), lambda bi, hi: (0, 0)),
        ],
        out_specs=pl.BlockSpec((1, r, wv), lambda bi, hi: (bi, 0, hi)),
        out_shape=jax.ShapeDtypeStruct((b, r, nh * dv), BF16),
        scratch_shapes=[pltpu.VMEM((r, wv), F32), pltpu.VMEM((hs, dv, dk), F32)],
        compiler_params=_cparams(("arbitrary", "arbitrary")),
        name="gla",
    )(proj, proj, proj, proj, lr, gu, gb, gnorm.reshape(1, dv))


def _rope_tables(n_lat, n_ctx):
    half = DA_QK // 2
    inv = 1.0 / (ROPE_BASE ** (jnp.arange(0, half, 2, dtype=F32) / half))
    t = jnp.arange(n_lat)
    row = (t // GRID_W).astype(F32)[:, None] * inv
    col = (t % GRID_W).astype(F32)[:, None] * inv
    ang = jnp.concatenate([row, row, col, col], axis=-1)
    lane = np.arange(LANES)
    dim = 32 * ((lane >> 4) & 1) + 16 * (lane >> 6) + (lane & 15)
    sign = jnp.asarray(np.where((lane >> 6) == 0, -1.0, 1.0), F32)
    cos = jnp.concatenate([jnp.cos(ang)[:, dim], jnp.ones((n_ctx, LANES), F32)], axis=0)
    sin = jnp.concatenate([jnp.sin(ang)[:, dim] * sign, jnp.zeros((n_ctx, LANES), F32)], axis=0)
    qf = math.log2(math.e) * DA_QK ** -0.5
    return jnp.stack([cos, cos * qf]), jnp.stack([sin, sin * qf])


def _pair_layout(w):
    d = w.shape[0]
    w = w.reshape(d, -1, 2, 2, 2, DA_QK // 4)
    return jnp.transpose(w, (0, 1, 4, 2, 3, 5)).reshape(d, -1)


def kernel(x, c, ctx, c_ctx, w_mod, b_mod, g_pre, g_post, w_out, ev_w_in, ev_lambda, ev_subln, ev_conv,
           od_w_in, od_rpb, od_gate_up, od_gate_bias, od_gnorm):
    b, s, d = x.shape
    n_ctx = ctx.shape[1]
    depth = w_mod.shape[0]
    bw = EXPAND * d // 2
    assert bw == DA_HEADS * LANES == NA_HEADS * NA_DIM and (s + n_ctx) % (ROW_SPLIT * 16) == 0
    assert s % ATT_TQ == 0 and n_ctx == ATT_TQ and s % GRID_W == 0 and n_ctx % GLA_CHUNK == 0
    tm = (s + n_ctx) // ROW_SPLIT
    assert tm % ROPE_ROWS == 0 and (ROW_SPLIT - 1) * tm <= s and s % (ROW_SPLIT * 16) == 0

    xall = jnp.concatenate([x, ctx], axis=1)
    nrow = -(-(b + 1) // 8) * 8
    cc = jnp.zeros((nrow, d), F32).at[:b].set(c).at[b].set(c_ctx)
    mod = _modulation(cc, w_mod, b_mod)
    cos_t, sin_t = _rope_tables(s, n_ctx)

    gk = bw // 2
    od_lr0 = 3 * bw + gk
    od_cols = np.cumsum([0, bw, bw, gk, bw, bw, gk, bw])
    na_cols = (int(od_cols[0]), int(od_cols[1]), int(od_cols[4]), int(od_cols[6]))
    gla_cols = (int(od_cols[2]), int(od_cols[3]), int(od_cols[5]), int(od_cols[7]))

    for l in range(depth):
        j = l // 2
        modb = mod[l, :b].reshape(b, 1, 3 * d)
        modc = mod[l, b:b + 1]
        if l % 2 == 0:
            lam_init = 0.8 - 0.6 * math.exp(-0.3 * l)
            w = ev_w_in[j]
            w = jnp.concatenate([_pair_layout(w[:, :bw]), w[:, bw:2 * bw], _pair_layout(w[:, 2 * bw:3 * bw]),
                                 w[:, 3 * bw:]], axis=1).astype(BF16)
            proj, _ = _inproj(xall, modb, modc, g_pre[l], w, (cos_t, sin_t, 0, 2), None, n_lat=s, tn=bw)
            vt = proj[:, :, bw:2 * bw].reshape(b, s + n_ctx, DA_HEADS, LANES).transpose(0, 2, 3, 1)
            ua = _diff_attention(proj, vt, ev_lambda[j], ev_subln[j], n_lat=s, lam_init=lam_init)
            ub = _short_conv(proj, ev_conv[j], n_lat=s)
        else:
            w = od_w_in[j]
            w_lr = jnp.pad(w[:, od_lr0:od_lr0 + 2 * GLA_RANK], ((0, 0), (0, LANES - 2 * GLA_RANK))).astype(BF16)
            w = jnp.concatenate([w[:, :od_lr0], w[:, od_lr0 + 2 * GLA_RANK:]], axis=1).astype(BF16)
            proj, lr = _inproj(xall, modb, modc, g_pre[l], w, None, w_lr, n_lat=s, tn=bw)
            tb = _natten_bias_table(od_rpb[j])
            ua = _neighbourhood_attention(proj, tb, n_lat=s, cols=na_cols)
            gu = jnp.zeros((2, GLA_HEADS, LANES, LANES), F32)
            gup = od_gate_up[j].reshape(2, GLA_RANK, GLA_HEADS, LANES).transpose(0, 2, 1, 3)
            gu = gu.at[0, :, :GLA_RANK].set(gup[0]).at[1, :, GLA_RANK:2 * GLA_RANK].set(gup[1]).astype(BF16)
            gb = od_gate_bias[j].reshape(2, GLA_HEADS, 1, LANES)
            ub = _gla(proj, lr, gu, gb, od_gnorm[j], n_lat=s, cols=gla_cols)
        xall = _outproj(ua, ub, w_out[l].astype(BF16), xall, modb, modc, g_post[l], n_lat=s,
                        latent_only=l == depth - 1)
    return xall
```

```python
import functools
import math

import numpy as np
import jax
import jax.numpy as jnp
from jax import lax
from jax.experimental import pallas as pl
from jax.experimental.pallas import tpu as pltpu

F32 = jnp.float32
BF16 = jnp.bfloat16

LANES = 128
GRID_W = 64
EXPAND = 2
DA_HEADS = 8
DA_QK = 64
SC_WIDTH = 3
NA_HEADS = 16
NA_DIM = 64
WIN_ROWS = 8
WIN_COLS = 16
GLA_HEADS = 4
GLA_RANK = 16
GLA_TAU = 16.0
GLA_CHUNK = 64
ROPE_BASE = 10000.0
EPS = 1e-6
ROW_SPLIT = 4
ROPE_ROWS = 64
ATT_TQ = 512
ATT_KCHUNK = 256
NA_GROUP = 4
GLA_GROUP = 4
GLA_HEADS_PER_STEP = 2
VMEM_LIMIT = 56 * 1024 * 1024


def _cparams(sem):
    return pltpu.CompilerParams(dimension_semantics=sem, vmem_limit_bytes=VMEM_LIMIT)


def _silu(t):
    return t * jax.nn.sigmoid(t)


def _mod_kernel(c_ref, w_ref, b_ref, o_ref):
    s = _silu(c_ref[...])
    o_ref[0] = jnp.dot(s, w_ref[0], preferred_element_type=F32) + b_ref[0]


def _modulation(cc, w_mod, b_mod):
    depth, d, d3 = w_mod.shape
    nrow = cc.shape[0]
    tn = d
    return pl.pallas_call(
        _mod_kernel,
        grid=(depth, d3 // tn),
        in_specs=[
            pl.BlockSpec((nrow, d), lambda l, j: (0, 0)),
            pl.BlockSpec((1, d, tn), lambda l, j: (l, 0, j)),
            pl.BlockSpec((1, 1, tn), lambda l, j: (l, 0, j)),
        ],
        out_specs=pl.BlockSpec((1, nrow, tn), lambda l, j: (l, 0, j)),
        out_shape=jax.ShapeDtypeStruct((depth, nrow, d3), F32),
        compiler_params=_cparams(("arbitrary", "arbitrary")),
        name="modulation",
    )(cc, w_mod, b_mod.reshape(depth, 1, d3))


def _inproj_kernel(*refs, tm, d, n_lat, rope_tiles, has_lr):
    x_ref, mb_ref, mc_ref, g_ref, w_ref = refs[:5]
    rest = list(refs[5:])
    cos_ref, sin_ref = (rest.pop(0), rest.pop(0)) if rope_tiles else (None, None)
    wlr_ref = rest.pop(0) if has_lr else None
    o_ref = rest.pop(0)
    lr_ref = rest.pop(0) if has_lr else None
    h_scr = rest.pop(0)
    acc_scr = rest.pop(0) if rope_tiles else None
    k = pl.program_id(1)
    j = pl.program_id(2)
    n_blk = pl.num_programs(1)

    @pl.when(j == 0)
    def _():
        xv = x_ref[0]
        xn = xv * lax.rsqrt(jnp.mean(xv * xv, axis=-1, keepdims=True) + EPS)
        mb = mb_ref[0]
        gain_b = g_ref[...] * (1.0 + mb[:, d:2 * d])
        shift_b = mb[:, 0:d]

        def emit(gain, shift):
            h_scr[...] = (xn * gain + shift).astype(BF16)
            if has_lr:
                lr_ref[0] = jnp.dot(h_scr[...], wlr_ref[...], preferred_element_type=F32)

        pl.when(k < n_blk - 1)(lambda: emit(gain_b, shift_b))

        @pl.when(k == n_blk - 1)
        def _():
            is_ctx = k * tm + lax.broadcasted_iota(jnp.int32, (tm, 1), 0) >= n_lat
            gain_c = g_ref[...] * (1.0 + mc_ref[:, d:2 * d])
            emit(jnp.where(is_ctx, gain_c, gain_b), jnp.where(is_ctx, mc_ref[:, 0:d], shift_b))

    tn = o_ref.shape[2]

    def project():
        w_tile = w_ref[:, pl.ds(pl.multiple_of(j * tn, tn), tn)]
        return jnp.dot(h_scr[...], w_tile, preferred_element_type=F32)

    if rope_tiles:
        is_rope = functools.reduce(jnp.logical_or, [j == t for t in rope_tiles])

        @pl.when(is_rope)
        def _():
            acc_scr[...] = project()

            def rows_body(c, carry):
                rows = pl.ds(pl.multiple_of(c * ROPE_ROWS, ROPE_ROWS), ROPE_ROWS)
                cos = cos_ref[0, rows, :]
                sin = sin_ref[0, rows, :]
                for hd in range(tn // LANES):
                    a = acc_scr[rows, hd * LANES:(hd + 1) * LANES]
                    o_ref[0, rows, hd * LANES:(hd + 1) * LANES] = (
                        a * cos + pltpu.roll(a, LANES // 2, 1) * sin).astype(o_ref.dtype)
                return carry

            lax.fori_loop(0, tm // ROPE_ROWS, rows_body, 0, unroll=True)

        @pl.when(jnp.logical_not(is_rope))
        def _():
            o_ref[0] = project().astype(o_ref.dtype)
    else:
        o_ref[0] = project().astype(o_ref.dtype)


def _inproj(xall, modb, modc, g, w, rope, w_lr, *, n_lat, tn):
    b, r, d = xall.shape
    nc = w.shape[1]
    tm = r // ROW_SPLIT
    has_lr = w_lr is not None
    in_specs = [
        pl.BlockSpec((1, tm, d), lambda bi, k, j: (bi, k, 0)),
        pl.BlockSpec((1, 1, 3 * d), lambda bi, k, j: (bi, 0, 0)),
        pl.BlockSpec((1, 3 * d), lambda bi, k, j: (0, 0)),
        pl.BlockSpec((1, d), lambda bi, k, j: (0, 0)),
        pl.BlockSpec((d, nc), lambda bi, k, j: (0, 0), pipeline_mode=pl.Buffered(1)),
    ]
    args = [xall, modb, modc, g.reshape(1, d), w]
    rope_tiles = ()
    if rope is not None:
        cos_t, sin_t, k_tile, q_tile = rope
        rope_tiles = (k_tile, q_tile)
        table = pl.BlockSpec((1, tm, LANES), lambda bi, k, j: (jnp.where(j == q_tile, 1, 0), k, 0))
        in_specs += [table, table]
        args += [cos_t, sin_t]
    out_specs = [pl.BlockSpec((1, tm, tn), lambda bi, k, j: (bi, k, j))]
    out_shape = [jax.ShapeDtypeStruct((b, r, nc), BF16)]
    if has_lr:
        in_specs.append(pl.BlockSpec((d, LANES), lambda bi, k, j: (0, 0)))
        args.append(w_lr)
        out_specs.append(pl.BlockSpec((1, tm, LANES), lambda bi, k, j: (bi, k, 0)))
        out_shape.append(jax.ShapeDtypeStruct((b, r, LANES), F32))
    res = pl.pallas_call(
        functools.partial(_inproj_kernel, tm=tm, d=d, n_lat=n_lat, rope_tiles=rope_tiles, has_lr=has_lr),
        grid=(b, ROW_SPLIT, nc // tn),
        in_specs=in_specs,
        out_specs=out_specs,
        out_shape=out_shape,
        scratch_shapes=[pltpu.VMEM((tm, d), BF16)] + ([pltpu.VMEM((tm, tn), F32)] if rope_tiles else []),
        compiler_params=_cparams(("arbitrary", "arbitrary", "arbitrary")),
        name="inproj",
    )(*args)
    return res if has_lr else (res[0], None)


def _outproj_kernel(ua_ref, ub_ref, w1_ref, w2_ref, x_ref, mb_ref, mc_ref, g_ref, o_ref, *, tm, d, n_lat):
    k = pl.program_id(1)
    y = jnp.dot(ua_ref[0], w1_ref[...], preferred_element_type=F32)
    y = y + jnp.dot(ub_ref[0], w2_ref[...], preferred_element_type=F32)
    y = y * lax.rsqrt(jnp.mean(y * y, axis=-1, keepdims=True) + EPS) * g_ref[...]
    row = k * tm + lax.broadcasted_iota(jnp.int32, (tm, 1), 0)
    gate = jnp.where(row >= n_lat, mc_ref[:, 2 * d:3 * d], mb_ref[0][:, 2 * d:3 * d])
    o_ref[0] = x_ref[0] + gate * y


def _outproj(ua, ub, w, xall, modb, modc, g, *, n_lat, latent_only):
    b, r, d = xall.shape
    bw = ua.shape[2]
    tm = n_lat // ROW_SPLIT if latent_only else r // ROW_SPLIT
    return pl.pallas_call(
        functools.partial(_outproj_kernel, tm=tm, d=d, n_lat=n_lat),
        grid=(b, ROW_SPLIT),
        in_specs=[
            pl.BlockSpec((1, tm, bw), lambda bi, k: (bi, k, 0)),
            pl.BlockSpec((1, tm, bw), lambda bi, k: (bi, k, 0)),
            pl.BlockSpec((bw, d), lambda bi, k: (0, 0)),
            pl.BlockSpec((bw, d), lambda bi, k: (1, 0)),
            pl.BlockSpec((1, tm, d), lambda bi, k: (bi, k, 0)),
            pl.BlockSpec((1, 1, 3 * d), lambda bi, k: (bi, 0, 0)),
            pl.BlockSpec((1, 3 * d), lambda bi, k: (0, 0)),
            pl.BlockSpec((1, d), lambda bi, k: (0, 0)),
        ],
        out_specs=pl.BlockSpec((1, tm, d), lambda bi, k: (bi, k, 0)),
        out_shape=jax.ShapeDtypeStruct((b, n_lat if latent_only else r, d), F32),
        input_output_aliases={} if latent_only else {4: 0},
        compiler_params=_cparams(("arbitrary", "arbitrary")),
        name="outproj",
    )(ua, ub, w, w, xall, modb, modc, g.reshape(1, d))


def _dattn_lambda(lam_ref, lam_init):
    lp = lam_ref[...]
    return (jnp.exp(jnp.sum(lp[0:1] * lp[1:2], axis=-1, keepdims=True))
            - jnp.exp(jnp.sum(lp[2:3] * lp[3:4], axis=-1, keepdims=True)) + lam_init)


def _dattn_split_maps(q):
    first = (lax.broadcasted_iota(jnp.int32, (1, LANES), 1) & (DA_QK // 2)) == 0
    zero = jnp.zeros_like(q)
    return jnp.concatenate([jnp.where(first, q, zero), jnp.where(first, zero, q)], axis=0)


def _dattn_finish(o, sub_ref, g_ref, o_ref, lam_init):
    o = o * lax.rsqrt(jnp.mean(o * o, axis=-1, keepdims=True) + EPS) * sub_ref[...] * (1.0 - lam_init)
    o_ref[0] = (o * _silu(g_ref[0].astype(F32))).astype(o_ref.dtype)


def _dattn_kernel(lam_ref, sub_ref, qn_ref, k_ref, vt_ref, g_ref, o_ref, sa_ref, ma_ref, sb_ref, mb_ref,
                  acc_ref, den_ref, *, n_blocks, lam_init):
    t = pl.program_id(0)
    lam = _dattn_lambda(lam_ref, lam_init)
    nt = (((1,), (1,)), ((), ()))
    tq = qn_ref.shape[1]

    def step(cur, prev, fin):
        if fin:
            o = acc_ref[...] * (1.0 / den_ref[...])
            _dattn_finish((o[:, :tq] - lam * o[:, tq:]).T, sub_ref, g_ref, o_ref, lam_init)
        r = k_ref.shape[1]
        bounds = list(range(0, r - ATT_KCHUNK + 1, ATT_KCHUNK)) + [r]
        if cur is not None:
            q2 = _dattn_split_maps(qn_ref[0])
        m_cur = l_prev = o_prev = None
        for lo, hi in zip(bounds[:-1], bounds[1:]):
            if cur is not None:
                s = lax.dot_general(k_ref[0, lo:hi, :], q2, nt, preferred_element_type=F32)
                cur[0][lo:hi, :] = s
                m = jnp.max(s, axis=0, keepdims=True)
                m_cur = m if m_cur is None else jnp.maximum(m_cur, m)
            if prev is not None:
                e = jnp.exp2(prev[0][lo:hi, :] - prev[1][...])
                l = jnp.sum(e, axis=0, keepdims=True)
                o = jnp.dot(vt_ref[0, 0, :, lo:hi], e.astype(BF16), preferred_element_type=F32)
                l_prev = l if l_prev is None else l_prev + l
                o_prev = o if o_prev is None else o_prev + o
        if cur is not None:
            cur[1][...] = m_cur
        if prev is not None:
            acc_ref[...] = o_prev
            den_ref[...] = l_prev

    bufs = ((sa_ref, ma_ref), (sb_ref, mb_ref))
    last = bufs[(n_blocks - 1) % 2]
    pl.when(t == 0)(lambda: step(bufs[0], None, False))
    pl.when(t == 1)(lambda: step(bufs[1], bufs[0], False))
    pl.when((t >= 2) & (t < n_blocks) & (t % 2 == 0))(lambda: step(bufs[0], bufs[1], True))
    pl.when((t >= 2) & (t < n_blocks) & (t % 2 == 1))(lambda: step(bufs[1], bufs[0], True))
    pl.when(t == n_blocks)(lambda: step(None, last, True))
    pl.when(t == n_blocks + 1)(lambda: step(None, None, True))


def _dattn_ctx_kernel(lam_ref, sub_ref, q_ref, k_ref, v_ref, g_ref, ua_ref, o_ref, *, lam_init):
    del ua_ref
    lam = _dattn_lambda(lam_ref, lam_init)
    nt = (((1,), (1,)), ((), ()))
    n = q_ref.shape[1]
    s = lax.dot_general(_dattn_split_maps(q_ref[0]), k_ref[0], nt, preferred_element_type=F32)
    e = jnp.exp2(s - jnp.max(s, axis=-1, keepdims=True))
    p = e / jnp.sum(e, axis=-1, keepdims=True)
    a = (p[:n] - lam * p[n:]).astype(BF16)
    _dattn_finish(jnp.dot(a, v_ref[0], preferred_element_type=F32), sub_ref, g_ref, o_ref, lam_init)


def _diff_attention(proj, vt, lam_p, subln, *, n_lat, lam_init):
    b, r, _ = proj.shape
    h = DA_HEADS
    tq = ATT_TQ
    nq = n_lat // tq
    n_ctx = r - n_lat
    sub = subln.reshape(1, LANES)
    small = [pl.BlockSpec(lam_p.shape, lambda *_: (0, 0)), pl.BlockSpec((1, LANES), lambda *_: (0, 0))]
    n_blocks = b * h * nq

    def unit(t, lag):
        u = jnp.clip(t - lag, 0, n_blocks - 1)
        return u // (h * nq), (u // nq) % h, u % nq

    def q_map(t):
        bi, hi, i = unit(t, 0)
        return bi, i, 2 * h + hi

    def k_map(t):
        bi, hi, _ = unit(t, 0)
        return bi, 0, hi

    def vt_map(t):
        bi, hi, _ = unit(t, 1)
        return bi, hi, 0, 0

    def g_map(t):
        bi, hi, i = unit(t, 2)
        return bi, i, 3 * h + hi

    def o_map(t):
        bi, hi, i = unit(t, 2)
        return bi, i, hi

    ua = pl.pallas_call(
        functools.partial(_dattn_kernel, n_blocks=n_blocks, lam_init=lam_init),
        grid=(n_blocks + 2,),
        in_specs=small + [
            pl.BlockSpec((1, tq, LANES), q_map),
            pl.BlockSpec((1, r, LANES), k_map),
            pl.BlockSpec((1, 1, LANES, r), vt_map),
            pl.BlockSpec((1, tq, LANES), g_map),
        ],
        out_specs=pl.BlockSpec((1, tq, LANES), o_map),
        out_shape=jax.ShapeDtypeStruct((b, r, h * LANES), BF16),
        scratch_shapes=[pltpu.VMEM((r, 2 * tq), F32), pltpu.VMEM((1, 2 * tq), F32)] * 2
        + [pltpu.VMEM((LANES, 2 * tq), F32), pltpu.VMEM((1, 2 * tq), F32)],
        compiler_params=_cparams(("arbitrary",)),
        name="diff_attention",
    )(lam_p, sub, proj, proj, vt, proj)
    cb = n_lat // n_ctx
    ctx_rows = lambda off: pl.BlockSpec((1, n_ctx, LANES), lambda bi, hi: (bi, cb, off + hi))
    return pl.pallas_call(
        functools.partial(_dattn_ctx_kernel, lam_init=lam_init),
        grid=(b, h),
        in_specs=small + [ctx_rows(2 * h), ctx_rows(0), ctx_rows(h), ctx_rows(3 * h),
                          pl.BlockSpec(memory_space=pl.ANY)],
        out_specs=ctx_rows(0),
        out_shape=jax.ShapeDtypeStruct(ua.shape, ua.dtype),
        input_output_aliases={6: 0},
        compiler_params=_cparams(("arbitrary", "arbitrary")),
        name="diff_attention_ctx",
    )(lam_p, sub, proj, proj, proj, proj, ua)


def _sconv_kernel(h_ref, b_ref, c_ref, g_ref, w_ref, o_ref, *, n_lat):
    r = h_ref.shape[1]
    p = c_ref[0].astype(F32) * h_ref[0].astype(F32)
    row = lax.broadcasted_iota(jnp.int32, (r, 1), 0)
    prev = jnp.where((row == 0) | (row == n_lat), 0.0, pltpu.roll(p, 1, 0))
    nxt = jnp.where((row == n_lat - 1) | (row == r - 1), 0.0, pltpu.roll(p, r - 1, 0))
    w = w_ref[...]
    conv = w[0:1] * prev + w[1:2] * p + w[2:3] * nxt
    o_ref[0] = (b_ref[0].astype(F32) * conv * _silu(g_ref[0].astype(F32))).astype(o_ref.dtype)


def _short_conv(proj, conv_w, *, n_lat):
    b, r, _ = proj.shape
    nb = conv_w.shape[1] // LANES
    base = 4 * nb
    spec = lambda off: pl.BlockSpec((1, r, LANES), lambda bi, j: (bi, 0, off + j))
    return pl.pallas_call(
        functools.partial(_sconv_kernel, n_lat=n_lat),
        grid=(b, nb),
        in_specs=[spec(base), spec(base + nb), spec(base + 2 * nb), spec(base + 3 * nb),
                  pl.BlockSpec((SC_WIDTH, LANES), lambda bi, j: (0, j))],
        out_specs=pl.BlockSpec((1, r, LANES), lambda bi, j: (bi, 0, j)),
        out_shape=jax.ShapeDtypeStruct((b, r, nb * LANES), BF16),
        compiler_params=_cparams(("arbitrary", "arbitrary")),
        name="short_conv",
    )(proj, proj, proj, proj, conv_w)


def _natten_kernel(q_ref, k_ref, v_ref, g_ref, tb_ref, o_ref, *, n_lat, n_rows):
    lane = lax.broadcasted_iota(jnp.int32, (1, LANES), 1)
    first = lane < NA_DIM
    nt = (((1,), (1,)), ((), ()))
    win = WIN_ROWS * GRID_W
    kc = k_ref[0, n_lat:, :]
    vc = v_ref[0, n_lat:, :]

    def stack_heads(q):
        q = q * (NA_DIM ** -0.5)
        zero = jnp.zeros_like(q)
        return jnp.concatenate([jnp.where(first, q, zero), jnp.where(first, zero, q)], axis=0)

    def softmax(parts):
        m = functools.reduce(jnp.maximum, [jnp.max(s, axis=-1, keepdims=True) for s in parts])
        es = [jnp.exp(s - m) for s in parts]
        l = functools.reduce(jnp.add, [jnp.sum(e, axis=-1, keepdims=True) for e in es])
        return [e.astype(BF16) for e in es], 1.0 / l

    def unstack_heads(o2):
        n = o2.shape[0] // 2
        return jnp.where(first, o2[:n], o2[n:])

    def group_body(gi, carry):
        rows = [gi * NA_GROUP + t for t in range(NA_GROUP)]
        starts = [jnp.clip(rr - WIN_ROWS // 2, 0, n_rows - WIN_ROWS) for rr in rows]
        qsl = [pl.ds(pl.multiple_of(rr * GRID_W, GRID_W), GRID_W) for rr in rows]
        ksl = [pl.ds(pl.multiple_of(rs * GRID_W, GRID_W), win) for rs in starts]
        scores = []
        for rr, rs, qs, ks in zip(rows, starts, qsl, ksl):
            q2 = stack_heads(q_ref[0, qs, :])
            s_loc = lax.dot_general(q2, k_ref[0, ks, :], nt, preferred_element_type=F32)
            s_loc = s_loc + tb_ref[0, rs - rr + (WIN_ROWS - 1)]
            s_ctx = lax.dot_general(q2, kc, nt, preferred_element_type=F32)
            scores.append((s_loc, s_ctx))
        weights = [softmax(list(sc)) for sc in scores]
        for (es, rl), qs, ks in zip(weights, qsl, ksl):
            o2 = jnp.dot(es[0], v_ref[0, ks, :], preferred_element_type=F32)
            o2 = o2 + jnp.dot(es[1], vc, preferred_element_type=F32)
            out = unstack_heads(o2 * rl)
            o_ref[0, qs, :] = (out * _silu(g_ref[0, qs, :].astype(F32))).astype(o_ref.dtype)
        return carry

    lax.fori_loop(0, n_rows // NA_GROUP, group_body, 0)

    s = lax.dot_general(stack_heads(q_ref[0, n_lat:, :]), kc, nt, preferred_element_type=F32)
    es, rl = softmax([s])
    outc = unstack_heads(jnp.dot(es[0], vc, preferred_element_type=F32) * rl)
    o_ref[0, n_lat:, :] = (outc * _silu(g_ref[0, n_lat:, :].astype(F32))).astype(o_ref.dtype)


def _neighbourhood_attention(proj, tb, *, n_lat, cols):
    b, r, _ = proj.shape
    nb = NA_HEADS // 2
    kb, vb, qb, gb = (c // LANES for c in cols)
    spec = lambda off: pl.BlockSpec((1, r, LANES), lambda bi, j: (bi, 0, off + j))
    return pl.pallas_call(
        functools.partial(_natten_kernel, n_lat=n_lat, n_rows=n_lat // GRID_W),
        grid=(b, nb),
        in_specs=[spec(qb), spec(kb), spec(vb), spec(gb),
                  pl.BlockSpec((1,) + tb.shape[1:], lambda bi, j: (j, 0, 0, 0))],
        out_specs=pl.BlockSpec((1, r, LANES), lambda bi, j: (bi, 0, j)),
        out_shape=jax.ShapeDtypeStruct((b, r, nb * LANES), BF16),
        compiler_params=_cparams(("arbitrary", "arbitrary")),
        name="neighbourhood_attention",
    )(proj, proj, proj, proj, tb)


def _natten_bias_table(rpb):
    qcol = np.arange(GRID_W)[:, None]
    kcol = np.arange(GRID_W)[None, :]
    win_start = np.clip(qcol - WIN_COLS // 2, 0, GRID_W - WIN_COLS)
    valid = (kcol >= win_start) & (kcol < win_start + WIN_COLS)
    pad = GRID_W - WIN_COLS
    padded = jnp.pad(rpb.astype(F32), ((0, 0), (0, 0), (pad, pad)))
    toe = jnp.stack([padded[:, :, GRID_W - 1 - q:2 * GRID_W - 1 - q] for q in range(GRID_W)], axis=2)
    toe = jnp.where(jnp.asarray(valid)[None, None], toe, -jnp.inf)
    t = jnp.stack([toe[:, off:off + WIN_ROWS] for off in range(WIN_ROWS)], axis=1)
    nh = rpb.shape[0]
    t = jnp.transpose(t, (0, 1, 3, 2, 4)).reshape(nh // 2, 2, WIN_ROWS, GRID_W, WIN_ROWS * GRID_W)
    return jnp.transpose(t, (0, 2, 1, 3, 4)).reshape(nh // 2, WIN_ROWS, 2 * GRID_W, WIN_ROWS * GRID_W)


def _gla_kernel(q_ref, k_ref, v_ref, g_ref, lr_ref, gu_ref, gb_ref, gn_ref, o_ref, of_scr, st_scr,
                *, n_lat, n_ctx, qscale):
    ch = GLA_CHUNK
    mid = ch // 2
    n_heads, dv, dk = st_scr.shape
    nt = (((1,), (1,)), ((), ()))
    tn = (((0,), (0,)), ((), ()))
    trow = lax.broadcasted_iota(jnp.int32, (ch, ch), 0)
    tcol = lax.broadcasted_iota(jnp.int32, (ch, ch), 1)
    srow = lax.broadcasted_iota(jnp.int32, (ch, 1), 0)

    def cumsum_rows(g, reverse):
        acc = g
        sh = 1
        while sh < ch:
            if reverse:
                acc = acc + jnp.where(srow < ch - sh, pltpu.roll(acc, ch - sh, 0), 0.0)
            else:
                acc = acc + jnp.where(srow >= sh, pltpu.roll(acc, sh, 0), 0.0)
            sh *= 2
        return acc

    def group(starts, dirn):
        reverse = dirn == 1
        keep = (tcol >= trow) if reverse else (tcol <= trow)
        rows_l = [pl.ds(s if isinstance(s, int) else pl.multiple_of(s, ch), ch) for s in starts]
        items = [(rows, hh, slice(hh * dk, (hh + 1) * dk), slice(hh * dv, (hh + 1) * dv))
                 for rows in rows_l for hh in range(n_heads)]
        zs = [jnp.dot(lr_ref[0, rows, :].astype(BF16), gu_ref[dirn, hh], preferred_element_type=F32)
              for rows, hh, _, _ in items]
        ops = []
        for (rows, hh, kc, _), z in zip(items, zs):
            g = jax.nn.log_sigmoid(z + gb_ref[dirn, hh]) * (1.0 / GLA_TAU)
            bcum = cumsum_rows(g, reverse)
            b_end = bcum[0:1] if reverse else bcum[ch - 1:ch]
            rho = bcum[mid:mid + 1]
            up = jnp.exp(bcum - rho)
            dn = jnp.exp(rho - bcum)
            qf = q_ref[0, rows, kc].astype(F32) * qscale
            kf = k_ref[0, rows, kc].astype(F32)
            ops.append(dict(
                q_state=(qf * (up * jnp.exp(rho))).astype(BF16),
                q_local=(qf * up).astype(BF16),
                k_local=(kf * dn).astype(BF16),
                k_state=(kf * (dn * jnp.exp(b_end - rho))).astype(BF16),
                decay=jnp.exp(b_end)))
        atts = [lax.dot_general(o["q_local"], o["k_local"], nt, preferred_element_type=F32) for o in ops]
        st_adds = [lax.dot_general(v_ref[0, rows, vc], o["k_state"], tn, preferred_element_type=F32)
                   for (rows, _, _, vc), o in zip(items, ops)]
        o_locals = [jnp.dot(jnp.where(keep, att, 0.0).astype(BF16), v_ref[0, rows, vc], preferred_element_type=F32)
                    for (rows, _, _, vc), att in zip(items, atts)]
        sts = [st_scr[hh] for hh in range(n_heads)]
        states = []
        for (_, hh, _, _), o, st_add in zip(items, ops, st_adds):
            states.append(sts[hh].astype(BF16))
            sts[hh] = sts[hh] * o["decay"] + st_add
        for hh in range(n_heads):
            st_scr[hh] = sts[hh]
        outs = [o_local + lax.dot_general(o["q_state"], s_in, nt, preferred_element_type=F32)
                for o, o_local, s_in in zip(ops, o_locals, states)]
        for (rows, hh, _, vc), o in zip(items, outs):
            if dirn == 0:
                of_scr[rows, vc] = o
            else:
                o = o + of_scr[rows, vc]
                o = o * lax.rsqrt(jnp.mean(o * o, axis=-1, keepdims=True) + EPS) * gn_ref[...]
                o_ref[0, rows, vc] = (o * _silu(g_ref[0, rows, vc].astype(F32))).astype(o_ref.dtype)

    n_cc = n_ctx // ch
    n_lg = n_lat // (ch * GLA_GROUP)
    gsz = ch * GLA_GROUP

    def latent_group(gi, dirn):
        if dirn == 0:
            group([gi * gsz + t * ch for t in range(GLA_GROUP)], 0)
        else:
            group([(n_lg - 1 - gi) * gsz + t * ch for t in range(GLA_GROUP - 1, -1, -1)], 1)

    st_scr[...] = jnp.zeros_like(st_scr)
    group([n_lat + c * ch for c in range(n_cc)], 0)
    lax.fori_loop(0, n_lg, lambda gi, carry: (latent_group(gi, 0), carry)[1], 0)
    st_scr[...] = jnp.zeros_like(st_scr)
    group([n_lat + c * ch for c in range(n_cc - 1, -1, -1)], 1)
    lax.fori_loop(0, n_lg, lambda gi, carry: (latent_group(gi, 1), carry)[1], 0)


def _gla(proj, lr, gu, gb, gnorm, *, n_lat, cols):
    b, r, _ = proj.shape
    nh = GLA_HEADS
    dk = LANES
    dv = 2 * LANES
    kcol, vcol, qcol, gcol = cols
    hs = GLA_HEADS_PER_STEP
    wk, wv = hs * dk, hs * dv
    return pl.pallas_call(
        functools.partial(_gla_kernel, n_lat=n_lat, n_ctx=r - n_lat, qscale=dk ** -0.5),
        grid=(b, nh // hs),
        in_specs=[
            pl.BlockSpec((1, r, wk), lambda bi, hi: (bi, 0, qcol // wk + hi)),
            pl.BlockSpec((1, r, wk), lambda bi, hi: (bi, 0, kcol // wk + hi)),
            pl.BlockSpec((1, r, wv), lambda bi, hi: (bi, 0, vcol // wv + hi)),
            pl.BlockSpec((1, r, wv), lambda bi, hi: (bi, 0, gcol // wv + hi)),
            pl.BlockSpec((1, r, LANES), lambda bi, hi: (bi, 0, 0)),
            pl.BlockSpec((2, hs, LANES, dk), lambda bi, hi: (0, hi, 0, 0)),
            pl.BlockSpec((2, hs, 1, dk), lambda bi, hi: (0, hi, 0, 0)),
            pl.BlockSpec((1, dv), lambda bi, hi: (0, 0)),
        ],
        out_specs=pl.BlockSpec((1, r, wv), lambda bi, hi: (bi, 0, hi)),
        out_shape=jax.ShapeDtypeStruct((b, r, nh * dv), BF16),
        scratch_shapes=[pltpu.VMEM((r, wv), F32), pltpu.VMEM((hs, dv, dk), F32)],
        compiler_params=_cparams(("arbitrary", "arbitrary")),
        name="gla",
    )(proj, proj, proj, proj, lr, gu, gb, gnorm.reshape(1, dv))


def _rope_tables(n_lat, n_ctx):
    half = DA_QK // 2
    inv = 1.0 / (ROPE_BASE ** (jnp.arange(0, half, 2, dtype=F32) / half))
    t = jnp.arange(n_lat)
    row = (t // GRID_W).astype(F32)[:, None] * inv
    col = (t % GRID_W).astype(F32)[:, None] * inv
    ang = jnp.concatenate([row, row, col, col], axis=-1)
    lane = np.arange(LANES)
    dim = 32 * ((lane >> 4) & 1) + 16 * (lane >> 6) + (lane & 15)
    sign = jnp.asarray(np.where((lane >> 6) == 0, -1.0, 1.0), F32)
    cos = jnp.concatenate([jnp.cos(ang)[:, dim], jnp.ones((n_ctx, LANES), F32)], axis=0)
    sin = jnp.concatenate([jnp.sin(ang)[:, dim] * sign, jnp.zeros((n_ctx, LANES), F32)], axis=0)
    qf = math.log2(math.e) * DA_QK ** -0.5
    return jnp.stack([cos, cos * qf]), jnp.stack([sin, sin * qf])


def _pair_layout(w):
    d = w.shape[0]
    w = w.reshape(d, -1, 2, 2, 2, DA_QK // 4)
    return jnp.transpose(w, (0, 1, 4, 2, 3, 5)).reshape(d, -1)


def kernel(x, c, ctx, c_ctx, w_mod, b_mod, g_pre, g_post, w_out, ev_w_in, ev_lambda, ev_subln, ev_conv,
           od_w_in, od_rpb, od_gate_up, od_gate_bias, od_gnorm):
    b, s, d = x.shape
    n_ctx = ctx.shape[1]
    depth = w_mod.shape[0]
    bw = EXPAND * d // 2
    assert bw == DA_HEADS * LANES == NA_HEADS * NA_DIM and (s + n_ctx) % (ROW_SPLIT * 16) == 0
    assert s % ATT_TQ == 0 and s % n_ctx == 0 and s % GRID_W == 0 and n_ctx % GLA_CHUNK == 0
    tm = (s + n_ctx) // ROW_SPLIT
    assert tm % ROPE_ROWS == 0 and (ROW_SPLIT - 1) * tm <= s and s % (ROW_SPLIT * 16) == 0

    xall = jnp.concatenate([x, ctx], axis=1)
    nrow = -(-(b + 1) // 8) * 8
    cc = jnp.zeros((nrow, d), F32).at[:b].set(c).at[b].set(c_ctx)
    mod = _modulation(cc, w_mod, b_mod)
    cos_t, sin_t = _rope_tables(s, n_ctx)

    gk = bw // 2
    od_lr0 = 3 * bw + gk
    od_cols = np.cumsum([0, bw, bw, gk, bw, bw, gk, bw])
    na_cols = (int(od_cols[0]), int(od_cols[1]), int(od_cols[4]), int(od_cols[6]))
    gla_cols = (int(od_cols[2]), int(od_cols[3]), int(od_cols[5]), int(od_cols[7]))

    for l in range(depth):
        j = l // 2
        modb = mod[l, :b].reshape(b, 1, 3 * d)
        modc = mod[l, b:b + 1]
        if l % 2 == 0:
            lam_init = 0.8 - 0.6 * math.exp(-0.3 * l)
            w = ev_w_in[j]
            w = jnp.concatenate([_pair_layout(w[:, :bw]), w[:, bw:2 * bw], _pair_layout(w[:, 2 * bw:3 * bw]),
                                 w[:, 3 * bw:]], axis=1).astype(BF16)
            proj, _ = _inproj(xall, modb, modc, g_pre[l], w, (cos_t, sin_t, 0, 2), None, n_lat=s, tn=bw)
            vt = proj[:, :, bw:2 * bw].reshape(b, s + n_ctx, DA_HEADS, LANES).transpose(0, 2, 3, 1)
            ua = _diff_attention(proj, vt, ev_lambda[j], ev_subln[j], n_lat=s, lam_init=lam_init)
            ub = _short_conv(proj, ev_conv[j], n_lat=s)
        else:
            w = od_w_in[j]
            w_lr = jnp.pad(w[:, od_lr0:od_lr0 + 2 * GLA_RANK], ((0, 0), (0, LANES - 2 * GLA_RANK))).astype(BF16)
            w = jnp.concatenate([w[:, :od_lr0], w[:, od_lr0 + 2 * GLA_RANK:]], axis=1).astype(BF16)
            proj, lr = _inproj(xall, modb, modc, g_pre[l], w, None, w_lr, n_lat=s, tn=bw)
            tb = _natten_bias_table(od_rpb[j])
            ua = _neighbourhood_attention(proj, tb, n_lat=s, cols=na_cols)
            gu = jnp.zeros((2, GLA_HEADS, LANES, LANES), F32)
            gup = od_gate_up[j].reshape(2, GLA_RANK, GLA_HEADS, LANES).transpose(0, 2, 1, 3)
            gu = gu.at[0, :, :GLA_RANK].set(gup[0]).at[1, :, GLA_RANK:2 * GLA_RANK].set(gup[1]).astype(BF16)
            gb = od_gate_bias[j].reshape(2, GLA_HEADS, 1, LANES)
            ub = _gla(proj, lr, gu, gb, od_gnorm[j], n_lat=s, cols=gla_cols)
        xall = _outproj(ua, ub, w_out[l].astype(BF16), xall, modb, modc, g_post[l], n_lat=s,
                        latent_only=l == depth - 1)
    return xall
```

```python
import functools
import math

import numpy as np
import jax
import jax.numpy as jnp
from jax import lax
from jax.experimental import pallas as pl
from jax.experimental.pallas import tpu as pltpu

F32 = jnp.float32
BF16 = jnp.bfloat16

LANES = 128
GRID_W = 64
EXPAND = 2
DA_HEADS = 8
DA_QK = 64
SC_WIDTH = 3
NA_HEADS = 16
NA_DIM = 64
WIN_ROWS = 8
WIN_COLS = 16
GLA_HEADS = 4
GLA_RANK = 16
GLA_TAU = 16.0
GLA_CHUNK = 64
ROPE_BASE = 10000.0
EPS = 1e-6
ROW_SPLIT = 4
ROPE_ROWS = 64
ATT_TQ = 512
ATT_KCHUNK = 256
NA_GROUP = 8
GLA_GROUP = 4
GLA_HEADS_PER_STEP = 2
VMEM_LIMIT = 56 * 1024 * 1024


def _cparams(sem):
    return pltpu.CompilerParams(dimension_semantics=sem, vmem_limit_bytes=VMEM_LIMIT)


def _silu(t):
    return t * jax.nn.sigmoid(t)


def _mod_kernel(c_ref, w_ref, b_ref, o_ref):
    s = _silu(c_ref[...])
    o_ref[0] = jnp.dot(s, w_ref[0], preferred_element_type=F32) + b_ref[0]


def _modulation(cc, w_mod, b_mod):
    depth, d, d3 = w_mod.shape
    nrow = cc.shape[0]
    tn = d
    return pl.pallas_call(
        _mod_kernel,
        grid=(depth, d3 // tn),
        in_specs=[
            pl.BlockSpec((nrow, d), lambda l, j: (0, 0)),
            pl.BlockSpec((1, d, tn), lambda l, j: (l, 0, j)),
            pl.BlockSpec((1, 1, tn), lambda l, j: (l, 0, j)),
        ],
        out_specs=pl.BlockSpec((1, nrow, tn), lambda l, j: (l, 0, j)),
        out_shape=jax.ShapeDtypeStruct((depth, nrow, d3), F32),
        compiler_params=_cparams(("arbitrary", "arbitrary")),
        name="modulation",
    )(cc, w_mod, b_mod.reshape(depth, 1, d3))


def _inproj_kernel(*refs, tm, d, n_lat, rope_tiles, has_lr):
    x_ref, mb_ref, mc_ref, g_ref, w_ref = refs[:5]
    rest = list(refs[5:])
    cos_ref, sin_ref = (rest.pop(0), rest.pop(0)) if rope_tiles else (None, None)
    wlr_ref = rest.pop(0) if has_lr else None
    o_ref = rest.pop(0)
    lr_ref = rest.pop(0) if has_lr else None
    h_scr = rest.pop(0)
    acc_scr = rest.pop(0) if rope_tiles else None
    k = pl.program_id(1)
    j = pl.program_id(2)
    n_blk = pl.num_programs(1)

    @pl.when(j == 0)
    def _():
        xv = x_ref[0]
        xn = xv * lax.rsqrt(jnp.mean(xv * xv, axis=-1, keepdims=True) + EPS)
        mb = mb_ref[0]
        gain_b = g_ref[...] * (1.0 + mb[:, d:2 * d])
        shift_b = mb[:, 0:d]

        def emit(gain, shift):
            h_scr[...] = (xn * gain + shift).astype(BF16)
            if has_lr:
                lr_ref[0] = jnp.dot(h_scr[...], wlr_ref[...], preferred_element_type=F32)

        pl.when(k < n_blk - 1)(lambda: emit(gain_b, shift_b))

        @pl.when(k == n_blk - 1)
        def _():
            is_ctx = k * tm + lax.broadcasted_iota(jnp.int32, (tm, 1), 0) >= n_lat
            gain_c = g_ref[...] * (1.0 + mc_ref[:, d:2 * d])
            emit(jnp.where(is_ctx, gain_c, gain_b), jnp.where(is_ctx, mc_ref[:, 0:d], shift_b))

    tn = o_ref.shape[2]

    def project():
        w_tile = w_ref[:, pl.ds(pl.multiple_of(j * tn, tn), tn)]
        return jnp.dot(h_scr[...], w_tile, preferred_element_type=F32)

    if rope_tiles:
        is_rope = functools.reduce(jnp.logical_or, [j == t for t in rope_tiles])

        @pl.when(is_rope)
        def _():
            acc_scr[...] = project()

            def rows_body(c, carry):
                rows = pl.ds(pl.multiple_of(c * ROPE_ROWS, ROPE_ROWS), ROPE_ROWS)
                cos = cos_ref[0, rows, :]
                sin = sin_ref[0, rows, :]
                for hd in range(tn // LANES):
                    a = acc_scr[rows, hd * LANES:(hd + 1) * LANES]
                    o_ref[0, rows, hd * LANES:(hd + 1) * LANES] = (
                        a * cos + pltpu.roll(a, LANES // 2, 1) * sin).astype(o_ref.dtype)
                return carry

            lax.fori_loop(0, tm // ROPE_ROWS, rows_body, 0, unroll=True)

        @pl.when(jnp.logical_not(is_rope))
        def _():
            o_ref[0] = project().astype(o_ref.dtype)
    else:
        o_ref[0] = project().astype(o_ref.dtype)


def _inproj(xall, modb, modc, g, w, rope, w_lr, *, n_lat, tn):
    b, r, d = xall.shape
    nc = w.shape[1]
    tm = r // ROW_SPLIT
    has_lr = w_lr is not None
    in_specs = [
        pl.BlockSpec((1, tm, d), lambda bi, k, j: (bi, k, 0)),
        pl.BlockSpec((1, 1, 3 * d), lambda bi, k, j: (bi, 0, 0)),
        pl.BlockSpec((1, 3 * d), lambda bi, k, j: (0, 0)),
        pl.BlockSpec((1, d), lambda bi, k, j: (0, 0)),
        pl.BlockSpec((d, nc), lambda bi, k, j: (0, 0), pipeline_mode=pl.Buffered(1)),
    ]
    args = [xall, modb, modc, g.reshape(1, d), w]
    rope_tiles = ()
    if rope is not None:
        cos_t, sin_t, k_tile, q_tile = rope
        rope_tiles = (k_tile, q_tile)
        table = pl.BlockSpec((1, tm, LANES), lambda bi, k, j: (jnp.where(j == q_tile, 1, 0), k, 0))
        in_specs += [table, table]
        args += [cos_t, sin_t]
    out_specs = [pl.BlockSpec((1, tm, tn), lambda bi, k, j: (bi, k, j))]
    out_shape = [jax.ShapeDtypeStruct((b, r, nc), BF16)]
    if has_lr:
        in_specs.append(pl.BlockSpec((d, LANES), lambda bi, k, j: (0, 0)))
        args.append(w_lr)
        out_specs.append(pl.BlockSpec((1, tm, LANES), lambda bi, k, j: (bi, k, 0)))
        out_shape.append(jax.ShapeDtypeStruct((b, r, LANES), F32))
    res = pl.pallas_call(
        functools.partial(_inproj_kernel, tm=tm, d=d, n_lat=n_lat, rope_tiles=rope_tiles, has_lr=has_lr),
        grid=(b, ROW_SPLIT, nc // tn),
        in_specs=in_specs,
        out_specs=out_specs,
        out_shape=out_shape,
        scratch_shapes=[pltpu.VMEM((tm, d), BF16)] + ([pltpu.VMEM((tm, tn), F32)] if rope_tiles else []),
        compiler_params=_cparams(("arbitrary", "arbitrary", "arbitrary")),
        name="inproj",
    )(*args)
    return res if has_lr else (res[0], None)


def _outproj_kernel(ua_ref, ub_ref, w1_ref, w2_ref, x_ref, mb_ref, mc_ref, g_ref, o_ref, *, tm, d, n_lat):
    k = pl.program_id(1)
    y = jnp.dot(ua_ref[0], w1_ref[...], preferred_element_type=F32)
    y = y + jnp.dot(ub_ref[0], w2_ref[...], preferred_element_type=F32)
    y = y * lax.rsqrt(jnp.mean(y * y, axis=-1, keepdims=True) + EPS) * g_ref[...]
    row = k * tm + lax.broadcasted_iota(jnp.int32, (tm, 1), 0)
    gate = jnp.where(row >= n_lat, mc_ref[:, 2 * d:3 * d], mb_ref[0][:, 2 * d:3 * d])
    o_ref[0] = x_ref[0] + gate * y


def _outproj(ua, ub, w, xall, modb, modc, g, *, n_lat, latent_only):
    b, r, d = xall.shape
    bw = ua.shape[2]
    tm = n_lat // ROW_SPLIT if latent_only else r // ROW_SPLIT
    return pl.pallas_call(
        functools.partial(_outproj_kernel, tm=tm, d=d, n_lat=n_lat),
        grid=(b, ROW_SPLIT),
        in_specs=[
            pl.BlockSpec((1, tm, bw), lambda bi, k: (bi, k, 0)),
            pl.BlockSpec((1, tm, bw), lambda bi, k: (bi, k, 0)),
            pl.BlockSpec((bw, d), lambda bi, k: (0, 0)),
            pl.BlockSpec((bw, d), lambda bi, k: (1, 0)),
            pl.BlockSpec((1, tm, d), lambda bi, k: (bi, k, 0)),
            pl.BlockSpec((1, 1, 3 * d), lambda bi, k: (bi, 0, 0)),
            pl.BlockSpec((1, 3 * d), lambda bi, k: (0, 0)),
            pl.BlockSpec((1, d), lambda bi, k: (0, 0)),
        ],
        out_specs=pl.BlockSpec((1, tm, d), lambda bi, k: (bi, k, 0)),
        out_shape=jax.ShapeDtypeStruct((b, n_lat if latent_only else r, d), F32),
        input_output_aliases={} if latent_only else {4: 0},
        compiler_params=_cparams(("arbitrary", "arbitrary")),
        name="outproj",
    )(ua, ub, w, w, xall, modb, modc, g.reshape(1, d))


def _dattn_lambda(lam_ref, lam_init):
    lp = lam_ref[...]
    return (jnp.exp(jnp.sum(lp[0:1] * lp[1:2], axis=-1, keepdims=True))
            - jnp.exp(jnp.sum(lp[2:3] * lp[3:4], axis=-1, keepdims=True)) + lam_init)


def _dattn_split_maps(q):
    first = (lax.broadcasted_iota(jnp.int32, (1, LANES), 1) & (DA_QK // 2)) == 0
    zero = jnp.zeros_like(q)
    return jnp.concatenate([jnp.where(first, q, zero), jnp.where(first, zero, q)], axis=0)


def _dattn_finish(o, sub_ref, g_ref, o_ref, lam_init):
    o = o * lax.rsqrt(jnp.mean(o * o, axis=-1, keepdims=True) + EPS) * sub_ref[...] * (1.0 - lam_init)
    o_ref[0] = (o * _silu(g_ref[0].astype(F32))).astype(o_ref.dtype)


def _dattn_kernel(lam_ref, sub_ref, qn_ref, k_ref, vt_ref, g_ref, o_ref, sa_ref, ma_ref, sb_ref, mb_ref,
                  acc_ref, den_ref, *, n_blocks, lam_init):
    t = pl.program_id(0)
    lam = _dattn_lambda(lam_ref, lam_init)
    nt = (((1,), (1,)), ((), ()))
    tq = qn_ref.shape[1]

    def step(cur, prev, fin):
        if fin:
            o = acc_ref[...] * (1.0 / den_ref[...])
            _dattn_finish((o[:, :tq] - lam * o[:, tq:]).T, sub_ref, g_ref, o_ref, lam_init)
        r = k_ref.shape[1]
        bounds = list(range(0, r - ATT_KCHUNK + 1, ATT_KCHUNK)) + [r]
        if cur is not None:
            q2 = _dattn_split_maps(qn_ref[0])
        m_cur = l_prev = o_prev = None
        for lo, hi in zip(bounds[:-1], bounds[1:]):
            if cur is not None:
                s = lax.dot_general(k_ref[0, lo:hi, :], q2, nt, preferred_element_type=F32)
                cur[0][lo:hi, :] = s
                m = jnp.max(s, axis=0, keepdims=True)
                m_cur = m if m_cur is None else jnp.maximum(m_cur, m)
            if prev is not None:
                e = jnp.exp2(prev[0][lo:hi, :] - prev[1][...])
                l = jnp.sum(e, axis=0, keepdims=True)
                o = jnp.dot(vt_ref[0, 0, :, lo:hi], e.astype(BF16), preferred_element_type=F32)
                l_prev = l if l_prev is None else l_prev + l
                o_prev = o if o_prev is None else o_prev + o
        if cur is not None:
            cur[1][...] = m_cur
        if prev is not None:
            acc_ref[...] = o_prev
            den_ref[...] = l_prev

    bufs = ((sa_ref, ma_ref), (sb_ref, mb_ref))
    last = bufs[(n_blocks - 1) % 2]
    pl.when(t == 0)(lambda: step(bufs[0], None, False))
    pl.when(t == 1)(lambda: step(bufs[1], bufs[0], False))
    pl.when((t >= 2) & (t < n_blocks) & (t % 2 == 0))(lambda: step(bufs[0], bufs[1], True))
    pl.when((t >= 2) & (t < n_blocks) & (t % 2 == 1))(lambda: step(bufs[1], bufs[0], True))
    pl.when(t == n_blocks)(lambda: step(None, last, True))
    pl.when(t == n_blocks + 1)(lambda: step(None, None, True))


def _dattn_ctx_kernel(lam_ref, sub_ref, q_ref, k_ref, v_ref, g_ref, ua_ref, o_ref, *, lam_init):
    del ua_ref
    lam = _dattn_lambda(lam_ref, lam_init)
    nt = (((1,), (1,)), ((), ()))
    n = q_ref.shape[1]
    s = lax.dot_general(_dattn_split_maps(q_ref[0]), k_ref[0], nt, preferred_element_type=F32)
    e = jnp.exp2(s - jnp.max(s, axis=-1, keepdims=True))
    p = e / jnp.sum(e, axis=-1, keepdims=True)
    a = (p[:n] - lam * p[n:]).astype(BF16)
    _dattn_finish(jnp.dot(a, v_ref[0], preferred_element_type=F32), sub_ref, g_ref, o_ref, lam_init)


def _diff_attention(proj, vt, lam_p, subln, *, n_lat, lam_init):
    b, r, _ = proj.shape
    h = DA_HEADS
    tq = ATT_TQ
    nq = n_lat // tq
    n_ctx = r - n_lat
    sub = subln.reshape(1, LANES)
    small = [pl.BlockSpec(lam_p.shape, lambda *_: (0, 0)), pl.BlockSpec((1, LANES), lambda *_: (0, 0))]
    n_blocks = b * h * nq

    def unit(t, lag):
        u = jnp.clip(t - lag, 0, n_blocks - 1)
        return u // (h * nq), (u // nq) % h, u % nq

    def q_map(t):
        bi, hi, i = unit(t, 0)
        return bi, i, 2 * h + hi

    def k_map(t):
        bi, hi, _ = unit(t, 0)
        return bi, 0, hi

    def vt_map(t):
        bi, hi, _ = unit(t, 1)
        return bi, hi, 0, 0

    def g_map(t):
        bi, hi, i = unit(t, 2)
        return bi, i, 3 * h + hi

    def o_map(t):
        bi, hi, i = unit(t, 2)
        return bi, i, hi

    ua = pl.pallas_call(
        functools.partial(_dattn_kernel, n_blocks=n_blocks, lam_init=lam_init),
        grid=(n_blocks + 2,),
        in_specs=small + [
            pl.BlockSpec((1, tq, LANES), q_map),
            pl.BlockSpec((1, r, LANES), k_map),
            pl.BlockSpec((1, 1, LANES, r), vt_map),
            pl.BlockSpec((1, tq, LANES), g_map),
        ],
        out_specs=pl.BlockSpec((1, tq, LANES), o_map),
        out_shape=jax.ShapeDtypeStruct((b, r, h * LANES), BF16),
        scratch_shapes=[pltpu.VMEM((r, 2 * tq), F32), pltpu.VMEM((1, 2 * tq), F32)] * 2
        + [pltpu.VMEM((LANES, 2 * tq), F32), pltpu.VMEM((1, 2 * tq), F32)],
        compiler_params=_cparams(("arbitrary",)),
        name="diff_attention",
    )(lam_p, sub, proj, proj, vt, proj)
    cb = n_lat // n_ctx
    ctx_rows = lambda off: pl.BlockSpec((1, n_ctx, LANES), lambda bi, hi: (bi, cb, off + hi))
    return pl.pallas_call(
        functools.partial(_dattn_ctx_kernel, lam_init=lam_init),
        grid=(b, h),
        in_specs=small + [ctx_rows(2 * h), ctx_rows(0), ctx_rows(h), ctx_rows(3 * h),
                          pl.BlockSpec(memory_space=pl.ANY)],
        out_specs=ctx_rows(0),
        out_shape=jax.ShapeDtypeStruct(ua.shape, ua.dtype),
        input_output_aliases={6: 0},
        compiler_params=_cparams(("arbitrary", "arbitrary")),
        name="diff_attention_ctx",
    )(lam_p, sub, proj, proj, proj, proj, ua)


def _sconv_kernel(h_ref, b_ref, c_ref, g_ref, w_ref, o_ref, *, n_lat):
    r = h_ref.shape[1]
    p = c_ref[0].astype(F32) * h_ref[0].astype(F32)
    row = lax.broadcasted_iota(jnp.int32, (r, 1), 0)
    prev = jnp.where((row == 0) | (row == n_lat), 0.0, pltpu.roll(p, 1, 0))
    nxt = jnp.where((row == n_lat - 1) | (row == r - 1), 0.0, pltpu.roll(p, r - 1, 0))
    w = w_ref[...]
    conv = w[0:1] * prev + w[1:2] * p + w[2:3] * nxt
    o_ref[0] = (b_ref[0].astype(F32) * conv * _silu(g_ref[0].astype(F32))).astype(o_ref.dtype)


def _short_conv(proj, conv_w, *, n_lat):
    b, r, _ = proj.shape
    nb = conv_w.shape[1] // LANES
    base = 4 * nb
    spec = lambda off: pl.BlockSpec((1, r, LANES), lambda bi, j: (bi, 0, off + j))
    return pl.pallas_call(
        functools.partial(_sconv_kernel, n_lat=n_lat),
        grid=(b, nb),
        in_specs=[spec(base), spec(base + nb), spec(base + 2 * nb), spec(base + 3 * nb),
                  pl.BlockSpec((SC_WIDTH, LANES), lambda bi, j: (0, j))],
        out_specs=pl.BlockSpec((1, r, LANES), lambda bi, j: (bi, 0, j)),
        out_shape=jax.ShapeDtypeStruct((b, r, nb * LANES), BF16),
        compiler_params=_cparams(("arbitrary", "arbitrary")),
        name="short_conv",
    )(proj, proj, proj, proj, conv_w)


def _natten_kernel(q_ref, k_ref, v_ref, g_ref, tb_ref, o_ref, *, n_lat, n_rows):
    lane = lax.broadcasted_iota(jnp.int32, (1, LANES), 1)
    first = lane < NA_DIM
    nt = (((1,), (1,)), ((), ()))
    win = WIN_ROWS * GRID_W
    kc = k_ref[0, n_lat:, :]
    vc = v_ref[0, n_lat:, :]

    def stack_heads(q):
        q = q * (NA_DIM ** -0.5)
        zero = jnp.zeros_like(q)
        return jnp.concatenate([jnp.where(first, q, zero), jnp.where(first, zero, q)], axis=0)

    def softmax(parts):
        m = functools.reduce(jnp.maximum, [jnp.max(s, axis=-1, keepdims=True) for s in parts])
        es = [jnp.exp(s - m) for s in parts]
        l = functools.reduce(jnp.add, [jnp.sum(e, axis=-1, keepdims=True) for e in es])
        return [e.astype(BF16) for e in es], 1.0 / l

    def unstack_heads(o2):
        n = o2.shape[0] // 2
        return jnp.where(first, o2[:n], o2[n:])

    def group_body(gi, carry):
        rows = [gi * NA_GROUP + t for t in range(NA_GROUP)]
        starts = [jnp.clip(rr - WIN_ROWS // 2, 0, n_rows - WIN_ROWS) for rr in rows]
        qsl = [pl.ds(pl.multiple_of(rr * GRID_W, GRID_W), GRID_W) for rr in rows]
        ksl = [pl.ds(pl.multiple_of(rs * GRID_W, GRID_W), win) for rs in starts]
        scores = {}
        weights = {}
        for i in range(NA_GROUP + 2):
            if i < NA_GROUP:
                q2 = stack_heads(q_ref[0, qsl[i], :])
                s_loc = lax.dot_general(q2, k_ref[0, ksl[i], :], nt, preferred_element_type=F32)
                off = starts[i] - rows[i] + (WIN_ROWS - 1)
                s_loc = s_loc + jnp.concatenate([tb_ref[0, off + 2 * p] for p in range(WIN_ROWS // 2)], axis=1)
                scores[i] = [s_loc, lax.dot_general(q2, kc, nt, preferred_element_type=F32)]
            if 0 <= i - 1 < NA_GROUP:
                weights[i - 1] = softmax(scores.pop(i - 1))
            if 0 <= i - 2 < NA_GROUP:
                es, rl = weights.pop(i - 2)
                qs, ks = qsl[i - 2], ksl[i - 2]
                o2 = jnp.dot(es[0], v_ref[0, ks, :], preferred_element_type=F32)
                o2 = o2 + jnp.dot(es[1], vc, preferred_element_type=F32)
                out = unstack_heads(o2 * rl)
                o_ref[0, qs, :] = (out * _silu(g_ref[0, qs, :].astype(F32))).astype(o_ref.dtype)
        return carry

    lax.fori_loop(0, n_rows // NA_GROUP, group_body, 0)

    s = lax.dot_general(stack_heads(q_ref[0, n_lat:, :]), kc, nt, preferred_element_type=F32)
    es, rl = softmax([s])
    outc = unstack_heads(jnp.dot(es[0], vc, preferred_element_type=F32) * rl)
    o_ref[0, n_lat:, :] = (outc * _silu(g_ref[0, n_lat:, :].astype(F32))).astype(o_ref.dtype)


def _neighbourhood_attention(proj, tb, *, n_lat, cols):
    b, r, _ = proj.shape
    nb = NA_HEADS // 2
    kb, vb, qb, gb = (c // LANES for c in cols)
    spec = lambda off: pl.BlockSpec((1, r, LANES), lambda bi, j: (bi, 0, off + j))
    return pl.pallas_call(
        functools.partial(_natten_kernel, n_lat=n_lat, n_rows=n_lat // GRID_W),
        grid=(b, nb),
        in_specs=[spec(qb), spec(kb), spec(vb), spec(gb),
                  pl.BlockSpec((1,) + tb.shape[1:], lambda bi, j: (j, 0, 0, 0))],
        out_specs=pl.BlockSpec((1, r, LANES), lambda bi, j: (bi, 0, j)),
        out_shape=jax.ShapeDtypeStruct((b, r, nb * LANES), BF16),
        compiler_params=_cparams(("arbitrary", "arbitrary")),
        name="neighbourhood_attention",
    )(proj, proj, proj, proj, tb)


def _natten_bias_table(rpb):
    qcol = np.arange(GRID_W)[:, None]
    kcol = np.arange(GRID_W)[None, :]
    win_start = np.clip(qcol - WIN_COLS // 2, 0, GRID_W - WIN_COLS)
    valid = (kcol >= win_start) & (kcol < win_start + WIN_COLS)
    pad = GRID_W - WIN_COLS
    padded = jnp.pad(rpb.astype(F32), ((0, 0), (0, 0), (pad, pad)))
    toe = jnp.stack([padded[:, :, GRID_W - 1 - q:2 * GRID_W - 1 - q] for q in range(GRID_W)], axis=2)
    toe = jnp.where(jnp.asarray(valid)[None, None], toe, -jnp.inf)
    t = jnp.concatenate([toe[:, :-1], toe[:, 1:]], axis=-1)
    nh, n_o = t.shape[:2]
    t = jnp.transpose(t.reshape(nh // 2, 2, n_o, GRID_W, 2 * GRID_W), (0, 2, 1, 3, 4))
    return t.reshape(nh // 2, n_o, 2 * GRID_W, 2 * GRID_W)


def _gla_kernel(q_ref, k_ref, v_ref, g_ref, lr_ref, gu_ref, gb_ref, gn_ref, o_ref, of_scr, st_scr,
                *, n_lat, n_ctx, qscale):
    ch = GLA_CHUNK
    mid = ch // 2
    n_heads, dv, dk = st_scr.shape
    nt = (((1,), (1,)), ((), ()))
    tn = (((0,), (0,)), ((), ()))
    trow = lax.broadcasted_iota(jnp.int32, (ch, ch), 0)
    tcol = lax.broadcasted_iota(jnp.int32, (ch, ch), 1)
    srow = lax.broadcasted_iota(jnp.int32, (ch, 1), 0)

    def cumsum_rows(g, reverse):
        acc = g
        sh = 1
        while sh < ch:
            if reverse:
                acc = acc + jnp.where(srow < ch - sh, pltpu.roll(acc, ch - sh, 0), 0.0)
            else:
                acc = acc + jnp.where(srow >= sh, pltpu.roll(acc, sh, 0), 0.0)
            sh *= 2
        return acc

    def group(starts, dirn):
        reverse = dirn == 1
        keep = (tcol >= trow) if reverse else (tcol <= trow)
        rows_l = [pl.ds(s if isinstance(s, int) else pl.multiple_of(s, ch), ch) for s in starts]
        items = [(rows, hh, slice(hh * dk, (hh + 1) * dk), slice(hh * dv, (hh + 1) * dv))
                 for rows in rows_l for hh in range(n_heads)]
        zs = [jnp.dot(lr_ref[0, rows, :].astype(BF16), gu_ref[dirn, hh], preferred_element_type=F32)
              for rows, hh, _, _ in items]
        ops = []
        for (rows, hh, kc, _), z in zip(items, zs):
            g = jax.nn.log_sigmoid(z + gb_ref[dirn, hh]) * (1.0 / GLA_TAU)
            bcum = cumsum_rows(g, reverse)
            b_end = bcum[0:1] if reverse else bcum[ch - 1:ch]
            rho = bcum[mid:mid + 1]
            up = jnp.exp(bcum - rho)
            dn = jnp.exp(rho - bcum)
            qf = q_ref[0, rows, kc].astype(F32) * qscale
            kf = k_ref[0, rows, kc].astype(F32)
            ops.append(dict(
                q_state=(qf * (up * jnp.exp(rho))).astype(BF16),
                q_local=(qf * up).astype(BF16),
                k_local=(kf * dn).astype(BF16),
                k_state=(kf * (dn * jnp.exp(b_end - rho))).astype(BF16),
                decay=jnp.exp(b_end)))
        atts = [lax.dot_general(o["q_local"], o["k_local"], nt, preferred_element_type=F32) for o in ops]
        st_adds = [lax.dot_general(v_ref[0, rows, vc], o["k_state"], tn, preferred_element_type=F32)
                   for (rows, _, _, vc), o in zip(items, ops)]
        o_locals = [jnp.dot(jnp.where(keep, att, 0.0).astype(BF16), v_ref[0, rows, vc], preferred_element_type=F32)
                    for (rows, _, _, vc), att in zip(items, atts)]
        sts = [st_scr[hh] for hh in range(n_heads)]
        states = []
        for (_, hh, _, _), o, st_add in zip(items, ops, st_adds):
            states.append(sts[hh].astype(BF16))
            sts[hh] = sts[hh] * o["decay"] + st_add
        for hh in range(n_heads):
            st_scr[hh] = sts[hh]
        outs = [o_local + lax.dot_general(o["q_state"], s_in, nt, preferred_element_type=F32)
                for o, o_local, s_in in zip(ops, o_locals, states)]
        for (rows, hh, _, vc), o in zip(items, outs):
            if dirn == 0:
                of_scr[rows, vc] = o
            else:
                o = o + of_scr[rows, vc]
                o = o * lax.rsqrt(jnp.mean(o * o, axis=-1, keepdims=True) + EPS) * gn_ref[...]
                o_ref[0, rows, vc] = (o * _silu(g_ref[0, rows, vc].astype(F32))).astype(o_ref.dtype)

    n_cc = n_ctx // ch
    n_lg = n_lat // (ch * GLA_GROUP)
    gsz = ch * GLA_GROUP

    def latent_group(gi, dirn):
        if dirn == 0:
            group([gi * gsz + t * ch for t in range(GLA_GROUP)], 0)
        else:
            group([(n_lg - 1 - gi) * gsz + t * ch for t in range(GLA_GROUP - 1, -1, -1)], 1)

    st_scr[...] = jnp.zeros_like(st_scr)
    group([n_lat + c * ch for c in range(n_cc)], 0)
    lax.fori_loop(0, n_lg, lambda gi, carry: (latent_group(gi, 0), carry)[1], 0)
    st_scr[...] = jnp.zeros_like(st_scr)
    group([n_lat + c * ch for c in range(n_cc - 1, -1, -1)], 1)
    lax.fori_loop(0, n_lg, lambda gi, carry: (latent_group(gi, 1), carry)[1], 0)


def _gla(proj, lr, gu, gb, gnorm, *, n_lat, cols):
    b, r, _ = proj.shape
    nh = GLA_HEADS
    dk = LANES
    dv = 2 * LANES
    kcol, vcol, qcol, gcol = cols
    hs = GLA_HEADS_PER_STEP
    wk, wv = hs * dk, hs * dv
    return pl.pallas_call(
        functools.partial(_gla_kernel, n_lat=n_lat, n_ctx=r - n_lat, qscale=dk ** -0.5),
        grid=(b, nh // hs),
        in_specs=[
            pl.BlockSpec((1, r, wk), lambda bi, hi: (bi, 0, qcol // wk + hi)),
            pl.BlockSpec((1, r, wk), lambda bi, hi: (bi, 0, kcol // wk + hi)),
            pl.BlockSpec((1, r, wv), lambda bi, hi: (bi, 0, vcol // wv + hi)),
            pl.BlockSpec((1, r, wv), lambda bi, hi: (bi, 0, gcol // wv + hi)),
            pl.BlockSpec((1, r, LANES), lambda bi, hi: (bi, 0, 0)),
            pl.BlockSpec((2, hs, LANES, dk), lambda bi, hi: (0, hi, 0, 0)),
            pl.BlockSpec((2, hs, 1, dk), lambda bi, hi: (0, hi, 0, 0)),
            pl.BlockSpec((1, dv), lambda bi, hi: (0, 0)),
        ],
        out_specs=pl.BlockSpec((1, r, wv), lambda bi, hi: (bi, 0, hi)),
        out_shape=jax.ShapeDtypeStruct((b, r, nh * dv), BF16),
        scratch_shapes=[pltpu.VMEM((r, wv), F32), pltpu.VMEM((hs, dv, dk), F32)],
        compiler_params=_cparams(("arbitrary", "arbitrary")),
        name="gla",
    )(proj, proj, proj, proj, lr, gu, gb, gnorm.reshape(1, dv))


def _rope_tables(n_lat, n_ctx):
    half = DA_QK // 2
    inv = 1.0 / (ROPE_BASE ** (jnp.arange(0, half, 2, dtype=F32) / half))
    t = jnp.arange(n_lat)
    row = (t // GRID_W).astype(F32)[:, None] * inv
    col = (t % GRID_W).astype(F32)[:, None] * inv
    ang = jnp.concatenate([row, row, col, col], axis=-1)
    lane = np.arange(LANES)
    dim = 32 * ((lane >> 4) & 1) + 16 * (lane >> 6) + (lane & 15)
    sign = jnp.asarray(np.where((lane >> 6) == 0, -1.0, 1.0), F32)
    cos = jnp.concatenate([jnp.cos(ang)[:, dim], jnp.ones((n_ctx, LANES), F32)], axis=0)
    sin = jnp.concatenate([jnp.sin(ang)[:, dim] * sign, jnp.zeros((n_ctx, LANES), F32)], axis=0)
    qf = math.log2(math.e) * DA_QK ** -0.5
    return jnp.stack([cos, cos * qf]), jnp.stack([sin, sin * qf])


def _pair_layout(w):
    d = w.shape[0]
    w = w.reshape(d, -1, 2, 2, 2, DA_QK // 4)
    return jnp.transpose(w, (0, 1, 4, 2, 3, 5)).reshape(d, -1)


def kernel(x, c, ctx, c_ctx, w_mod, b_mod, g_pre, g_post, w_out, ev_w_in, ev_lambda, ev_subln, ev_conv,
           od_w_in, od_rpb, od_gate_up, od_gate_bias, od_gnorm):
    b, s, d = x.shape
    n_ctx = ctx.shape[1]
    depth = w_mod.shape[0]
    bw = EXPAND * d // 2
    assert bw == DA_HEADS * LANES == NA_HEADS * NA_DIM and (s + n_ctx) % (ROW_SPLIT * 16) == 0
    assert s % ATT_TQ == 0 and s % n_ctx == 0 and s % GRID_W == 0 and n_ctx % GLA_CHUNK == 0
    tm = (s + n_ctx) // ROW_SPLIT
    assert tm % ROPE_ROWS == 0 and (ROW_SPLIT - 1) * tm <= s and s % (ROW_SPLIT * 16) == 0

    xall = jnp.concatenate([x, ctx], axis=1)
    nrow = -(-(b + 1) // 8) * 8
    cc = jnp.zeros((nrow, d), F32).at[:b].set(c).at[b].set(c_ctx)
    mod = _modulation(cc, w_mod, b_mod)
    cos_t, sin_t = _rope_tables(s, n_ctx)

    gk = bw // 2
    od_lr0 = 3 * bw + gk
    od_cols = np.cumsum([0, bw, bw, gk, bw, bw, gk, bw])
    na_cols = (int(od_cols[0]), int(od_cols[1]), int(od_cols[4]), int(od_cols[6]))
    gla_cols = (int(od_cols[2]), int(od_cols[3]), int(od_cols[5]), int(od_cols[7]))

    for l in range(depth):
        j = l // 2
        modb = mod[l, :b].reshape(b, 1, 3 * d)
        modc = mod[l, b:b + 1]
        if l % 2 == 0:
            lam_init = 0.8 - 0.6 * math.exp(-0.3 * l)
            w = ev_w_in[j]
            w = jnp.concatenate([_pair_layout(w[:, :bw]), w[:, bw:2 * bw], _pair_layout(w[:, 2 * bw:3 * bw]),
                                 w[:, 3 * bw:]], axis=1).astype(BF16)
            proj, _ = _inproj(xall, modb, modc, g_pre[l], w, (cos_t, sin_t, 0, 2), None, n_lat=s, tn=bw)
            vt = jnp.swapaxes(proj[:, :, bw:2 * bw], 1, 2).reshape(b, DA_HEADS, LANES, s + n_ctx)
            ua = _diff_attention(proj, vt, ev_lambda[j], ev_subln[j], n_lat=s, lam_init=lam_init)
            ub = _short_conv(proj, ev_conv[j], n_lat=s)
        else:
            w = od_w_in[j]
            w_lr = jnp.pad(w[:, od_lr0:od_lr0 + 2 * GLA_RANK], ((0, 0), (0, LANES - 2 * GLA_RANK))).astype(BF16)
            w = jnp.concatenate([w[:, :od_lr0], w[:, od_lr0 + 2 * GLA_RANK:]], axis=1).astype(BF16)
            proj, lr = _inproj(xall, modb, modc, g_pre[l], w, None, w_lr, n_lat=s, tn=bw)
            tb = _natten_bias_table(od_rpb[j])
            ua = _neighbourhood_attention(proj, tb, n_lat=s, cols=na_cols)
            gu = jnp.zeros((2, GLA_HEADS, LANES, LANES), F32)
            gup = od_gate_up[j].reshape(2, GLA_RANK, GLA_HEADS, LANES).transpose(0, 2, 1, 3)
            gu = gu.at[0, :, :GLA_RANK].set(gup[0]).at[1, :, GLA_RANK:2 * GLA_RANK].set(gup[1]).astype(BF16)
            gb = od_gate_bias[j].reshape(2, GLA_HEADS, 1, LANES)
            ub = _gla(proj, lr, gu, gb, od_gnorm[j], n_lat=s, cols=gla_cols)
        xall = _outproj(ua, ub, w_out[l].astype(BF16), xall, modb, modc, g_post[l], n_lat=s,
                        latent_only=l == depth - 1)
    return xall
```

```python
import functools
import math

import numpy as np
import jax
import jax.numpy as jnp
from jax import lax
from jax.experimental import pallas as pl
from jax.experimental.pallas import tpu as pltpu

F32 = jnp.float32
BF16 = jnp.bfloat16

LANES = 128
GRID_W = 64
EXPAND = 2
DA_HEADS = 8
DA_QK = 64
SC_WIDTH = 3
NA_HEADS = 16
NA_DIM = 64
WIN_ROWS = 8
WIN_COLS = 16
GLA_HEADS = 4
GLA_RANK = 16
GLA_TAU = 16.0
GLA_CHUNK = 64
ROPE_BASE = 10000.0
EPS = 1e-6
LOG2E = math.log2(math.e)
ROW_SPLIT = 4
ROPE_ROWS = 64
ATT_TQ = 512
ATT_KCHUNK = 256
NA_GROUP = 8
GLA_GROUP = 8
GLA_HEADS_PER_STEP = 2
VMEM_LIMIT = 56 * 1024 * 1024


def _cparams(sem):
    return pltpu.CompilerParams(dimension_semantics=sem, vmem_limit_bytes=VMEM_LIMIT)


def _silu(t):
    return t * jax.nn.sigmoid(t)


def _mod_kernel(c_ref, w_ref, b_ref, o_ref):
    s = _silu(c_ref[...])
    o_ref[0] = jnp.dot(s, w_ref[0], preferred_element_type=F32) + b_ref[0]


def _modulation(cc, w_mod, b_mod):
    depth, d, d3 = w_mod.shape
    nrow = cc.shape[0]
    tn = d
    return pl.pallas_call(
        _mod_kernel,
        grid=(depth, d3 // tn),
        in_specs=[
            pl.BlockSpec((nrow, d), lambda l, j: (0, 0)),
            pl.BlockSpec((1, d, tn), lambda l, j: (l, 0, j)),
            pl.BlockSpec((1, 1, tn), lambda l, j: (l, 0, j)),
        ],
        out_specs=pl.BlockSpec((1, nrow, tn), lambda l, j: (l, 0, j)),
        out_shape=jax.ShapeDtypeStruct((depth, nrow, d3), F32),
        compiler_params=_cparams(("arbitrary", "arbitrary")),
        name="modulation",
    )(cc, w_mod, b_mod.reshape(depth, 1, d3))


def _inproj_kernel(*refs, tm, d, n_lat, rope_tiles, has_lr):
    x_ref, mb_ref, mc_ref, g_ref, w_ref = refs[:5]
    rest = list(refs[5:])
    cos_ref, sin_ref = (rest.pop(0), rest.pop(0)) if rope_tiles else (None, None)
    wlr_ref = rest.pop(0) if has_lr else None
    o_ref = rest.pop(0)
    lr_ref = rest.pop(0) if has_lr else None
    h_scr = rest.pop(0)
    acc_scr = rest.pop(0) if rope_tiles else None
    k = pl.program_id(1)
    j = pl.program_id(2)
    n_blk = pl.num_programs(1)

    @pl.when(j == 0)
    def _():
        xv = x_ref[0]
        xn = xv * lax.rsqrt(jnp.mean(xv * xv, axis=-1, keepdims=True) + EPS)
        mb = mb_ref[0]
        gain_b = g_ref[...] * (1.0 + mb[:, d:2 * d])
        shift_b = mb[:, 0:d]

        def emit(gain, shift):
            h_scr[...] = (xn * gain + shift).astype(BF16)
            if has_lr:
                lr_ref[0] = jnp.dot(h_scr[...], wlr_ref[...], preferred_element_type=F32)

        pl.when(k < n_blk - 1)(lambda: emit(gain_b, shift_b))

        @pl.when(k == n_blk - 1)
        def _():
            is_ctx = k * tm + lax.broadcasted_iota(jnp.int32, (tm, 1), 0) >= n_lat
            gain_c = g_ref[...] * (1.0 + mc_ref[:, d:2 * d])
            emit(jnp.where(is_ctx, gain_c, gain_b), jnp.where(is_ctx, mc_ref[:, 0:d], shift_b))

    tn = o_ref.shape[2]

    def project():
        w_tile = w_ref[:, pl.ds(pl.multiple_of(j * tn, tn), tn)]
        return jnp.dot(h_scr[...], w_tile, preferred_element_type=F32)

    if rope_tiles:
        is_rope = functools.reduce(jnp.logical_or, [j == t for t in rope_tiles])

        @pl.when(is_rope)
        def _():
            acc_scr[...] = project()

            def rows_body(c, carry):
                rows = pl.ds(pl.multiple_of(c * ROPE_ROWS, ROPE_ROWS), ROPE_ROWS)
                cos = cos_ref[0, rows, :]
                sin = sin_ref[0, rows, :]
                for hd in range(tn // LANES):
                    a = acc_scr[rows, hd * LANES:(hd + 1) * LANES]
                    o_ref[0, rows, hd * LANES:(hd + 1) * LANES] = (
                        a * cos + pltpu.roll(a, LANES // 2, 1) * sin).astype(o_ref.dtype)
                return carry

            lax.fori_loop(0, tm // ROPE_ROWS, rows_body, 0, unroll=True)

        @pl.when(jnp.logical_not(is_rope))
        def _():
            o_ref[0] = project().astype(o_ref.dtype)
    else:
        o_ref[0] = project().astype(o_ref.dtype)


def _inproj(xall, modb, modc, g, w, rope, w_lr, *, n_lat, tn):
    b, r, d = xall.shape
    nc = w.shape[1]
    tm = r // ROW_SPLIT
    has_lr = w_lr is not None
    in_specs = [
        pl.BlockSpec((1, tm, d), lambda bi, k, j: (bi, k, 0)),
        pl.BlockSpec((1, 1, 3 * d), lambda bi, k, j: (bi, 0, 0)),
        pl.BlockSpec((1, 3 * d), lambda bi, k, j: (0, 0)),
        pl.BlockSpec((1, d), lambda bi, k, j: (0, 0)),
        pl.BlockSpec((d, nc), lambda bi, k, j: (0, 0), pipeline_mode=pl.Buffered(1)),
    ]
    args = [xall, modb, modc, g.reshape(1, d), w]
    rope_tiles = ()
    if rope is not None:
        cos_t, sin_t, k_tile, q_tile = rope
        rope_tiles = (k_tile, q_tile)
        table = pl.BlockSpec((1, tm, LANES), lambda bi, k, j: (jnp.where(j == q_tile, 1, 0), k, 0))
        in_specs += [table, table]
        args += [cos_t, sin_t]
    out_specs = [pl.BlockSpec((1, tm, tn), lambda bi, k, j: (bi, k, j))]
    out_shape = [jax.ShapeDtypeStruct((b, r, nc), BF16)]
    if has_lr:
        in_specs.append(pl.BlockSpec((d, LANES), lambda bi, k, j: (0, 0)))
        args.append(w_lr)
        out_specs.append(pl.BlockSpec((1, tm, LANES), lambda bi, k, j: (bi, k, 0)))
        out_shape.append(jax.ShapeDtypeStruct((b, r, LANES), F32))
    res = pl.pallas_call(
        functools.partial(_inproj_kernel, tm=tm, d=d, n_lat=n_lat, rope_tiles=rope_tiles, has_lr=has_lr),
        grid=(b, ROW_SPLIT, nc // tn),
        in_specs=in_specs,
        out_specs=out_specs,
        out_shape=out_shape,
        scratch_shapes=[pltpu.VMEM((tm, d), BF16)] + ([pltpu.VMEM((tm, tn), F32)] if rope_tiles else []),
        compiler_params=_cparams(("arbitrary", "arbitrary", "arbitrary")),
        name="inproj",
    )(*args)
    return res if has_lr else (res[0], None)


def _outproj_kernel(ua_ref, ub_ref, w1_ref, w2_ref, x_ref, mb_ref, mc_ref, g_ref, o_ref, *, tm, d, n_lat):
    k = pl.program_id(1)
    y = jnp.dot(ua_ref[0], w1_ref[...], preferred_element_type=F32)
    y = y + jnp.dot(ub_ref[0], w2_ref[...], preferred_element_type=F32)
    y = y * lax.rsqrt(jnp.mean(y * y, axis=-1, keepdims=True) + EPS) * g_ref[...]
    row = k * tm + lax.broadcasted_iota(jnp.int32, (tm, 1), 0)
    gate = jnp.where(row >= n_lat, mc_ref[:, 2 * d:3 * d], mb_ref[0][:, 2 * d:3 * d])
    o_ref[0] = x_ref[0] + gate * y


def _outproj(ua, ub, w, xall, modb, modc, g, *, n_lat, latent_only):
    b, r, d = xall.shape
    bw = ua.shape[2]
    tm = n_lat // ROW_SPLIT if latent_only else r // ROW_SPLIT
    return pl.pallas_call(
        functools.partial(_outproj_kernel, tm=tm, d=d, n_lat=n_lat),
        grid=(b, ROW_SPLIT),
        in_specs=[
            pl.BlockSpec((1, tm, bw), lambda bi, k: (bi, k, 0)),
            pl.BlockSpec((1, tm, bw), lambda bi, k: (bi, k, 0)),
            pl.BlockSpec((bw, d), lambda bi, k: (0, 0)),
            pl.BlockSpec((bw, d), lambda bi, k: (1, 0)),
            pl.BlockSpec((1, tm, d), lambda bi, k: (bi, k, 0)),
            pl.BlockSpec((1, 1, 3 * d), lambda bi, k: (bi, 0, 0)),
            pl.BlockSpec((1, 3 * d), lambda bi, k: (0, 0)),
            pl.BlockSpec((1, d), lambda bi, k: (0, 0)),
        ],
        out_specs=pl.BlockSpec((1, tm, d), lambda bi, k: (bi, k, 0)),
        out_shape=jax.ShapeDtypeStruct((b, n_lat if latent_only else r, d), F32),
        input_output_aliases={} if latent_only else {4: 0},
        compiler_params=_cparams(("arbitrary", "arbitrary")),
        name="outproj",
    )(ua, ub, w, w, xall, modb, modc, g.reshape(1, d))


def _dattn_lambda(lam_ref, lam_init):
    lp = lam_ref[...]
    return (jnp.exp(jnp.sum(lp[0:1] * lp[1:2], axis=-1, keepdims=True))
            - jnp.exp(jnp.sum(lp[2:3] * lp[3:4], axis=-1, keepdims=True)) + lam_init)


def _dattn_split_maps(q):
    first = (lax.broadcasted_iota(jnp.int32, (1, LANES), 1) & (DA_QK // 2)) == 0
    zero = jnp.zeros_like(q)
    return jnp.concatenate([jnp.where(first, q, zero), jnp.where(first, zero, q)], axis=0)


def _dattn_finish(o, sub_ref, g_ref, o_ref, lam_init):
    o = o * lax.rsqrt(jnp.mean(o * o, axis=-1, keepdims=True) + EPS) * sub_ref[...] * (1.0 - lam_init)
    o_ref[0] = (o * _silu(g_ref[0].astype(F32))).astype(o_ref.dtype)


def _dattn_kernel(lam_ref, sub_ref, qn_ref, k_ref, vt_ref, g_ref, o_ref, sa_ref, ma_ref, sb_ref, mb_ref,
                  acc_ref, den_ref, *, n_blocks, lam_init):
    t = pl.program_id(0)
    lam = _dattn_lambda(lam_ref, lam_init)
    nt = (((1,), (1,)), ((), ()))
    tq = qn_ref.shape[1]

    def step(cur, prev, fin):
        if fin:
            o = acc_ref[...] * (1.0 / den_ref[...])
            _dattn_finish((o[:, :tq] - lam * o[:, tq:]).T, sub_ref, g_ref, o_ref, lam_init)
        r = k_ref.shape[1]
        bounds = list(range(0, r - ATT_KCHUNK + 1, ATT_KCHUNK)) + [r]
        if cur is not None:
            q2 = _dattn_split_maps(qn_ref[0])
        m_cur = l_prev = o_prev = None
        for lo, hi in zip(bounds[:-1], bounds[1:]):
            if cur is not None:
                s = lax.dot_general(k_ref[0, lo:hi, :], q2, nt, preferred_element_type=F32)
                cur[0][lo:hi, :] = s
                m = jnp.max(s, axis=0, keepdims=True)
                m_cur = m if m_cur is None else jnp.maximum(m_cur, m)
            if prev is not None:
                e = jnp.exp2(prev[0][lo:hi, :] - prev[1][...])
                l = jnp.sum(e, axis=0, keepdims=True)
                o = jnp.dot(vt_ref[0, 0, :, lo:hi], e.astype(BF16), preferred_element_type=F32)
                l_prev = l if l_prev is None else l_prev + l
                o_prev = o if o_prev is None else o_prev + o
        if cur is not None:
            cur[1][...] = m_cur
        if prev is not None:
            acc_ref[...] = o_prev
            den_ref[...] = l_prev

    bufs = ((sa_ref, ma_ref), (sb_ref, mb_ref))
    last = bufs[(n_blocks - 1) % 2]
    pl.when(t == 0)(lambda: step(bufs[0], None, False))
    pl.when(t == 1)(lambda: step(bufs[1], bufs[0], False))
    pl.when((t >= 2) & (t < n_blocks) & (t % 2 == 0))(lambda: step(bufs[0], bufs[1], True))
    pl.when((t >= 2) & (t < n_blocks) & (t % 2 == 1))(lambda: step(bufs[1], bufs[0], True))
    pl.when(t == n_blocks)(lambda: step(None, last, True))
    pl.when(t == n_blocks + 1)(lambda: step(None, None, True))


def _dattn_ctx_kernel(lam_ref, sub_ref, q_ref, k_ref, v_ref, g_ref, ua_ref, o_ref, *, lam_init):
    del ua_ref
    lam = _dattn_lambda(lam_ref, lam_init)
    nt = (((1,), (1,)), ((), ()))
    n = q_ref.shape[1]
    s = lax.dot_general(_dattn_split_maps(q_ref[0]), k_ref[0], nt, preferred_element_type=F32)
    e = jnp.exp2(s - jnp.max(s, axis=-1, keepdims=True))
    p = e / jnp.sum(e, axis=-1, keepdims=True)
    a = (p[:n] - lam * p[n:]).astype(BF16)
    _dattn_finish(jnp.dot(a, v_ref[0], preferred_element_type=F32), sub_ref, g_ref, o_ref, lam_init)


def _diff_attention(proj, vt, lam_p, subln, *, n_lat, lam_init):
    b, r, _ = proj.shape
    h = DA_HEADS
    tq = ATT_TQ
    nq = n_lat // tq
    n_ctx = r - n_lat
    sub = subln.reshape(1, LANES)
    small = [pl.BlockSpec(lam_p.shape, lambda *_: (0, 0)), pl.BlockSpec((1, LANES), lambda *_: (0, 0))]
    n_blocks = b * h * nq

    def unit(t, lag):
        u = jnp.clip(t - lag, 0, n_blocks - 1)
        return u // (h * nq), (u // nq) % h, u % nq

    def q_map(t):
        bi, hi, i = unit(t, 0)
        return bi, i, 2 * h + hi

    def k_map(t):
        bi, hi, _ = unit(t, 0)
        return bi, 0, hi

    def vt_map(t):
        bi, hi, _ = unit(t, 1)
        return bi, hi, 0, 0

    def g_map(t):
        bi, hi, i = unit(t, 2)
        return bi, i, 3 * h + hi

    def o_map(t):
        bi, hi, i = unit(t, 2)
        return bi, i, hi

    ua = pl.pallas_call(
        functools.partial(_dattn_kernel, n_blocks=n_blocks, lam_init=lam_init),
        grid=(n_blocks + 2,),
        in_specs=small + [
            pl.BlockSpec((1, tq, LANES), q_map),
            pl.BlockSpec((1, r, LANES), k_map),
            pl.BlockSpec((1, 1, LANES, r), vt_map),
            pl.BlockSpec((1, tq, LANES), g_map),
        ],
        out_specs=pl.BlockSpec((1, tq, LANES), o_map),
        out_shape=jax.ShapeDtypeStruct((b, r, h * LANES), BF16),
        scratch_shapes=[pltpu.VMEM((r, 2 * tq), F32), pltpu.VMEM((1, 2 * tq), F32)] * 2
        + [pltpu.VMEM((LANES, 2 * tq), F32), pltpu.VMEM((1, 2 * tq), F32)],
        compiler_params=_cparams(("arbitrary",)),
        name="diff_attention",
    )(lam_p, sub, proj, proj, vt, proj)
    cb = n_lat // n_ctx
    ctx_rows = lambda off: pl.BlockSpec((1, n_ctx, LANES), lambda bi, hi: (bi, cb, off + hi))
    return pl.pallas_call(
        functools.partial(_dattn_ctx_kernel, lam_init=lam_init),
        grid=(b, h),
        in_specs=small + [ctx_rows(2 * h), ctx_rows(0), ctx_rows(h), ctx_rows(3 * h),
                          pl.BlockSpec(memory_space=pl.ANY)],
        out_specs=ctx_rows(0),
        out_shape=jax.ShapeDtypeStruct(ua.shape, ua.dtype),
        input_output_aliases={6: 0},
        compiler_params=_cparams(("arbitrary", "arbitrary")),
        name="diff_attention_ctx",
    )(lam_p, sub, proj, proj, proj, proj, ua)


def _sconv_kernel(h_ref, b_ref, c_ref, g_ref, w_ref, o_ref, *, n_lat):
    r = h_ref.shape[1]
    p = c_ref[0].astype(F32) * h_ref[0].astype(F32)
    row = lax.broadcasted_iota(jnp.int32, (r, 1), 0)
    prev = jnp.where((row == 0) | (row == n_lat), 0.0, pltpu.roll(p, 1, 0))
    nxt = jnp.where((row == n_lat - 1) | (row == r - 1), 0.0, pltpu.roll(p, r - 1, 0))
    w = w_ref[...]
    conv = w[0:1] * prev + w[1:2] * p + w[2:3] * nxt
    o_ref[0] = (b_ref[0].astype(F32) * conv * _silu(g_ref[0].astype(F32))).astype(o_ref.dtype)


def _short_conv(proj, conv_w, *, n_lat):
    b, r, _ = proj.shape
    nb = conv_w.shape[1] // LANES
    base = 4 * nb
    spec = lambda off: pl.BlockSpec((1, r, LANES), lambda bi, j: (bi, 0, off + j))
    return pl.pallas_call(
        functools.partial(_sconv_kernel, n_lat=n_lat),
        grid=(b, nb),
        in_specs=[spec(base), spec(base + nb), spec(base + 2 * nb), spec(base + 3 * nb),
                  pl.BlockSpec((SC_WIDTH, LANES), lambda bi, j: (0, j))],
        out_specs=pl.BlockSpec((1, r, LANES), lambda bi, j: (bi, 0, j)),
        out_shape=jax.ShapeDtypeStruct((b, r, nb * LANES), BF16),
        compiler_params=_cparams(("arbitrary", "arbitrary")),
        name="short_conv",
    )(proj, proj, proj, proj, conv_w)


def _natten_kernel(q_ref, k_ref, v_ref, g_ref, tb_ref, o_ref, *, n_lat, n_rows):
    lane = lax.broadcasted_iota(jnp.int32, (1, LANES), 1)
    first = lane < NA_DIM
    nt = (((1,), (1,)), ((), ()))
    win = WIN_ROWS * GRID_W
    kc = k_ref[0, n_lat:, :]
    vc = v_ref[0, n_lat:, :]

    def stack_heads(q):
        q = q * (NA_DIM ** -0.5)
        zero = jnp.zeros_like(q)
        return jnp.concatenate([jnp.where(first, q, zero), jnp.where(first, zero, q)], axis=0)

    def softmax(parts):
        m = functools.reduce(jnp.maximum, [jnp.max(s, axis=-1, keepdims=True) for s in parts])
        es = [jnp.exp2(s - m) for s in parts]
        l = functools.reduce(jnp.add, [jnp.sum(e, axis=-1, keepdims=True) for e in es])
        return [e.astype(BF16) for e in es], 1.0 / l

    def unstack_heads(o2):
        n = o2.shape[0] // 2
        return jnp.where(first, o2[:n], o2[n:])

    def group_body(gi, carry):
        rows = [gi * NA_GROUP + t for t in range(NA_GROUP)]
        starts = [jnp.clip(rr - WIN_ROWS // 2, 0, n_rows - WIN_ROWS) for rr in rows]
        qsl = [pl.ds(pl.multiple_of(rr * GRID_W, GRID_W), GRID_W) for rr in rows]
        ksl = [pl.ds(pl.multiple_of(rs * GRID_W, GRID_W), win) for rs in starts]
        scores = {}
        weights = {}
        for i in range(NA_GROUP + 2):
            if i < NA_GROUP:
                q2 = stack_heads(q_ref[0, qsl[i], :])
                s_loc = lax.dot_general(q2, k_ref[0, ksl[i], :], nt, preferred_element_type=F32)
                off = starts[i] - rows[i] + (WIN_ROWS - 1)
                s_loc = s_loc + jnp.concatenate([tb_ref[0, off + 2 * p] for p in range(WIN_ROWS // 2)], axis=1)
                scores[i] = [s_loc, lax.dot_general(q2, kc, nt, preferred_element_type=F32)]
            if 0 <= i - 1 < NA_GROUP:
                weights[i - 1] = softmax(scores.pop(i - 1))
            if 0 <= i - 2 < NA_GROUP:
                es, rl = weights.pop(i - 2)
                qs, ks = qsl[i - 2], ksl[i - 2]
                o2 = jnp.dot(es[0], v_ref[0, ks, :], preferred_element_type=F32)
                o2 = o2 + jnp.dot(es[1], vc, preferred_element_type=F32)
                out = unstack_heads(o2 * rl)
                o_ref[0, qs, :] = (out * _silu(g_ref[0, qs, :].astype(F32))).astype(o_ref.dtype)
        return carry

    lax.fori_loop(0, n_rows // NA_GROUP, group_body, 0)

    s = lax.dot_general(stack_heads(q_ref[0, n_lat:, :]), kc, nt, preferred_element_type=F32)
    es, rl = softmax([s])
    outc = unstack_heads(jnp.dot(es[0], vc, preferred_element_type=F32) * rl)
    o_ref[0, n_lat:, :] = (outc * _silu(g_ref[0, n_lat:, :].astype(F32))).astype(o_ref.dtype)


def _neighbourhood_attention(proj, tb, *, n_lat, cols):
    b, r, _ = proj.shape
    nb = NA_HEADS // 2
    kb, vb, qb, gb = (c // LANES for c in cols)
    spec = lambda off: pl.BlockSpec((1, r, LANES), lambda bi, j: (bi, 0, off + j))
    return pl.pallas_call(
        functools.partial(_natten_kernel, n_lat=n_lat, n_rows=n_lat // GRID_W),
        grid=(b, nb),
        in_specs=[spec(qb), spec(kb), spec(vb), spec(gb),
                  pl.BlockSpec((1,) + tb.shape[1:], lambda bi, j: (j, 0, 0, 0))],
        out_specs=pl.BlockSpec((1, r, LANES), lambda bi, j: (bi, 0, j)),
        out_shape=jax.ShapeDtypeStruct((b, r, nb * LANES), BF16),
        compiler_params=_cparams(("arbitrary", "arbitrary")),
        name="neighbourhood_attention",
    )(proj, proj, proj, proj, tb)


def _natten_bias_table(rpb):
    qcol = np.arange(GRID_W)[:, None]
    kcol = np.arange(GRID_W)[None, :]
    win_start = np.clip(qcol - WIN_COLS // 2, 0, GRID_W - WIN_COLS)
    valid = (kcol >= win_start) & (kcol < win_start + WIN_COLS)
    pad = GRID_W - WIN_COLS
    padded = jnp.pad(rpb.astype(F32), ((0, 0), (0, 0), (pad, pad)))
    toe = jnp.stack([padded[:, :, GRID_W - 1 - q:2 * GRID_W - 1 - q] for q in range(GRID_W)], axis=2)
    toe = jnp.where(jnp.asarray(valid)[None, None], toe * LOG2E, -jnp.inf)
    t = jnp.concatenate([toe[:, :-1], toe[:, 1:]], axis=-1)
    nh, n_o = t.shape[:2]
    t = jnp.transpose(t.reshape(nh // 2, 2, n_o, GRID_W, 2 * GRID_W), (0, 2, 1, 3, 4))
    return t.reshape(nh // 2, n_o, 2 * GRID_W, 2 * GRID_W)


def _gla_kernel(q_ref, k_ref, v_ref, g_ref, lr_ref, gu_ref, gb_ref, gn_ref, o_ref, of_scr, st_scr,
                *, n_lat, n_ctx, qscale):
    ch = GLA_CHUNK
    mid = ch // 2
    n_heads, dv, dk = st_scr.shape
    nt = (((1,), (1,)), ((), ()))
    tn = (((0,), (0,)), ((), ()))
    trow = lax.broadcasted_iota(jnp.int32, (ch, ch), 0)
    tcol = lax.broadcasted_iota(jnp.int32, (ch, ch), 1)
    srow = lax.broadcasted_iota(jnp.int32, (ch, 1), 0)

    def cumsum_rows(g, reverse):
        acc = g
        sh = 1
        while sh < ch:
            if reverse:
                acc = acc + jnp.where(srow < ch - sh, pltpu.roll(acc, ch - sh, 0), 0.0)
            else:
                acc = acc + jnp.where(srow >= sh, pltpu.roll(acc, sh, 0), 0.0)
            sh *= 2
        return acc

    def group(starts, dirn):
        reverse = dirn == 1
        keep = (tcol >= trow) if reverse else (tcol <= trow)
        rows_l = [pl.ds(s if isinstance(s, int) else pl.multiple_of(s, ch), ch) for s in starts]
        items = [(rows, hh, slice(hh * dk, (hh + 1) * dk), slice(hh * dv, (hh + 1) * dv))
                 for rows in rows_l for hh in range(n_heads)]
        zs = [jnp.dot(lr_ref[0, rows, :].astype(BF16), gu_ref[dirn, hh], preferred_element_type=F32)
              for rows, hh, _, _ in items]
        ops = []
        for (rows, hh, kc, _), z in zip(items, zs):
            g = jax.nn.log_sigmoid(z + gb_ref[dirn, hh]) * (1.0 / GLA_TAU)
            bcum = cumsum_rows(g, reverse)
            b_end = bcum[0:1] if reverse else bcum[ch - 1:ch]
            rho = bcum[mid:mid + 1]
            up = jnp.exp(bcum - rho)
            dn = jnp.exp(rho - bcum)
            qf = q_ref[0, rows, kc].astype(F32) * qscale
            kf = k_ref[0, rows, kc].astype(F32)
            ops.append(dict(
                q_state=(qf * (up * jnp.exp(rho))).astype(BF16),
                q_local=(qf * up).astype(BF16),
                k_local=(kf * dn).astype(BF16),
                k_state=(kf * (dn * jnp.exp(b_end - rho))).astype(BF16),
                decay=jnp.exp(b_end)))
        atts = [lax.dot_general(o["q_local"], o["k_local"], nt, preferred_element_type=F32) for o in ops]
        st_adds = [lax.dot_general(v_ref[0, rows, vc], o["k_state"], tn, preferred_element_type=F32)
                   for (rows, _, _, vc), o in zip(items, ops)]
        o_locals = [jnp.dot(jnp.where(keep, att, 0.0).astype(BF16), v_ref[0, rows, vc], preferred_element_type=F32)
                    for (rows, _, _, vc), att in zip(items, atts)]
        sts = [st_scr[hh] for hh in range(n_heads)]
        states = []
        for (_, hh, _, _), o, st_add in zip(items, ops, st_adds):
            states.append(sts[hh].astype(BF16))
            sts[hh] = sts[hh] * o["decay"] + st_add
        for hh in range(n_heads):
            st_scr[hh] = sts[hh]
        outs = [o_local + lax.dot_general(o["q_state"], s_in, nt, preferred_element_type=F32)
                for o, o_local, s_in in zip(ops, o_locals, states)]
        for (rows, hh, _, vc), o in zip(items, outs):
            if dirn == 0:
                of_scr[rows, vc] = o
            else:
                o = o + of_scr[rows, vc]
                o = o * lax.rsqrt(jnp.mean(o * o, axis=-1, keepdims=True) + EPS) * gn_ref[...]
                o_ref[0, rows, vc] = (o * _silu(g_ref[0, rows, vc].astype(F32))).astype(o_ref.dtype)

    n_cc = n_ctx // ch
    n_lg = n_lat // (ch * GLA_GROUP)
    gsz = ch * GLA_GROUP

    def latent_group(gi, dirn):
        if dirn == 0:
            group([gi * gsz + t * ch for t in range(GLA_GROUP)], 0)
        else:
            group([(n_lg - 1 - gi) * gsz + t * ch for t in range(GLA_GROUP - 1, -1, -1)], 1)

    st_scr[...] = jnp.zeros_like(st_scr)
    group([n_lat + c * ch for c in range(n_cc)], 0)
    lax.fori_loop(0, n_lg, lambda gi, carry: (latent_group(gi, 0), carry)[1], 0)
    st_scr[...] = jnp.zeros_like(st_scr)
    group([n_lat + c * ch for c in range(n_cc - 1, -1, -1)], 1)
    lax.fori_loop(0, n_lg, lambda gi, carry: (latent_group(gi, 1), carry)[1], 0)


def _gla(proj, lr, gu, gb, gnorm, *, n_lat, cols):
    b, r, _ = proj.shape
    nh = GLA_HEADS
    dk = LANES
    dv = 2 * LANES
    kcol, vcol, qcol, gcol = cols
    hs = GLA_HEADS_PER_STEP
    wk, wv = hs * dk, hs * dv
    return pl.pallas_call(
        functools.partial(_gla_kernel, n_lat=n_lat, n_ctx=r - n_lat, qscale=dk ** -0.5),
        grid=(b, nh // hs),
        in_specs=[
            pl.BlockSpec((1, r, wk), lambda bi, hi: (bi, 0, qcol // wk + hi)),
            pl.BlockSpec((1, r, wk), lambda bi, hi: (bi, 0, kcol // wk + hi)),
            pl.BlockSpec((1, r, wv), lambda bi, hi: (bi, 0, vcol // wv + hi)),
            pl.BlockSpec((1, r, wv), lambda bi, hi: (bi, 0, gcol // wv + hi)),
            pl.BlockSpec((1, r, LANES), lambda bi, hi: (bi, 0, 0)),
            pl.BlockSpec((2, hs, LANES, dk), lambda bi, hi: (0, hi, 0, 0)),
            pl.BlockSpec((2, hs, 1, dk), lambda bi, hi: (0, hi, 0, 0)),
            pl.BlockSpec((1, dv), lambda bi, hi: (0, 0)),
        ],
        out_specs=pl.BlockSpec((1, r, wv), lambda bi, hi: (bi, 0, hi)),
        out_shape=jax.ShapeDtypeStruct((b, r, nh * dv), BF16),
        scratch_shapes=[pltpu.VMEM((r, wv), F32), pltpu.VMEM((hs, dv, dk), F32)],
        compiler_params=_cparams(("arbitrary", "arbitrary")),
        name="gla",
    )(proj, proj, proj, proj, lr, gu, gb, gnorm.reshape(1, dv))


def _rope_tables(n_lat, n_ctx):
    half = DA_QK // 2
    inv = 1.0 / (ROPE_BASE ** (jnp.arange(0, half, 2, dtype=F32) / half))
    t = jnp.arange(n_lat)
    row = (t // GRID_W).astype(F32)[:, None] * inv
    col = (t % GRID_W).astype(F32)[:, None] * inv
    ang = jnp.concatenate([row, row, col, col], axis=-1)
    lane = np.arange(LANES)
    dim = 32 * ((lane >> 4) & 1) + 16 * (lane >> 6) + (lane & 15)
    sign = jnp.asarray(np.where((lane >> 6) == 0, -1.0, 1.0), F32)
    cos = jnp.concatenate([jnp.cos(ang)[:, dim], jnp.ones((n_ctx, LANES), F32)], axis=0)
    sin = jnp.concatenate([jnp.sin(ang)[:, dim] * sign, jnp.zeros((n_ctx, LANES), F32)], axis=0)
    qf = LOG2E * DA_QK ** -0.5
    return jnp.stack([cos, cos * qf]), jnp.stack([sin, sin * qf])


def _pair_layout(w):
    d = w.shape[0]
    w = w.reshape(d, -1, 2, 2, 2, DA_QK // 4)
    return jnp.transpose(w, (0, 1, 4, 2, 3, 5)).reshape(d, -1)


def kernel(x, c, ctx, c_ctx, w_mod, b_mod, g_pre, g_post, w_out, ev_w_in, ev_lambda, ev_subln, ev_conv,
           od_w_in, od_rpb, od_gate_up, od_gate_bias, od_gnorm):
    b, s, d = x.shape
    n_ctx = ctx.shape[1]
    depth = w_mod.shape[0]
    bw = EXPAND * d // 2
    assert bw == DA_HEADS * LANES == NA_HEADS * NA_DIM and (s + n_ctx) % (ROW_SPLIT * 16) == 0
    assert s % ATT_TQ == 0 and s % n_ctx == 0 and s % GRID_W == 0 and n_ctx % GLA_CHUNK == 0
    tm = (s + n_ctx) // ROW_SPLIT
    assert tm % ROPE_ROWS == 0 and (ROW_SPLIT - 1) * tm <= s and s % (ROW_SPLIT * 16) == 0

    xall = jnp.concatenate([x, ctx], axis=1)
    nrow = -(-(b + 1) // 8) * 8
    cc = jnp.zeros((nrow, d), F32).at[:b].set(c).at[b].set(c_ctx)
    mod = _modulation(cc, w_mod, b_mod)
    cos_t, sin_t = _rope_tables(s, n_ctx)

    gk = bw // 2
    od_lr0 = 3 * bw + gk
    od_cols = np.cumsum([0, bw, bw, gk, bw, bw, gk, bw])
    na_cols = (int(od_cols[0]), int(od_cols[1]), int(od_cols[4]), int(od_cols[6]))
    gla_cols = (int(od_cols[2]), int(od_cols[3]), int(od_cols[5]), int(od_cols[7]))

    for l in range(depth):
        j = l // 2
        modb = mod[l, :b].reshape(b, 1, 3 * d)
        modc = mod[l, b:b + 1]
        if l % 2 == 0:
            lam_init = 0.8 - 0.6 * math.exp(-0.3 * l)
            w = ev_w_in[j]
            w = jnp.concatenate([_pair_layout(w[:, :bw]), w[:, bw:2 * bw], _pair_layout(w[:, 2 * bw:3 * bw]),
                                 w[:, 3 * bw:]], axis=1).astype(BF16)
            proj, _ = _inproj(xall, modb, modc, g_pre[l], w, (cos_t, sin_t, 0, 2), None, n_lat=s, tn=bw)
            vt = jnp.swapaxes(proj[:, :, bw:2 * bw], 1, 2).reshape(b, DA_HEADS, LANES, s + n_ctx)
            ua = _diff_attention(proj, vt, ev_lambda[j], ev_subln[j], n_lat=s, lam_init=lam_init)
            ub = _short_conv(proj, ev_conv[j], n_lat=s)
        else:
            w = od_w_in[j]
            w_lr = jnp.pad(w[:, od_lr0:od_lr0 + 2 * GLA_RANK], ((0, 0), (0, LANES - 2 * GLA_RANK))).astype(BF16)
            w = jnp.concatenate([w[:, :od_lr0], w[:, od_lr0 + 2 * GLA_RANK:]], axis=1)
            col_scale = np.ones((1, w.shape[1]), np.float32)
            col_scale[:, na_cols[2]:na_cols[2] + bw] = LOG2E
            w = (w * col_scale).astype(BF16)
            proj, lr = _inproj(xall, modb, modc, g_pre[l], w, None, w_lr, n_lat=s, tn=bw)
            tb = _natten_bias_table(od_rpb[j])
            ua = _neighbourhood_attention(proj, tb, n_lat=s, cols=na_cols)
            gu = jnp.zeros((2, GLA_HEADS, LANES, LANES), F32)
            gup = od_gate_up[j].reshape(2, GLA_RANK, GLA_HEADS, LANES).transpose(0, 2, 1, 3)
            gu = gu.at[0, :, :GLA_RANK].set(gup[0]).at[1, :, GLA_RANK:2 * GLA_RANK].set(gup[1]).astype(BF16)
            gb = od_gate_bias[j].reshape(2, GLA_HEADS, 1, LANES)
            ub = _gla(proj, lr, gu, gb, od_gnorm[j], n_lat=s, cols=gla_cols)
        xall = _outproj(ua, ub, w_out[l].astype(BF16), xall, modb, modc, g_post[l], n_lat=s,
                        latent_only=l == depth - 1)
    return xall
```

```python
import functools
import math

import numpy as np
import jax
import jax.numpy as jnp
from jax import lax
from jax.experimental import pallas as pl
from jax.experimental.pallas import tpu as pltpu

F32 = jnp.float32
BF16 = jnp.bfloat16

LANES = 128
GRID_W = 64
EXPAND = 2
DA_HEADS = 8
DA_QK = 64
SC_WIDTH = 3
NA_HEADS = 16
NA_DIM = 64
WIN_ROWS = 8
WIN_COLS = 16
GLA_HEADS = 4
GLA_RANK = 16
GLA_TAU = 16.0
GLA_CHUNK = 64
ROPE_BASE = 10000.0
EPS = 1e-6
LOG2E = math.log2(math.e)
ROW_SPLIT = 4
ROPE_ROWS = 64
ATT_TQ = 512
ATT_KCHUNK = 256
NA_GROUP = 8
GLA_GROUP = 8
GLA_HEADS_PER_STEP = 2
VMEM_LIMIT = 56 * 1024 * 1024


def _cparams(sem):
    return pltpu.CompilerParams(dimension_semantics=sem, vmem_limit_bytes=VMEM_LIMIT)


def _silu(t):
    return t * jax.nn.sigmoid(t)


def _mod_kernel(c_ref, w_ref, b_ref, o_ref):
    s = _silu(c_ref[...])
    o_ref[0] = jnp.dot(s, w_ref[0], preferred_element_type=F32) + b_ref[0]


def _modulation(cc, w_mod, b_mod):
    depth, d, d3 = w_mod.shape
    nrow = cc.shape[0]
    tn = d
    return pl.pallas_call(
        _mod_kernel,
        grid=(depth, d3 // tn),
        in_specs=[
            pl.BlockSpec((nrow, d), lambda l, j: (0, 0)),
            pl.BlockSpec((1, d, tn), lambda l, j: (l, 0, j)),
            pl.BlockSpec((1, 1, tn), lambda l, j: (l, 0, j)),
        ],
        out_specs=pl.BlockSpec((1, nrow, tn), lambda l, j: (l, 0, j)),
        out_shape=jax.ShapeDtypeStruct((depth, nrow, d3), F32),
        compiler_params=_cparams(("arbitrary", "arbitrary")),
        name="modulation",
    )(cc, w_mod, b_mod.reshape(depth, 1, d3))


def _inproj_kernel(*refs, tm, d, n_lat, rope_tiles, has_lr):
    x_ref, mb_ref, mc_ref, g_ref, w_ref = refs[:5]
    rest = list(refs[5:])
    cos_ref, sin_ref = (rest.pop(0), rest.pop(0)) if rope_tiles else (None, None)
    wlr_ref = rest.pop(0) if has_lr else None
    o_ref = rest.pop(0)
    lr_ref = rest.pop(0) if has_lr else None
    h_scr = rest.pop(0)
    acc_scr = rest.pop(0) if rope_tiles else None
    k = pl.program_id(1)
    j = pl.program_id(2)
    n_blk = pl.num_programs(1)

    @pl.when(j == 0)
    def _():
        xv = x_ref[0]
        xn = xv * lax.rsqrt(jnp.mean(xv * xv, axis=-1, keepdims=True) + EPS)
        mb = mb_ref[0]
        gain_b = g_ref[...] * (1.0 + mb[:, d:2 * d])
        shift_b = mb[:, 0:d]

        def emit(gain, shift):
            h_scr[...] = (xn * gain + shift).astype(BF16)
            if has_lr:
                lr_ref[0] = jnp.dot(h_scr[...], wlr_ref[...], preferred_element_type=F32)

        pl.when(k < n_blk - 1)(lambda: emit(gain_b, shift_b))

        @pl.when(k == n_blk - 1)
        def _():
            is_ctx = k * tm + lax.broadcasted_iota(jnp.int32, (tm, 1), 0) >= n_lat
            gain_c = g_ref[...] * (1.0 + mc_ref[:, d:2 * d])
            emit(jnp.where(is_ctx, gain_c, gain_b), jnp.where(is_ctx, mc_ref[:, 0:d], shift_b))

    tn = o_ref.shape[2]

    def project():
        w_tile = w_ref[:, pl.ds(pl.multiple_of(j * tn, tn), tn)]
        return jnp.dot(h_scr[...], w_tile, preferred_element_type=F32)

    if rope_tiles:
        is_rope = functools.reduce(jnp.logical_or, [j == t for t in rope_tiles])

        @pl.when(is_rope)
        def _():
            acc_scr[...] = project()

            def rows_body(c, carry):
                rows = pl.ds(pl.multiple_of(c * ROPE_ROWS, ROPE_ROWS), ROPE_ROWS)
                cos = cos_ref[0, rows, :]
                sin = sin_ref[0, rows, :]
                for hd in range(tn // LANES):
                    a = acc_scr[rows, hd * LANES:(hd + 1) * LANES]
                    o_ref[0, rows, hd * LANES:(hd + 1) * LANES] = (
                        a * cos + pltpu.roll(a, LANES // 2, 1) * sin).astype(o_ref.dtype)
                return carry

            lax.fori_loop(0, tm // ROPE_ROWS, rows_body, 0, unroll=True)

        @pl.when(jnp.logical_not(is_rope))
        def _():
            o_ref[0] = project().astype(o_ref.dtype)
    else:
        o_ref[0] = project().astype(o_ref.dtype)


def _inproj(xall, modb, modc, g, w, rope, w_lr, *, n_lat, tn):
    b, r, d = xall.shape
    nc = w.shape[1]
    tm = r // ROW_SPLIT
    has_lr = w_lr is not None
    in_specs = [
        pl.BlockSpec((1, tm, d), lambda bi, k, j: (bi, k, 0)),
        pl.BlockSpec((1, 1, 3 * d), lambda bi, k, j: (bi, 0, 0)),
        pl.BlockSpec((1, 3 * d), lambda bi, k, j: (0, 0)),
        pl.BlockSpec((1, d), lambda bi, k, j: (0, 0)),
        pl.BlockSpec((d, nc), lambda bi, k, j: (0, 0), pipeline_mode=pl.Buffered(1)),
    ]
    args = [xall, modb, modc, g.reshape(1, d), w]
    rope_tiles = ()
    if rope is not None:
        cos_t, sin_t, k_tile, q_tile = rope
        rope_tiles = (k_tile, q_tile)
        table = pl.BlockSpec((1, tm, LANES), lambda bi, k, j: (jnp.where(j == q_tile, 1, 0), k, 0))
        in_specs += [table, table]
        args += [cos_t, sin_t]
    out_specs = [pl.BlockSpec((1, tm, tn), lambda bi, k, j: (bi, k, j))]
    out_shape = [jax.ShapeDtypeStruct((b, r, nc), BF16)]
    if has_lr:
        in_specs.append(pl.BlockSpec((d, LANES), lambda bi, k, j: (0, 0)))
        args.append(w_lr)
        out_specs.append(pl.BlockSpec((1, tm, LANES), lambda bi, k, j: (bi, k, 0)))
        out_shape.append(jax.ShapeDtypeStruct((b, r, LANES), F32))
    res = pl.pallas_call(
        functools.partial(_inproj_kernel, tm=tm, d=d, n_lat=n_lat, rope_tiles=rope_tiles, has_lr=has_lr),
        grid=(b, ROW_SPLIT, nc // tn),
        in_specs=in_specs,
        out_specs=out_specs,
        out_shape=out_shape,
        scratch_shapes=[pltpu.VMEM((tm, d), BF16)] + ([pltpu.VMEM((tm, tn), F32)] if rope_tiles else []),
        compiler_params=_cparams(("arbitrary", "arbitrary", "arbitrary")),
        name="inproj",
    )(*args)
    return res if has_lr else (res[0], None)


def _outproj_kernel(ua_ref, ub_ref, w1_ref, w2_ref, x_ref, mb_ref, mc_ref, g_ref, o_ref, *, tm, d, n_lat,
                    has_ctx):
    k = pl.program_id(1)
    y = jnp.dot(ua_ref[0], w1_ref[...], preferred_element_type=F32)
    y = y + jnp.dot(ub_ref[0], w2_ref[...], preferred_element_type=F32)
    y = y * lax.rsqrt(jnp.mean(y * y, axis=-1, keepdims=True) + EPS)
    gate_b = g_ref[...] * mb_ref[0][:, 2 * d:3 * d]
    if not has_ctx:
        o_ref[0] = x_ref[0] + gate_b * y
        return

    n_blk = pl.num_programs(1)

    @pl.when(k < n_blk - 1)
    def _():
        o_ref[0] = x_ref[0] + gate_b * y

    @pl.when(k == n_blk - 1)
    def _():
        is_ctx = k * tm + lax.broadcasted_iota(jnp.int32, (tm, 1), 0) >= n_lat
        gate_c = g_ref[...] * mc_ref[:, 2 * d:3 * d]
        o_ref[0] = x_ref[0] + jnp.where(is_ctx, gate_c, gate_b) * y


def _outproj(ua, ub, w, xall, modb, modc, g, *, n_lat, latent_only):
    b, r, d = xall.shape
    bw = ua.shape[2]
    tm = n_lat // ROW_SPLIT if latent_only else r // ROW_SPLIT
    return pl.pallas_call(
        functools.partial(_outproj_kernel, tm=tm, d=d, n_lat=n_lat, has_ctx=not latent_only),
        grid=(b, ROW_SPLIT),
        in_specs=[
            pl.BlockSpec((1, tm, bw), lambda bi, k: (bi, k, 0)),
            pl.BlockSpec((1, tm, bw), lambda bi, k: (bi, k, 0)),
            pl.BlockSpec((bw, d), lambda bi, k: (0, 0)),
            pl.BlockSpec((bw, d), lambda bi, k: (1, 0)),
            pl.BlockSpec((1, tm, d), lambda bi, k: (bi, k, 0)),
            pl.BlockSpec((1, 1, 3 * d), lambda bi, k: (bi, 0, 0)),
            pl.BlockSpec((1, 3 * d), lambda bi, k: (0, 0)),
            pl.BlockSpec((1, d), lambda bi, k: (0, 0)),
        ],
        out_specs=pl.BlockSpec((1, tm, d), lambda bi, k: (bi, k, 0)),
        out_shape=jax.ShapeDtypeStruct((b, n_lat if latent_only else r, d), F32),
        input_output_aliases={} if latent_only else {4: 0},
        compiler_params=_cparams(("arbitrary", "arbitrary")),
        name="outproj",
    )(ua, ub, w, w, xall, modb, modc, g.reshape(1, d))


def _dattn_lambda(lam_ref, lam_init):
    lp = lam_ref[...]
    return (jnp.exp(jnp.sum(lp[0:1] * lp[1:2], axis=-1, keepdims=True))
            - jnp.exp(jnp.sum(lp[2:3] * lp[3:4], axis=-1, keepdims=True)) + lam_init)


def _dattn_split_maps(q):
    first = (lax.broadcasted_iota(jnp.int32, (1, LANES), 1) & (DA_QK // 2)) == 0
    zero = jnp.zeros_like(q)
    return jnp.concatenate([jnp.where(first, q, zero), jnp.where(first, zero, q)], axis=0)


def _dattn_finish(o, sub_ref, gate, lam_init):
    o = o * lax.rsqrt(jnp.mean(o * o, axis=-1, keepdims=True) + EPS) * sub_ref[...] * (1.0 - lam_init)
    return (o * _silu(gate.astype(F32))).astype(BF16)


def _dattn_kernel(lam_ref, sub_ref, qn_ref, k_ref, vt_ref, g_ref, o_ref, sa_ref, ma_ref, sb_ref, mb_ref,
                  acc_ref, den_ref, *, n_blocks, lam_init):
    t = pl.program_id(0)
    lam = _dattn_lambda(lam_ref, lam_init)
    nt = (((1,), (1,)), ((), ()))
    tq = qn_ref.shape[1]

    def step(cur, prev, fin):
        if fin:
            o = acc_ref[...] * (1.0 / den_ref[...])
            o_ref[0] = _dattn_finish((o[:, :tq] - lam * o[:, tq:]).T, sub_ref, g_ref[0], lam_init)
        r = k_ref.shape[1]
        bounds = list(range(0, r - ATT_KCHUNK + 1, ATT_KCHUNK)) + [r]
        if cur is not None:
            q2 = _dattn_split_maps(qn_ref[0])
        m_cur = l_prev = o_prev = None
        for lo, hi in zip(bounds[:-1], bounds[1:]):
            if cur is not None:
                s = lax.dot_general(k_ref[0, lo:hi, :], q2, nt, preferred_element_type=F32)
                cur[0][lo:hi, :] = s
                m = jnp.max(s, axis=0, keepdims=True)
                m_cur = m if m_cur is None else jnp.maximum(m_cur, m)
            if prev is not None:
                e = jnp.exp2(prev[0][lo:hi, :] - prev[1][...])
                l = jnp.sum(e, axis=0, keepdims=True)
                o = jnp.dot(vt_ref[0, 0, :, lo:hi], e.astype(BF16), preferred_element_type=F32)
                l_prev = l if l_prev is None else l_prev + l
                o_prev = o if o_prev is None else o_prev + o
        if cur is not None:
            cur[1][...] = m_cur
        if prev is not None:
            acc_ref[...] = o_prev
            den_ref[...] = l_prev

    bufs = ((sa_ref, ma_ref), (sb_ref, mb_ref))
    last = bufs[(n_blocks - 1) % 2]
    pl.when(t == 0)(lambda: step(bufs[0], None, False))
    pl.when(t == 1)(lambda: step(bufs[1], bufs[0], False))
    pl.when((t >= 2) & (t < n_blocks) & (t % 2 == 0))(lambda: step(bufs[0], bufs[1], True))
    pl.when((t >= 2) & (t < n_blocks) & (t % 2 == 1))(lambda: step(bufs[1], bufs[0], True))
    pl.when(t == n_blocks)(lambda: step(None, last, True))
    pl.when(t == n_blocks + 1)(lambda: step(None, None, True))


def _dattn_ctx_kernel(lam_ref, sub_ref, q_ref, k_ref, v_ref, g_ref, ua_ref, o_ref, *, lam_init):
    del ua_ref
    lam = _dattn_lambda(lam_ref, lam_init)
    nt = (((1,), (1,)), ((), ()))
    n = q_ref.shape[1]
    for hd in range(q_ref.shape[2] // LANES):
        cols = slice(hd * LANES, (hd + 1) * LANES)
        s = lax.dot_general(_dattn_split_maps(q_ref[0, :, cols]), k_ref[0, :, cols], nt,
                            preferred_element_type=F32)
        e = jnp.exp2(s - jnp.max(s, axis=-1, keepdims=True))
        p = e / jnp.sum(e, axis=-1, keepdims=True)
        a = (p[:n] - lam * p[n:]).astype(BF16)
        o = jnp.dot(a, v_ref[0, :, cols], preferred_element_type=F32)
        o_ref[0, :, cols] = _dattn_finish(o, sub_ref, g_ref[0, :, cols], lam_init)


def _diff_attention(proj, vt, lam_p, subln, *, n_lat, lam_init):
    b, r, _ = proj.shape
    h = DA_HEADS
    tq = ATT_TQ
    nq = n_lat // tq
    n_ctx = r - n_lat
    sub = subln.reshape(1, LANES)
    small = [pl.BlockSpec(lam_p.shape, lambda *_: (0, 0)), pl.BlockSpec((1, LANES), lambda *_: (0, 0))]
    n_blocks = b * h * nq

    def unit(t, lag):
        u = jnp.clip(t - lag, 0, n_blocks - 1)
        return u // (h * nq), (u // nq) % h, u % nq

    def q_map(t):
        bi, hi, i = unit(t, 0)
        return bi, i, 2 * h + hi

    def k_map(t):
        bi, hi, _ = unit(t, 0)
        return bi, 0, hi

    def vt_map(t):
        bi, hi, _ = unit(t, 1)
        return bi, hi, 0, 0

    def g_map(t):
        bi, hi, i = unit(t, 2)
        return bi, i, 3 * h + hi

    def o_map(t):
        bi, hi, i = unit(t, 2)
        return bi, i, hi

    ua = pl.pallas_call(
        functools.partial(_dattn_kernel, n_blocks=n_blocks, lam_init=lam_init),
        grid=(n_blocks + 2,),
        in_specs=small + [
            pl.BlockSpec((1, tq, LANES), q_map),
            pl.BlockSpec((1, r, LANES), k_map),
            pl.BlockSpec((1, 1, LANES, r), vt_map),
            pl.BlockSpec((1, tq, LANES), g_map),
        ],
        out_specs=pl.BlockSpec((1, tq, LANES), o_map),
        out_shape=jax.ShapeDtypeStruct((b, r, h * LANES), BF16),
        scratch_shapes=[pltpu.VMEM((r, 2 * tq), F32), pltpu.VMEM((1, 2 * tq), F32)] * 2
        + [pltpu.VMEM((LANES, 2 * tq), F32), pltpu.VMEM((1, 2 * tq), F32)],
        compiler_params=_cparams(("arbitrary",)),
        name="diff_attention",
    )(lam_p, sub, proj, proj, vt, proj)
    cb = n_lat // n_ctx
    ctx_rows = lambda grp: pl.BlockSpec((1, n_ctx, h * LANES), lambda bi: (bi, cb, grp))
    return pl.pallas_call(
        functools.partial(_dattn_ctx_kernel, lam_init=lam_init),
        grid=(b,),
        in_specs=small + [ctx_rows(2), ctx_rows(0), ctx_rows(1), ctx_rows(3),
                          pl.BlockSpec(memory_space=pl.ANY)],
        out_specs=ctx_rows(0),
        out_shape=jax.ShapeDtypeStruct(ua.shape, ua.dtype),
        input_output_aliases={6: 0},
        compiler_params=_cparams(("arbitrary",)),
        name="diff_attention_ctx",
    )(lam_p, sub, proj, proj, proj, proj, ua)


def _sconv_kernel(h_ref, b_ref, c_ref, g_ref, w_ref, o_ref, *, n_lat):
    r = h_ref.shape[1]
    p = c_ref[0].astype(F32) * h_ref[0].astype(F32)
    row = lax.broadcasted_iota(jnp.int32, (r, 1), 0)
    prev = jnp.where((row == 0) | (row == n_lat), 0.0, pltpu.roll(p, 1, 0))
    nxt = jnp.where((row == n_lat - 1) | (row == r - 1), 0.0, pltpu.roll(p, r - 1, 0))
    w = w_ref[...]
    conv = w[0:1] * prev + w[1:2] * p + w[2:3] * nxt
    o_ref[0] = (b_ref[0].astype(F32) * conv * _silu(g_ref[0].astype(F32))).astype(o_ref.dtype)


def _short_conv(proj, conv_w, *, n_lat):
    b, r, _ = proj.shape
    nb = conv_w.shape[1] // LANES
    base = 4 * nb
    spec = lambda off: pl.BlockSpec((1, r, LANES), lambda bi, j: (bi, 0, off + j))
    return pl.pallas_call(
        functools.partial(_sconv_kernel, n_lat=n_lat),
        grid=(b, nb),
        in_specs=[spec(base), spec(base + nb), spec(base + 2 * nb), spec(base + 3 * nb),
                  pl.BlockSpec((SC_WIDTH, LANES), lambda bi, j: (0, j))],
        out_specs=pl.BlockSpec((1, r, LANES), lambda bi, j: (bi, 0, j)),
        out_shape=jax.ShapeDtypeStruct((b, r, nb * LANES), BF16),
        compiler_params=_cparams(("arbitrary", "arbitrary")),
        name="short_conv",
    )(proj, proj, proj, proj, conv_w)


def _natten_kernel(q_ref, k_ref, v_ref, g_ref, tb_ref, o_ref, *, n_lat, n_rows):
    lane = lax.broadcasted_iota(jnp.int32, (1, LANES), 1)
    first = lane < NA_DIM
    nt = (((1,), (1,)), ((), ()))
    win = WIN_ROWS * GRID_W
    kc = k_ref[0, n_lat:, :]
    vc = v_ref[0, n_lat:, :]

    def stack_heads(q):
        q = q * (NA_DIM ** -0.5)
        zero = jnp.zeros_like(q)
        return jnp.concatenate([jnp.where(first, q, zero), jnp.where(first, zero, q)], axis=0)

    def softmax(parts):
        m = functools.reduce(jnp.maximum, [jnp.max(s, axis=-1, keepdims=True) for s in parts])
        es = [jnp.exp2(s - m) for s in parts]
        l = functools.reduce(jnp.add, [jnp.sum(e, axis=-1, keepdims=True) for e in es])
        return [e.astype(BF16) for e in es], 1.0 / l

    def unstack_heads(o2):
        n = o2.shape[0] // 2
        return jnp.where(first, o2[:n], o2[n:])

    def group_body(gi, carry):
        rows = [gi * NA_GROUP + t for t in range(NA_GROUP)]
        starts = [jnp.clip(rr - WIN_ROWS // 2, 0, n_rows - WIN_ROWS) for rr in rows]
        qsl = [pl.ds(pl.multiple_of(rr * GRID_W, GRID_W), GRID_W) for rr in rows]
        ksl = [pl.ds(pl.multiple_of(rs * GRID_W, GRID_W), win) for rs in starts]
        scores = {}
        weights = {}
        for i in range(NA_GROUP + 2):
            if i < NA_GROUP:
                q2 = stack_heads(q_ref[0, qsl[i], :])
                s_loc = lax.dot_general(q2, k_ref[0, ksl[i], :], nt, preferred_element_type=F32)
                off = starts[i] - rows[i] + (WIN_ROWS - 1)
                s_loc = s_loc + jnp.concatenate([tb_ref[0, off + 2 * p] for p in range(WIN_ROWS // 2)], axis=1)
                scores[i] = [s_loc, lax.dot_general(q2, kc, nt, preferred_element_type=F32)]
            if 0 <= i - 1 < NA_GROUP:
                weights[i - 1] = softmax(scores.pop(i - 1))
            if 0 <= i - 2 < NA_GROUP:
                es, rl = weights.pop(i - 2)
                qs, ks = qsl[i - 2], ksl[i - 2]
                o2 = jnp.dot(es[0], v_ref[0, ks, :], preferred_element_type=F32)
                o2 = o2 + jnp.dot(es[1], vc, preferred_element_type=F32)
                out = unstack_heads(o2 * rl)
                o_ref[0, qs, :] = (out * _silu(g_ref[0, qs, :].astype(F32))).astype(o_ref.dtype)
        return carry

    lax.fori_loop(0, n_rows // NA_GROUP, group_body, 0)

    s = lax.dot_general(stack_heads(q_ref[0, n_lat:, :]), kc, nt, preferred_element_type=F32)
    es, rl = softmax([s])
    outc = unstack_heads(jnp.dot(es[0], vc, preferred_element_type=F32) * rl)
    o_ref[0, n_lat:, :] = (outc * _silu(g_ref[0, n_lat:, :].astype(F32))).astype(o_ref.dtype)


def _neighbourhood_attention(proj, tb, *, n_lat, cols):
    b, r, _ = proj.shape
    nb = NA_HEADS // 2
    kb, vb, qb, gb = (c // LANES for c in cols)
    spec = lambda off: pl.BlockSpec((1, r, LANES), lambda bi, j: (bi, 0, off + j))
    return pl.pallas_call(
        functools.partial(_natten_kernel, n_lat=n_lat, n_rows=n_lat // GRID_W),
        grid=(b, nb),
        in_specs=[spec(qb), spec(kb), spec(vb), spec(gb),
                  pl.BlockSpec((1,) + tb.shape[1:], lambda bi, j: (j, 0, 0, 0))],
        out_specs=pl.BlockSpec((1, r, LANES), lambda bi, j: (bi, 0, j)),
        out_shape=jax.ShapeDtypeStruct((b, r, nb * LANES), BF16),
        compiler_params=_cparams(("arbitrary", "arbitrary")),
        name="neighbourhood_attention",
    )(proj, proj, proj, proj, tb)


def _natten_bias_table(rpb):
    qcol = np.arange(GRID_W)[:, None]
    kcol = np.arange(GRID_W)[None, :]
    win_start = np.clip(qcol - WIN_COLS // 2, 0, GRID_W - WIN_COLS)
    valid = (kcol >= win_start) & (kcol < win_start + WIN_COLS)
    pad = GRID_W - WIN_COLS
    padded = jnp.pad(rpb.astype(F32), ((0, 0), (0, 0), (pad, pad)))
    toe = jnp.stack([padded[:, :, GRID_W - 1 - q:2 * GRID_W - 1 - q] for q in range(GRID_W)], axis=2)
    toe = jnp.where(jnp.asarray(valid)[None, None], toe * LOG2E, -jnp.inf)
    t = jnp.concatenate([toe[:, :-1], toe[:, 1:]], axis=-1)
    nh, n_o = t.shape[:2]
    t = jnp.transpose(t.reshape(nh // 2, 2, n_o, GRID_W, 2 * GRID_W), (0, 2, 1, 3, 4))
    return t.reshape(nh // 2, n_o, 2 * GRID_W, 2 * GRID_W)


def _gla_kernel(q_ref, k_ref, v_ref, g_ref, lr_ref, gu_ref, gb_ref, gn_ref, o_ref, of_scr, st_scr,
                *, n_lat, n_ctx, qscale):
    ch = GLA_CHUNK
    mid = ch // 2
    n_heads, dv, dk = st_scr.shape
    nt = (((1,), (1,)), ((), ()))
    tn = (((0,), (0,)), ((), ()))
    trow = lax.broadcasted_iota(jnp.int32, (ch, ch), 0)
    tcol = lax.broadcasted_iota(jnp.int32, (ch, ch), 1)
    srow = lax.broadcasted_iota(jnp.int32, (ch, 1), 0)

    def cumsum_rows(g, reverse):
        acc = g
        sh = 1
        while sh < ch:
            if reverse:
                acc = acc + jnp.where(srow < ch - sh, pltpu.roll(acc, ch - sh, 0), 0.0)
            else:
                acc = acc + jnp.where(srow >= sh, pltpu.roll(acc, sh, 0), 0.0)
            sh *= 2
        return acc

    def group(starts, dirn):
        reverse = dirn == 1
        keep = (tcol >= trow) if reverse else (tcol <= trow)
        rows_l = [pl.ds(s if isinstance(s, int) else pl.multiple_of(s, ch), ch) for s in starts]
        items = [(rows, hh, slice(hh * dk, (hh + 1) * dk), slice(hh * dv, (hh + 1) * dv))
                 for rows in rows_l for hh in range(n_heads)]
        zs = [jnp.dot(lr_ref[0, rows, :].astype(BF16), gu_ref[dirn, hh], preferred_element_type=F32)
              for rows, hh, _, _ in items]
        ops = []
        for (rows, hh, kc, _), z in zip(items, zs):
            g = jax.nn.log_sigmoid(z + gb_ref[dirn, hh]) * (1.0 / GLA_TAU)
            bcum = cumsum_rows(g, reverse)
            b_end = bcum[0:1] if reverse else bcum[ch - 1:ch]
            rho = bcum[mid:mid + 1]
            up = jnp.exp(bcum - rho)
            dn = jnp.exp(rho - bcum)
            qf = q_ref[0, rows, kc].astype(F32) * qscale
            kf = k_ref[0, rows, kc].astype(F32)
            ops.append(dict(
                q_state=(qf * (up * jnp.exp(rho))).astype(BF16),
                q_local=(qf * up).astype(BF16),
                k_local=(kf * dn).astype(BF16),
                k_state=(kf * (dn * jnp.exp(b_end - rho))).astype(BF16),
                decay=jnp.exp(b_end)))
        atts = [lax.dot_general(o["q_local"], o["k_local"], nt, preferred_element_type=F32) for o in ops]
        st_adds = [lax.dot_general(v_ref[0, rows, vc], o["k_state"], tn, preferred_element_type=F32)
                   for (rows, _, _, vc), o in zip(items, ops)]
        o_locals = [jnp.dot(jnp.where(keep, att, 0.0).astype(BF16), v_ref[0, rows, vc], preferred_element_type=F32)
                    for (rows, _, _, vc), att in zip(items, atts)]
        sts = [st_scr[hh] for hh in range(n_heads)]
        states = []
        for (_, hh, _, _), o, st_add in zip(items, ops, st_adds):
            states.append(sts[hh].astype(BF16))
            sts[hh] = sts[hh] * o["decay"] + st_add
        for hh in range(n_heads):
            st_scr[hh] = sts[hh]
        outs = [o_local + lax.dot_general(o["q_state"], s_in, nt, preferred_element_type=F32)
                for o, o_local, s_in in zip(ops, o_locals, states)]
        for (rows, hh, _, vc), o in zip(items, outs):
            if dirn == 0:
                of_scr[rows, vc] = o
            else:
                o = o + of_scr[rows, vc]
                o = o * lax.rsqrt(jnp.mean(o * o, axis=-1, keepdims=True) + EPS) * gn_ref[...]
                o_ref[0, rows, vc] = (o * _silu(g_ref[0, rows, vc].astype(F32))).astype(o_ref.dtype)

    n_cc = n_ctx // ch
    n_lg = n_lat // (ch * GLA_GROUP)
    gsz = ch * GLA_GROUP

    def latent_group(gi, dirn):
        if dirn == 0:
            group([gi * gsz + t * ch for t in range(GLA_GROUP)], 0)
        else:
            group([(n_lg - 1 - gi) * gsz + t * ch for t in range(GLA_GROUP - 1, -1, -1)], 1)

    st_scr[...] = jnp.zeros_like(st_scr)
    group([n_lat + c * ch for c in range(n_cc)], 0)
    lax.fori_loop(0, n_lg, lambda gi, carry: (latent_group(gi, 0), carry)[1], 0)
    st_scr[...] = jnp.zeros_like(st_scr)
    group([n_lat + c * ch for c in range(n_cc - 1, -1, -1)], 1)
    lax.fori_loop(0, n_lg, lambda gi, carry: (latent_group(gi, 1), carry)[1], 0)


def _gla(proj, lr, gu, gb, gnorm, *, n_lat, cols):
    b, r, _ = proj.shape
    nh = GLA_HEADS
    dk = LANES
    dv = 2 * LANES
    kcol, vcol, qcol, gcol = cols
    hs = GLA_HEADS_PER_STEP
    wk, wv = hs * dk, hs * dv
    return pl.pallas_call(
        functools.partial(_gla_kernel, n_lat=n_lat, n_ctx=r - n_lat, qscale=dk ** -0.5),
        grid=(b, nh // hs),
        in_specs=[
            pl.BlockSpec((1, r, wk), lambda bi, hi: (bi, 0, qcol // wk + hi)),
            pl.BlockSpec((1, r, wk), lambda bi, hi: (bi, 0, kcol // wk + hi)),
            pl.BlockSpec((1, r, wv), lambda bi, hi: (bi, 0, vcol // wv + hi)),
            pl.BlockSpec((1, r, wv), lambda bi, hi: (bi, 0, gcol // wv + hi)),
            pl.BlockSpec((1, r, LANES), lambda bi, hi: (bi, 0, 0)),
            pl.BlockSpec((2, hs, LANES, dk), lambda bi, hi: (0, hi, 0, 0)),
            pl.BlockSpec((2, hs, 1, dk), lambda bi, hi: (0, hi, 0, 0)),
            pl.BlockSpec((1, dv), lambda bi, hi: (0, 0)),
        ],
        out_specs=pl.BlockSpec((1, r, wv), lambda bi, hi: (bi, 0, hi)),
        out_shape=jax.ShapeDtypeStruct((b, r, nh * dv), BF16),
        scratch_shapes=[pltpu.VMEM((r, wv), F32), pltpu.VMEM((hs, dv, dk), F32)],
        compiler_params=_cparams(("arbitrary", "arbitrary")),
        name="gla",
    )(proj, proj, proj, proj, lr, gu, gb, gnorm.reshape(1, dv))


def _rope_tables(n_lat, n_ctx):
    half = DA_QK // 2
    inv = 1.0 / (ROPE_BASE ** (jnp.arange(0, half, 2, dtype=F32) / half))
    t = jnp.arange(n_lat)
    row = (t // GRID_W).astype(F32)[:, None] * inv
    col = (t % GRID_W).astype(F32)[:, None] * inv
    ang = jnp.concatenate([row, row, col, col], axis=-1)
    lane = np.arange(LANES)
    dim = 32 * ((lane >> 4) & 1) + 16 * (lane >> 6) + (lane & 15)
    sign = jnp.asarray(np.where((lane >> 6) == 0, -1.0, 1.0), F32)
    cos = jnp.concatenate([jnp.cos(ang)[:, dim], jnp.ones((n_ctx, LANES), F32)], axis=0)
    sin = jnp.concatenate([jnp.sin(ang)[:, dim] * sign, jnp.zeros((n_ctx, LANES), F32)], axis=0)
    qf = LOG2E * DA_QK ** -0.5
    return jnp.stack([cos, cos * qf]), jnp.stack([sin, sin * qf])


def _pair_layout(w):
    d = w.shape[0]
    w = w.reshape(d, -1, 2, 2, 2, DA_QK // 4)
    return jnp.transpose(w, (0, 1, 4, 2, 3, 5)).reshape(d, -1)


def kernel(x, c, ctx, c_ctx, w_mod, b_mod, g_pre, g_post, w_out, ev_w_in, ev_lambda, ev_subln, ev_conv,
           od_w_in, od_rpb, od_gate_up, od_gate_bias, od_gnorm):
    b, s, d = x.shape
    n_ctx = ctx.shape[1]
    depth = w_mod.shape[0]
    bw = EXPAND * d // 2
    assert bw == DA_HEADS * LANES == NA_HEADS * NA_DIM and (s + n_ctx) % (ROW_SPLIT * 16) == 0
    assert s % ATT_TQ == 0 and s % n_ctx == 0 and s % GRID_W == 0 and n_ctx % GLA_CHUNK == 0
    tm = (s + n_ctx) // ROW_SPLIT
    assert tm % ROPE_ROWS == 0 and (ROW_SPLIT - 1) * tm <= s and s % (ROW_SPLIT * 16) == 0

    xall = jnp.concatenate([x, ctx], axis=1)
    nrow = -(-(b + 1) // 8) * 8
    cc = jnp.zeros((nrow, d), F32).at[:b].set(c).at[b].set(c_ctx)
    mod = _modulation(cc, w_mod, b_mod)
    cos_t, sin_t = _rope_tables(s, n_ctx)

    gk = bw // 2
    od_lr0 = 3 * bw + gk
    od_cols = np.cumsum([0, bw, bw, gk, bw, bw, gk, bw])
    na_cols = (int(od_cols[0]), int(od_cols[1]), int(od_cols[4]), int(od_cols[6]))
    gla_cols = (int(od_cols[2]), int(od_cols[3]), int(od_cols[5]), int(od_cols[7]))

    for l in range(depth):
        j = l // 2
        modb = mod[l, :b].reshape(b, 1, 3 * d)
        modc = mod[l, b:b + 1]
        if l % 2 == 0:
            lam_init = 0.8 - 0.6 * math.exp(-0.3 * l)
            w = ev_w_in[j]
            w = jnp.concatenate([_pair_layout(w[:, :bw]), w[:, bw:2 * bw], _pair_layout(w[:, 2 * bw:3 * bw]),
                                 w[:, 3 * bw:]], axis=1).astype(BF16)
            proj, _ = _inproj(xall, modb, modc, g_pre[l], w, (cos_t, sin_t, 0, 2), None, n_lat=s, tn=bw)
            vt = jnp.swapaxes(proj[:, :, bw:2 * bw], 1, 2).reshape(b, DA_HEADS, LANES, s + n_ctx)
            ua = _diff_attention(proj, vt, ev_lambda[j], ev_subln[j], n_lat=s, lam_init=lam_init)
            ub = _short_conv(proj, ev_conv[j], n_lat=s)
        else:
            w = od_w_in[j]
            w_lr = jnp.pad(w[:, od_lr0:od_lr0 + 2 * GLA_RANK], ((0, 0), (0, LANES - 2 * GLA_RANK))).astype(BF16)
            w = jnp.concatenate([w[:, :od_lr0], w[:, od_lr0 + 2 * GLA_RANK:]], axis=1)
            col_scale = np.ones((1, w.shape[1]), np.float32)
            col_scale[:, na_cols[2]:na_cols[2] + bw] = LOG2E
            w = (w * col_scale).astype(BF16)
            proj, lr = _inproj(xall, modb, modc, g_pre[l], w, None, w_lr, n_lat=s, tn=bw)
            tb = _natten_bias_table(od_rpb[j])
            ua = _neighbourhood_attention(proj, tb, n_lat=s, cols=na_cols)
            gu = jnp.zeros((2, GLA_HEADS, LANES, LANES), F32)
            gup = od_gate_up[j].reshape(2, GLA_RANK, GLA_HEADS, LANES).transpose(0, 2, 1, 3)
            gu = gu.at[0, :, :GLA_RANK].set(gup[0]).at[1, :, GLA_RANK:2 * GLA_RANK].set(gup[1]).astype(BF16)
            gb = od_gate_bias[j].reshape(2, GLA_HEADS, 1, LANES)
            ub = _gla(proj, lr, gu, gb, od_gnorm[j], n_lat=s, cols=gla_cols)
        xall = _outproj(ua, ub, w_out[l].astype(BF16), xall, modb, modc, g_post[l], n_lat=s,
                        latent_only=l == depth - 1)
    return xall
```

```python
import functools
import math

import numpy as np
import jax
import jax.numpy as jnp
from jax import lax
from jax.experimental import pallas as pl
from jax.experimental.pallas import tpu as pltpu

F32 = jnp.float32
BF16 = jnp.bfloat16

LANES = 128
GRID_W = 64
EXPAND = 2
DA_HEADS = 8
DA_QK = 64
SC_WIDTH = 3
NA_HEADS = 16
NA_DIM = 64
WIN_ROWS = 8
WIN_COLS = 16
GLA_HEADS = 4
GLA_RANK = 16
GLA_TAU = 16.0
GLA_CHUNK = 64
ROPE_BASE = 10000.0
EPS = 1e-6
LOG2E = math.log2(math.e)
ROW_SPLIT = 4
ROPE_ROWS = 64
ATT_TQ = 512
ATT_KCHUNK = 256
NA_GROUP = 8
GLA_GROUP = 8
GLA_HEADS_PER_STEP = 2
VMEM_LIMIT = 56 * 1024 * 1024


def _cparams(sem):
    return pltpu.CompilerParams(dimension_semantics=sem, vmem_limit_bytes=VMEM_LIMIT)


def _silu(t):
    return t * jax.nn.sigmoid(t)


def _mod_kernel(c_ref, w_ref, b_ref, o_ref):
    s = _silu(c_ref[...])
    o_ref[0] = jnp.dot(s, w_ref[0], preferred_element_type=F32) + b_ref[0]


def _modulation(cc, w_mod, b_mod):
    depth, d, d3 = w_mod.shape
    nrow = cc.shape[0]
    tn = d
    return pl.pallas_call(
        _mod_kernel,
        grid=(depth, d3 // tn),
        in_specs=[
            pl.BlockSpec((nrow, d), lambda l, j: (0, 0)),
            pl.BlockSpec((1, d, tn), lambda l, j: (l, 0, j)),
            pl.BlockSpec((1, 1, tn), lambda l, j: (l, 0, j)),
        ],
        out_specs=pl.BlockSpec((1, nrow, tn), lambda l, j: (l, 0, j)),
        out_shape=jax.ShapeDtypeStruct((depth, nrow, d3), F32),
        compiler_params=_cparams(("arbitrary", "arbitrary")),
        name="modulation",
    )(cc, w_mod, b_mod.reshape(depth, 1, d3))


def _inproj_kernel(*refs, tm, d, n_lat, rope_tiles, has_lr):
    x_ref, mb_ref, mc_ref, g_ref, w_ref = refs[:5]
    rest = list(refs[5:])
    cos_ref, sin_ref = (rest.pop(0), rest.pop(0)) if rope_tiles else (None, None)
    wlr_ref = rest.pop(0) if has_lr else None
    o_ref = rest.pop(0)
    lr_ref = rest.pop(0) if has_lr else None
    h_scr = rest.pop(0)
    acc_scr = rest.pop(0) if rope_tiles else None
    k = pl.program_id(1)
    j = pl.program_id(2)
    n_blk = pl.num_programs(1)

    @pl.when(j == 0)
    def _():
        xv = x_ref[0]
        xn = xv * lax.rsqrt(jnp.mean(xv * xv, axis=-1, keepdims=True) + EPS)
        mb = mb_ref[0]
        gain_b = g_ref[...] * (1.0 + mb[:, d:2 * d])
        shift_b = mb[:, 0:d]

        def emit(gain, shift):
            h_scr[...] = (xn * gain + shift).astype(BF16)
            if has_lr:
                lr_ref[0] = jnp.dot(h_scr[...], wlr_ref[...], preferred_element_type=F32)

        pl.when(k < n_blk - 1)(lambda: emit(gain_b, shift_b))

        @pl.when(k == n_blk - 1)
        def _():
            is_ctx = k * tm + lax.broadcasted_iota(jnp.int32, (tm, 1), 0) >= n_lat
            gain_c = g_ref[...] * (1.0 + mc_ref[:, d:2 * d])
            emit(jnp.where(is_ctx, gain_c, gain_b), jnp.where(is_ctx, mc_ref[:, 0:d], shift_b))

    tn = o_ref.shape[2]

    def project():
        w_tile = w_ref[:, pl.ds(pl.multiple_of(j * tn, tn), tn)]
        return jnp.dot(h_scr[...], w_tile, preferred_element_type=F32)

    if rope_tiles:
        is_rope = functools.reduce(jnp.logical_or, [j == t for t in rope_tiles])

        @pl.when(is_rope)
        def _():
            acc_scr[...] = project()

            def rows_body(c, carry):
                rows = pl.ds(pl.multiple_of(c * ROPE_ROWS, ROPE_ROWS), ROPE_ROWS)
                cos = cos_ref[0, rows, :]
                sin = sin_ref[0, rows, :]
                for hd in range(tn // LANES):
                    a = acc_scr[rows, hd * LANES:(hd + 1) * LANES]
                    o_ref[0, rows, hd * LANES:(hd + 1) * LANES] = (
                        a * cos + pltpu.roll(a, LANES // 2, 1) * sin).astype(o_ref.dtype)
                return carry

            lax.fori_loop(0, tm // ROPE_ROWS, rows_body, 0, unroll=True)

        @pl.when(jnp.logical_not(is_rope))
        def _():
            o_ref[0] = project().astype(o_ref.dtype)
    else:
        o_ref[0] = project().astype(o_ref.dtype)


def _inproj(xall, modb, modc, g, w, rope, w_lr, *, n_lat, tn):
    b, r, d = xall.shape
    nc = w.shape[1]
    tm = r // ROW_SPLIT
    has_lr = w_lr is not None
    in_specs = [
        pl.BlockSpec((1, tm, d), lambda bi, k, j: (bi, k, 0)),
        pl.BlockSpec((1, 1, 3 * d), lambda bi, k, j: (bi, 0, 0)),
        pl.BlockSpec((1, 3 * d), lambda bi, k, j: (0, 0)),
        pl.BlockSpec((1, d), lambda bi, k, j: (0, 0)),
        pl.BlockSpec((d, nc), lambda bi, k, j: (0, 0), pipeline_mode=pl.Buffered(1)),
    ]
    args = [xall, modb, modc, g.reshape(1, d), w]
    rope_tiles = ()
    if rope is not None:
        cos_t, sin_t, k_tile, q_tile = rope
        rope_tiles = (k_tile, q_tile)
        table = pl.BlockSpec((1, tm, LANES), lambda bi, k, j: (jnp.where(j == q_tile, 1, 0), k, 0))
        in_specs += [table, table]
        args += [cos_t, sin_t]
    out_specs = [pl.BlockSpec((1, tm, tn), lambda bi, k, j: (bi, k, j))]
    out_shape = [jax.ShapeDtypeStruct((b, r, nc), BF16)]
    if has_lr:
        in_specs.append(pl.BlockSpec((d, LANES), lambda bi, k, j: (0, 0)))
        args.append(w_lr)
        out_specs.append(pl.BlockSpec((1, tm, LANES), lambda bi, k, j: (bi, k, 0)))
        out_shape.append(jax.ShapeDtypeStruct((b, r, LANES), F32))
    res = pl.pallas_call(
        functools.partial(_inproj_kernel, tm=tm, d=d, n_lat=n_lat, rope_tiles=rope_tiles, has_lr=has_lr),
        grid=(b, ROW_SPLIT, nc // tn),
        in_specs=in_specs,
        out_specs=out_specs,
        out_shape=out_shape,
        scratch_shapes=[pltpu.VMEM((tm, d), BF16)] + ([pltpu.VMEM((tm, tn), F32)] if rope_tiles else []),
        compiler_params=_cparams(("arbitrary", "arbitrary", "arbitrary")),
        name="inproj",
    )(*args)
    return res if has_lr else (res[0], None)


def _outproj_kernel(ua_ref, ub_ref, w1_ref, w2_ref, x_ref, mb_ref, mc_ref, g_ref, o_ref, *, tm, d, n_lat,
                    has_ctx):
    k = pl.program_id(1)
    y = jnp.dot(ua_ref[0], w1_ref[...], preferred_element_type=F32)
    y = y + jnp.dot(ub_ref[0], w2_ref[...], preferred_element_type=F32)
    y = y * lax.rsqrt(jnp.mean(y * y, axis=-1, keepdims=True) + EPS)
    gate_b = g_ref[...] * mb_ref[0][:, 2 * d:3 * d]
    if not has_ctx:
        o_ref[0] = x_ref[0] + gate_b * y
        return

    n_blk = pl.num_programs(1)

    @pl.when(k < n_blk - 1)
    def _():
        o_ref[0] = x_ref[0] + gate_b * y

    @pl.when(k == n_blk - 1)
    def _():
        is_ctx = k * tm + lax.broadcasted_iota(jnp.int32, (tm, 1), 0) >= n_lat
        gate_c = g_ref[...] * mc_ref[:, 2 * d:3 * d]
        o_ref[0] = x_ref[0] + jnp.where(is_ctx, gate_c, gate_b) * y


def _outproj(ua, ub, w, xall, modb, modc, g, *, n_lat, latent_only):
    b, r, d = xall.shape
    bw = ua.shape[2]
    tm = n_lat // ROW_SPLIT if latent_only else r // ROW_SPLIT
    return pl.pallas_call(
        functools.partial(_outproj_kernel, tm=tm, d=d, n_lat=n_lat, has_ctx=not latent_only),
        grid=(b, ROW_SPLIT),
        in_specs=[
            pl.BlockSpec((1, tm, bw), lambda bi, k: (bi, k, 0)),
            pl.BlockSpec((1, tm, bw), lambda bi, k: (bi, k, 0)),
            pl.BlockSpec((bw, d), lambda bi, k: (0, 0)),
            pl.BlockSpec((bw, d), lambda bi, k: (1, 0)),
            pl.BlockSpec((1, tm, d), lambda bi, k: (bi, k, 0)),
            pl.BlockSpec((1, 1, 3 * d), lambda bi, k: (bi, 0, 0)),
            pl.BlockSpec((1, 3 * d), lambda bi, k: (0, 0)),
            pl.BlockSpec((1, d), lambda bi, k: (0, 0)),
        ],
        out_specs=pl.BlockSpec((1, tm, d), lambda bi, k: (bi, k, 0)),
        out_shape=jax.ShapeDtypeStruct((b, n_lat if latent_only else r, d), F32),
        input_output_aliases={} if latent_only else {4: 0},
        compiler_params=_cparams(("arbitrary", "arbitrary")),
        name="outproj",
    )(ua, ub, w, w, xall, modb, modc, g.reshape(1, d))


def _dattn_lambda(lam_ref, lam_init):
    lp = lam_ref[...]
    return (jnp.exp(jnp.sum(lp[0:1] * lp[1:2], axis=-1, keepdims=True))
            - jnp.exp(jnp.sum(lp[2:3] * lp[3:4], axis=-1, keepdims=True)) + lam_init)


def _dattn_split_maps(q):
    first = (lax.broadcasted_iota(jnp.int32, (1, LANES), 1) & (DA_QK // 2)) == 0
    zero = jnp.zeros_like(q)
    return jnp.concatenate([jnp.where(first, q, zero), jnp.where(first, zero, q)], axis=0)


def _dattn_finish(o, sub_ref, gate, lam_init):
    o = o * lax.rsqrt(jnp.mean(o * o, axis=-1, keepdims=True) + EPS) * sub_ref[...] * (1.0 - lam_init)
    return (o * _silu(gate.astype(F32))).astype(BF16)


def _dattn_kernel(lam_ref, sub_ref, qn_ref, k_ref, vt_ref, g_ref, ua_ref, o_ref, sa_ref, ma_ref, sb_ref, mb_ref,
                  acc_ref, den_ref, *, n_blocks, lam_init):
    del ua_ref
    t = pl.program_id(0)
    lam = _dattn_lambda(lam_ref, lam_init)
    nt = (((1,), (1,)), ((), ()))
    tq = qn_ref.shape[1]

    def step(cur, prev, fin):
        if fin:
            o = acc_ref[...] * (1.0 / den_ref[...])
            o_ref[0] = _dattn_finish((o[:, :tq] - lam * o[:, tq:]).T, sub_ref, g_ref[0], lam_init)
        r = k_ref.shape[1]
        bounds = list(range(0, r - ATT_KCHUNK + 1, ATT_KCHUNK)) + [r]
        if cur is not None:
            q2 = _dattn_split_maps(qn_ref[0])
        m_cur = l_prev = o_prev = None
        for lo, hi in zip(bounds[:-1], bounds[1:]):
            if cur is not None:
                s = lax.dot_general(k_ref[0, lo:hi, :], q2, nt, preferred_element_type=F32)
                cur[0][lo:hi, :] = s
                m = jnp.max(s, axis=0, keepdims=True)
                m_cur = m if m_cur is None else jnp.maximum(m_cur, m)
            if prev is not None:
                e = jnp.exp2(prev[0][lo:hi, :] - prev[1][...])
                l = jnp.sum(e, axis=0, keepdims=True)
                o = jnp.dot(vt_ref[0, 0, :, lo:hi], e.astype(BF16), preferred_element_type=F32)
                l_prev = l if l_prev is None else l_prev + l
                o_prev = o if o_prev is None else o_prev + o
        if cur is not None:
            cur[1][...] = m_cur
        if prev is not None:
            acc_ref[...] = o_prev
            den_ref[...] = l_prev

    bufs = ((sa_ref, ma_ref), (sb_ref, mb_ref))
    last = bufs[(n_blocks - 1) % 2]
    pl.when(t == 0)(lambda: step(bufs[0], None, False))
    pl.when(t == 1)(lambda: step(bufs[1], bufs[0], False))
    pl.when((t >= 2) & (t < n_blocks) & (t % 2 == 0))(lambda: step(bufs[0], bufs[1], True))
    pl.when((t >= 2) & (t < n_blocks) & (t % 2 == 1))(lambda: step(bufs[1], bufs[0], True))
    pl.when(t == n_blocks)(lambda: step(None, last, True))
    pl.when(t == n_blocks + 1)(lambda: step(None, None, True))


def _dattn_ctx_kernel(lam_ref, sub_ref, q_ref, k_ref, v_ref, g_ref, ua_ref, o_ref, *, lam_init):
    del ua_ref
    lam = _dattn_lambda(lam_ref, lam_init)
    nt = (((1,), (1,)), ((), ()))
    n = q_ref.shape[1]
    for hd in range(q_ref.shape[2] // LANES):
        cols = slice(hd * LANES, (hd + 1) * LANES)
        s = lax.dot_general(_dattn_split_maps(q_ref[0, :, cols]), k_ref[0, :, cols], nt,
                            preferred_element_type=F32)
        e = jnp.exp2(s - jnp.max(s, axis=-1, keepdims=True))
        p = e / jnp.sum(e, axis=-1, keepdims=True)
        a = (p[:n] - lam * p[n:]).astype(BF16)
        o = jnp.dot(a, v_ref[0, :, cols], preferred_element_type=F32)
        o_ref[0, :, cols] = _dattn_finish(o, sub_ref, g_ref[0, :, cols], lam_init)


def _diff_attention(proj, vt, lam_p, subln, *, n_lat, lam_init):
    b, r, _ = proj.shape
    h = DA_HEADS
    tq = ATT_TQ
    nq = n_lat // tq
    n_ctx = r - n_lat
    sub = subln.reshape(1, LANES)
    small = [pl.BlockSpec(lam_p.shape, lambda *_: (0, 0)), pl.BlockSpec((1, LANES), lambda *_: (0, 0))]
    n_blocks = b * h * nq

    def unit(t, lag):
        u = jnp.clip(t - lag, 0, n_blocks - 1)
        return u // (h * nq), (u // nq) % h, u % nq

    def q_map(t):
        bi, hi, i = unit(t, 0)
        return bi, i, 2 * h + hi

    def k_map(t):
        bi, hi, _ = unit(t, 0)
        return bi, 0, hi

    def vt_map(t):
        bi, hi, _ = unit(t, 1)
        return bi, hi, 0, 0

    def g_map(t):
        bi, hi, i = unit(t, 2)
        return bi, i, 3 * h + hi

    def o_map(t):
        bi, hi, i = unit(t, 2)
        return bi, i, hi

    ua = pl.pallas_call(
        functools.partial(_dattn_kernel, n_blocks=n_blocks, lam_init=lam_init),
        grid=(n_blocks + 2,),
        in_specs=small + [
            pl.BlockSpec((1, tq, LANES), q_map),
            pl.BlockSpec((1, r, LANES), k_map),
            pl.BlockSpec((1, 1, LANES, r), vt_map),
            pl.BlockSpec((1, tq, LANES), g_map),
            pl.BlockSpec(memory_space=pl.ANY),
        ],
        out_specs=pl.BlockSpec((1, tq, LANES), o_map),
        out_shape=jax.ShapeDtypeStruct((b, r, h * LANES), BF16),
        scratch_shapes=[pltpu.VMEM((r, 2 * tq), F32), pltpu.VMEM((1, 2 * tq), F32)] * 2
        + [pltpu.VMEM((LANES, 2 * tq), F32), pltpu.VMEM((1, 2 * tq), F32)],
        input_output_aliases={6: 0},
        compiler_params=_cparams(("arbitrary",)),
        name="diff_attention",
    )(lam_p, sub, proj, proj, vt, proj, jnp.zeros((b, r, h * LANES), BF16))
    cb = n_lat // n_ctx
    ctx_rows = lambda grp: pl.BlockSpec((1, n_ctx, h * LANES), lambda bi: (bi, cb, grp))
    return pl.pallas_call(
        functools.partial(_dattn_ctx_kernel, lam_init=lam_init),
        grid=(b,),
        in_specs=small + [ctx_rows(2), ctx_rows(0), ctx_rows(1), ctx_rows(3),
                          pl.BlockSpec(memory_space=pl.ANY)],
        out_specs=ctx_rows(0),
        out_shape=jax.ShapeDtypeStruct(ua.shape, ua.dtype),
        input_output_aliases={6: 0},
        compiler_params=_cparams(("arbitrary",)),
        name="diff_attention_ctx",
    )(lam_p, sub, proj, proj, proj, proj, ua)


def _sconv_kernel(h_ref, b_ref, c_ref, g_ref, w_ref, o_ref, *, n_lat):
    r = h_ref.shape[1]
    p = c_ref[0].astype(F32) * h_ref[0].astype(F32)
    row = lax.broadcasted_iota(jnp.int32, (r, 1), 0)
    prev = jnp.where((row == 0) | (row == n_lat), 0.0, pltpu.roll(p, 1, 0))
    nxt = jnp.where((row == n_lat - 1) | (row == r - 1), 0.0, pltpu.roll(p, r - 1, 0))
    w = w_ref[...]
    conv = w[0:1] * prev + w[1:2] * p + w[2:3] * nxt
    o_ref[0] = (b_ref[0].astype(F32) * conv * _silu(g_ref[0].astype(F32))).astype(o_ref.dtype)


def _short_conv(proj, conv_w, *, n_lat):
    b, r, _ = proj.shape
    nb = conv_w.shape[1] // LANES
    base = 4 * nb
    spec = lambda off: pl.BlockSpec((1, r, LANES), lambda bi, j: (bi, 0, off + j))
    return pl.pallas_call(
        functools.partial(_sconv_kernel, n_lat=n_lat),
        grid=(b, nb),
        in_specs=[spec(base), spec(base + nb), spec(base + 2 * nb), spec(base + 3 * nb),
                  pl.BlockSpec((SC_WIDTH, LANES), lambda bi, j: (0, j))],
        out_specs=pl.BlockSpec((1, r, LANES), lambda bi, j: (bi, 0, j)),
        out_shape=jax.ShapeDtypeStruct((b, r, nb * LANES), BF16),
        compiler_params=_cparams(("arbitrary", "arbitrary")),
        name="short_conv",
    )(proj, proj, proj, proj, conv_w)


def _natten_kernel(q_ref, k_ref, v_ref, g_ref, tb_ref, o_ref, *, n_lat, n_rows):
    lane = lax.broadcasted_iota(jnp.int32, (1, LANES), 1)
    first = lane < NA_DIM
    nt = (((1,), (1,)), ((), ()))
    win = WIN_ROWS * GRID_W
    kc = k_ref[0, n_lat:, :]
    vc = v_ref[0, n_lat:, :]

    def stack_heads(q):
        q = q * (NA_DIM ** -0.5)
        zero = jnp.zeros_like(q)
        return jnp.concatenate([jnp.where(first, q, zero), jnp.where(first, zero, q)], axis=0)

    def softmax(parts):
        m = functools.reduce(jnp.maximum, [jnp.max(s, axis=-1, keepdims=True) for s in parts])
        es = [jnp.exp2(s - m) for s in parts]
        l = functools.reduce(jnp.add, [jnp.sum(e, axis=-1, keepdims=True) for e in es])
        return [e.astype(BF16) for e in es], 1.0 / l

    def unstack_heads(o2):
        n = o2.shape[0] // 2
        return jnp.where(first, o2[:n], o2[n:])

    def group_body(gi, carry):
        rows = [gi * NA_GROUP + t for t in range(NA_GROUP)]
        starts = [jnp.clip(rr - WIN_ROWS // 2, 0, n_rows - WIN_ROWS) for rr in rows]
        qsl = [pl.ds(pl.multiple_of(rr * GRID_W, GRID_W), GRID_W) for rr in rows]
        ksl = [pl.ds(pl.multiple_of(rs * GRID_W, GRID_W), win) for rs in starts]
        scores = {}
        weights = {}
        for i in range(NA_GROUP + 2):
            if i < NA_GROUP:
                q2 = stack_heads(q_ref[0, qsl[i], :])
                s_loc = lax.dot_general(q2, k_ref[0, ksl[i], :], nt, preferred_element_type=F32)
                off = starts[i] - rows[i] + (WIN_ROWS - 1)
                s_loc = s_loc + jnp.concatenate([tb_ref[0, off + 2 * p] for p in range(WIN_ROWS // 2)], axis=1)
                scores[i] = [s_loc, lax.dot_general(q2, kc, nt, preferred_element_type=F32)]
            if 0 <= i - 1 < NA_GROUP:
                weights[i - 1] = softmax(scores.pop(i - 1))
            if 0 <= i - 2 < NA_GROUP:
                es, rl = weights.pop(i - 2)
                qs, ks = qsl[i - 2], ksl[i - 2]
                o2 = jnp.dot(es[0], v_ref[0, ks, :], preferred_element_type=F32)
                o2 = o2 + jnp.dot(es[1], vc, preferred_element_type=F32)
                out = unstack_heads(o2 * rl)
                o_ref[0, qs, :] = (out * _silu(g_ref[0, qs, :].astype(F32))).astype(o_ref.dtype)
        return carry

    lax.fori_loop(0, n_rows // NA_GROUP, group_body, 0)

    s = lax.dot_general(stack_heads(q_ref[0, n_lat:, :]), kc, nt, preferred_element_type=F32)
    es, rl = softmax([s])
    outc = unstack_heads(jnp.dot(es[0], vc, preferred_element_type=F32) * rl)
    o_ref[0, n_lat:, :] = (outc * _silu(g_ref[0, n_lat:, :].astype(F32))).astype(o_ref.dtype)


def _neighbourhood_attention(proj, tb, *, n_lat, cols):
    b, r, _ = proj.shape
    nb = NA_HEADS // 2
    kb, vb, qb, gb = (c // LANES for c in cols)
    spec = lambda off: pl.BlockSpec((1, r, LANES), lambda bi, j: (bi, 0, off + j))
    return pl.pallas_call(
        functools.partial(_natten_kernel, n_lat=n_lat, n_rows=n_lat // GRID_W),
        grid=(b, nb),
        in_specs=[spec(qb), spec(kb), spec(vb), spec(gb),
                  pl.BlockSpec((1,) + tb.shape[1:], lambda bi, j: (j, 0, 0, 0))],
        out_specs=pl.BlockSpec((1, r, LANES), lambda bi, j: (bi, 0, j)),
        out_shape=jax.ShapeDtypeStruct((b, r, nb * LANES), BF16),
        compiler_params=_cparams(("arbitrary", "arbitrary")),
        name="neighbourhood_attention",
    )(proj, proj, proj, proj, tb)


def _natten_bias_table(rpb):
    qcol = np.arange(GRID_W)[:, None]
    kcol = np.arange(GRID_W)[None, :]
    win_start = np.clip(qcol - WIN_COLS // 2, 0, GRID_W - WIN_COLS)
    valid = (kcol >= win_start) & (kcol < win_start + WIN_COLS)
    pad = GRID_W - WIN_COLS
    padded = jnp.pad(rpb.astype(F32), ((0, 0), (0, 0), (pad, pad)))
    toe = jnp.stack([padded[:, :, GRID_W - 1 - q:2 * GRID_W - 1 - q] for q in range(GRID_W)], axis=2)
    toe = jnp.where(jnp.asarray(valid)[None, None], toe * LOG2E, -jnp.inf)
    t = jnp.concatenate([toe[:, :-1], toe[:, 1:]], axis=-1)
    nh, n_o = t.shape[:2]
    t = jnp.transpose(t.reshape(nh // 2, 2, n_o, GRID_W, 2 * GRID_W), (0, 2, 1, 3, 4))
    return t.reshape(nh // 2, n_o, 2 * GRID_W, 2 * GRID_W)


def _gla_kernel(q_ref, k_ref, v_ref, g_ref, lr_ref, gu_ref, gb_ref, gn_ref, o_ref, of_scr, st_scr,
                *, n_lat, n_ctx, qscale):
    ch = GLA_CHUNK
    mid = ch // 2
    n_heads, dv, dk = st_scr.shape
    nt = (((1,), (1,)), ((), ()))
    tn = (((0,), (0,)), ((), ()))
    trow = lax.broadcasted_iota(jnp.int32, (ch, ch), 0)
    tcol = lax.broadcasted_iota(jnp.int32, (ch, ch), 1)
    srow = lax.broadcasted_iota(jnp.int32, (ch, 1), 0)

    def cumsum_rows(g, reverse):
        acc = g
        sh = 1
        while sh < ch:
            if reverse:
                acc = acc + jnp.where(srow < ch - sh, pltpu.roll(acc, ch - sh, 0), 0.0)
            else:
                acc = acc + jnp.where(srow >= sh, pltpu.roll(acc, sh, 0), 0.0)
            sh *= 2
        return acc

    def group(starts, dirn):
        reverse = dirn == 1
        keep = (tcol >= trow) if reverse else (tcol <= trow)
        rows_l = [pl.ds(s if isinstance(s, int) else pl.multiple_of(s, ch), ch) for s in starts]
        items = [(rows, hh, slice(hh * dk, (hh + 1) * dk), slice(hh * dv, (hh + 1) * dv))
                 for rows in rows_l for hh in range(n_heads)]
        zs = [jnp.dot(lr_ref[0, rows, :].astype(BF16), gu_ref[dirn, hh], preferred_element_type=F32)
              for rows, hh, _, _ in items]
        ops = []
        for (rows, hh, kc, _), z in zip(items, zs):
            g = jax.nn.log_sigmoid(z + gb_ref[dirn, hh]) * (1.0 / GLA_TAU)
            bcum = cumsum_rows(g, reverse)
            b_end = bcum[0:1] if reverse else bcum[ch - 1:ch]
            rho = bcum[mid:mid + 1]
            up = jnp.exp(bcum - rho)
            dn = jnp.exp(rho - bcum)
            qf = q_ref[0, rows, kc].astype(F32) * qscale
            kf = k_ref[0, rows, kc].astype(F32)
            ops.append(dict(
                q_state=(qf * (up * jnp.exp(rho))).astype(BF16),
                q_local=(qf * up).astype(BF16),
                k_local=(kf * dn).astype(BF16),
                k_state=(kf * (dn * jnp.exp(b_end - rho))).astype(BF16),
                decay=jnp.exp(b_end)))
        atts = [lax.dot_general(o["q_local"], o["k_local"], nt, preferred_element_type=F32) for o in ops]
        st_adds = [lax.dot_general(v_ref[0, rows, vc], o["k_state"], tn, preferred_element_type=F32)
                   for (rows, _, _, vc), o in zip(items, ops)]
        o_locals = [jnp.dot(jnp.where(keep, att, 0.0).astype(BF16), v_ref[0, rows, vc], preferred_element_type=F32)
                    for (rows, _, _, vc), att in zip(items, atts)]
        sts = [st_scr[hh] for hh in range(n_heads)]
        states = []
        for (_, hh, _, _), o, st_add in zip(items, ops, st_adds):
            states.append(sts[hh].astype(BF16))
            sts[hh] = sts[hh] * o["decay"] + st_add
        for hh in range(n_heads):
            st_scr[hh] = sts[hh]
        outs = [o_local + lax.dot_general(o["q_state"], s_in, nt, preferred_element_type=F32)
                for o, o_local, s_in in zip(ops, o_locals, states)]
        for (rows, hh, _, vc), o in zip(items, outs):
            if dirn == 0:
                of_scr[rows, vc] = o
            else:
                o = o + of_scr[rows, vc]
                o = o * lax.rsqrt(jnp.mean(o * o, axis=-1, keepdims=True) + EPS) * gn_ref[...]
                o_ref[0, rows, vc] = (o * _silu(g_ref[0, rows, vc].astype(F32))).astype(o_ref.dtype)

    n_cc = n_ctx // ch
    n_lg = n_lat // (ch * GLA_GROUP)
    gsz = ch * GLA_GROUP

    def latent_group(gi, dirn):
        if dirn == 0:
            group([gi * gsz + t * ch for t in range(GLA_GROUP)], 0)
        else:
            group([(n_lg - 1 - gi) * gsz + t * ch for t in range(GLA_GROUP - 1, -1, -1)], 1)

    st_scr[...] = jnp.zeros_like(st_scr)
    group([n_lat + c * ch for c in range(n_cc)], 0)
    lax.fori_loop(0, n_lg, lambda gi, carry: (latent_group(gi, 0), carry)[1], 0)
    st_scr[...] = jnp.zeros_like(st_scr)
    group([n_lat + c * ch for c in range(n_cc - 1, -1, -1)], 1)
    lax.fori_loop(0, n_lg, lambda gi, carry: (latent_group(gi, 1), carry)[1], 0)


def _gla(proj, lr, gu, gb, gnorm, *, n_lat, cols):
    b, r, _ = proj.shape
    nh = GLA_HEADS
    dk = LANES
    dv = 2 * LANES
    kcol, vcol, qcol, gcol = cols
    hs = GLA_HEADS_PER_STEP
    wk, wv = hs * dk, hs * dv
    return pl.pallas_call(
        functools.partial(_gla_kernel, n_lat=n_lat, n_ctx=r - n_lat, qscale=dk ** -0.5),
        grid=(b, nh // hs),
        in_specs=[
            pl.BlockSpec((1, r, wk), lambda bi, hi: (bi, 0, qcol // wk + hi)),
            pl.BlockSpec((1, r, wk), lambda bi, hi: (bi, 0, kcol // wk + hi)),
            pl.BlockSpec((1, r, wv), lambda bi, hi: (bi, 0, vcol // wv + hi)),
            pl.BlockSpec((1, r, wv), lambda bi, hi: (bi, 0, gcol // wv + hi)),
            pl.BlockSpec((1, r, LANES), lambda bi, hi: (bi, 0, 0)),
            pl.BlockSpec((2, hs, LANES, dk), lambda bi, hi: (0, hi, 0, 0)),
            pl.BlockSpec((2, hs, 1, dk), lambda bi, hi: (0, hi, 0, 0)),
            pl.BlockSpec((1, dv), lambda bi, hi: (0, 0)),
        ],
        out_specs=pl.BlockSpec((1, r, wv), lambda bi, hi: (bi, 0, hi)),
        out_shape=jax.ShapeDtypeStruct((b, r, nh * dv), BF16),
        scratch_shapes=[pltpu.VMEM((r, wv), F32), pltpu.VMEM((hs, dv, dk), F32)],
        compiler_params=_cparams(("arbitrary", "arbitrary")),
        name="gla",
    )(proj, proj, proj, proj, lr, gu, gb, gnorm.reshape(1, dv))


def _rope_tables(n_lat, n_ctx):
    half = DA_QK // 2
    inv = 1.0 / (ROPE_BASE ** (jnp.arange(0, half, 2, dtype=F32) / half))
    t = jnp.arange(n_lat)
    row = (t // GRID_W).astype(F32)[:, None] * inv
    col = (t % GRID_W).astype(F32)[:, None] * inv
    ang = jnp.concatenate([row, row, col, col], axis=-1)
    lane = np.arange(LANES)
    dim = 32 * ((lane >> 4) & 1) + 16 * (lane >> 6) + (lane & 15)
    sign = jnp.asarray(np.where((lane >> 6) == 0, -1.0, 1.0), F32)
    cos = jnp.concatenate([jnp.cos(ang)[:, dim], jnp.ones((n_ctx, LANES), F32)], axis=0)
    sin = jnp.concatenate([jnp.sin(ang)[:, dim] * sign, jnp.zeros((n_ctx, LANES), F32)], axis=0)
    qf = LOG2E * DA_QK ** -0.5
    return jnp.stack([cos, cos * qf]), jnp.stack([sin, sin * qf])


def _pair_layout(w):
    d = w.shape[0]
    w = w.reshape(d, -1, 2, 2, 2, DA_QK // 4)
    return jnp.transpose(w, (0, 1, 4, 2, 3, 5)).reshape(d, -1)


def kernel(x, c, ctx, c_ctx, w_mod, b_mod, g_pre, g_post, w_out, ev_w_in, ev_lambda, ev_subln, ev_conv,
           od_w_in, od_rpb, od_gate_up, od_gate_bias, od_gnorm):
    b, s, d = x.shape
    n_ctx = ctx.shape[1]
    depth = w_mod.shape[0]
    bw = EXPAND * d // 2
    assert bw == DA_HEADS * LANES == NA_HEADS * NA_DIM and (s + n_ctx) % (ROW_SPLIT * 16) == 0
    assert s % ATT_TQ == 0 and s % n_ctx == 0 and s % GRID_W == 0 and n_ctx % GLA_CHUNK == 0
    tm = (s + n_ctx) // ROW_SPLIT
    assert tm % ROPE_ROWS == 0 and (ROW_SPLIT - 1) * tm <= s and s % (ROW_SPLIT * 16) == 0

    xall = jnp.concatenate([x, ctx], axis=1)
    nrow = -(-(b + 1) // 8) * 8
    cc = jnp.zeros((nrow, d), F32).at[:b].set(c).at[b].set(c_ctx)
    mod = _modulation(cc, w_mod, b_mod)
    cos_t, sin_t = _rope_tables(s, n_ctx)

    gk = bw // 2
    od_lr0 = 3 * bw + gk
    od_cols = np.cumsum([0, bw, bw, gk, bw, bw, gk, bw])
    na_cols = (int(od_cols[0]), int(od_cols[1]), int(od_cols[4]), int(od_cols[6]))
    gla_cols = (int(od_cols[2]), int(od_cols[3]), int(od_cols[5]), int(od_cols[7]))

    for l in range(depth):
        j = l // 2
        modb = mod[l, :b].reshape(b, 1, 3 * d)
        modc = mod[l, b:b + 1]
        if l % 2 == 0:
            lam_init = 0.8 - 0.6 * math.exp(-0.3 * l)
            w = ev_w_in[j]
            w = jnp.concatenate([_pair_layout(w[:, :bw]), w[:, bw:2 * bw], _pair_layout(w[:, 2 * bw:3 * bw]),
                                 w[:, 3 * bw:]], axis=1).astype(BF16)
            proj, _ = _inproj(xall, modb, modc, g_pre[l], w, (cos_t, sin_t, 0, 2), None, n_lat=s, tn=bw)
            vt = jnp.swapaxes(proj[:, :, bw:2 * bw], 1, 2).reshape(b, DA_HEADS, LANES, s + n_ctx)
            ua = _diff_attention(proj, vt, ev_lambda[j], ev_subln[j], n_lat=s, lam_init=lam_init)
            ub = _short_conv(proj, ev_conv[j], n_lat=s)
        else:
            w = od_w_in[j]
            w_lr = jnp.pad(w[:, od_lr0:od_lr0 + 2 * GLA_RANK], ((0, 0), (0, LANES - 2 * GLA_RANK))).astype(BF16)
            w = jnp.concatenate([w[:, :od_lr0], w[:, od_lr0 + 2 * GLA_RANK:]], axis=1)
            col_scale = np.ones((1, w.shape[1]), np.float32)
            col_scale[:, na_cols[2]:na_cols[2] + bw] = LOG2E
            w = (w * col_scale).astype(BF16)
            proj, lr = _inproj(xall, modb, modc, g_pre[l], w, None, w_lr, n_lat=s, tn=bw)
            tb = _natten_bias_table(od_rpb[j])
            ua = _neighbourhood_attention(proj, tb, n_lat=s, cols=na_cols)
            gu = jnp.zeros((2, GLA_HEADS, LANES, LANES), F32)
            gup = od_gate_up[j].reshape(2, GLA_RANK, GLA_HEADS, LANES).transpose(0, 2, 1, 3)
            gu = gu.at[0, :, :GLA_RANK].set(gup[0]).at[1, :, GLA_RANK:2 * GLA_RANK].set(gup[1]).astype(BF16)
            gb = od_gate_bias[j].reshape(2, GLA_HEADS, 1, LANES)
            ub = _gla(proj, lr, gu, gb, od_gnorm[j], n_lat=s, cols=gla_cols)
        xall = _outproj(ua, ub, w_out[l].astype(BF16), xall, modb, modc, g_post[l], n_lat=s,
                        latent_only=l == depth - 1)
    return xall
```

```python
import functools
import math

import numpy as np
import jax
import jax.numpy as jnp
from jax import lax
from jax.experimental import pallas as pl
from jax.experimental.pallas import tpu as pltpu

F32 = jnp.float32
BF16 = jnp.bfloat16

LANES = 128
GRID_W = 64
EXPAND = 2
DA_HEADS = 8
DA_QK = 64
SC_WIDTH = 3
NA_HEADS = 16
NA_DIM = 64
WIN_ROWS = 8
WIN_COLS = 16
GLA_HEADS = 4
GLA_RANK = 16
GLA_TAU = 16.0
GLA_CHUNK = 64
ROPE_BASE = 10000.0
EPS = 1e-6
LOG2E = math.log2(math.e)
ROW_SPLIT = 4
ROPE_ROWS = 64
ATT_TQ = 512
ATT_KCHUNK = 256
NA_GROUP = 32
GLA_GROUP = 16
GLA_HEADS_PER_STEP = 2
VMEM_LIMIT = 56 * 1024 * 1024


def _cparams(sem):
    return pltpu.CompilerParams(dimension_semantics=sem, vmem_limit_bytes=VMEM_LIMIT)


def _silu(t):
    return t * jax.nn.sigmoid(t)


def _mod_kernel(c_ref, w_ref, b_ref, o_ref):
    s = _silu(c_ref[...])
    o_ref[0] = jnp.dot(s, w_ref[0], preferred_element_type=F32) + b_ref[0]


def _modulation(cc, w_mod, b_mod):
    depth, d, d3 = w_mod.shape
    nrow = cc.shape[0]
    tn = d
    return pl.pallas_call(
        _mod_kernel,
        grid=(depth, d3 // tn),
        in_specs=[
            pl.BlockSpec((nrow, d), lambda l, j: (0, 0)),
            pl.BlockSpec((1, d, tn), lambda l, j: (l, 0, j)),
            pl.BlockSpec((1, 1, tn), lambda l, j: (l, 0, j)),
        ],
        out_specs=pl.BlockSpec((1, nrow, tn), lambda l, j: (l, 0, j)),
        out_shape=jax.ShapeDtypeStruct((depth, nrow, d3), F32),
        compiler_params=_cparams(("arbitrary", "arbitrary")),
        name="modulation",
    )(cc, w_mod, b_mod.reshape(depth, 1, d3))


def _inproj_kernel(*refs, tm, d, n_lat, rope_tiles, has_lr):
    x_ref, mb_ref, mc_ref, g_ref, w_ref = refs[:5]
    rest = list(refs[5:])
    cos_ref, sin_ref = (rest.pop(0), rest.pop(0)) if rope_tiles else (None, None)
    wlr_ref = rest.pop(0) if has_lr else None
    o_ref = rest.pop(0)
    lr_ref = rest.pop(0) if has_lr else None
    h_scr = rest.pop(0)
    acc_scr = rest.pop(0) if rope_tiles else None
    k = pl.program_id(1)
    j = pl.program_id(2)
    n_blk = pl.num_programs(1)

    @pl.when(j == 0)
    def _():
        xv = x_ref[0]
        xn = xv * lax.rsqrt(jnp.mean(xv * xv, axis=-1, keepdims=True) + EPS)
        mb = mb_ref[0]
        gain_b = g_ref[...] * (1.0 + mb[:, d:2 * d])
        shift_b = mb[:, 0:d]

        def emit(gain, shift):
            h_scr[...] = (xn * gain + shift).astype(BF16)
            if has_lr:
                lr_ref[0] = jnp.dot(h_scr[...], wlr_ref[...], preferred_element_type=F32)

        pl.when(k < n_blk - 1)(lambda: emit(gain_b, shift_b))

        @pl.when(k == n_blk - 1)
        def _():
            is_ctx = k * tm + lax.broadcasted_iota(jnp.int32, (tm, 1), 0) >= n_lat
            gain_c = g_ref[...] * (1.0 + mc_ref[:, d:2 * d])
            emit(jnp.where(is_ctx, gain_c, gain_b), jnp.where(is_ctx, mc_ref[:, 0:d], shift_b))

    tn = o_ref.shape[2]

    def project():
        w_tile = w_ref[:, pl.ds(pl.multiple_of(j * tn, tn), tn)]
        return jnp.dot(h_scr[...], w_tile, preferred_element_type=F32)

    if rope_tiles:
        is_rope = functools.reduce(jnp.logical_or, [j == t for t in rope_tiles])

        @pl.when(is_rope)
        def _():
            acc_scr[...] = project()

            def rows_body(c, carry):
                rows = pl.ds(pl.multiple_of(c * ROPE_ROWS, ROPE_ROWS), ROPE_ROWS)
                cos = cos_ref[0, rows, :]
                sin = sin_ref[0, rows, :]
                for hd in range(tn // LANES):
                    a = acc_scr[rows, hd * LANES:(hd + 1) * LANES]
                    o_ref[0, rows, hd * LANES:(hd + 1) * LANES] = (
                        a * cos + pltpu.roll(a, LANES // 2, 1) * sin).astype(o_ref.dtype)
                return carry

            lax.fori_loop(0, tm // ROPE_ROWS, rows_body, 0, unroll=True)

        @pl.when(jnp.logical_not(is_rope))
        def _():
            o_ref[0] = project().astype(o_ref.dtype)
    else:
        o_ref[0] = project().astype(o_ref.dtype)


def _inproj(xall, modb, modc, g, w, rope, w_lr, *, n_lat, tn):
    b, r, d = xall.shape
    nc = w.shape[1]
    tm = r // ROW_SPLIT
    has_lr = w_lr is not None
    in_specs = [
        pl.BlockSpec((1, tm, d), lambda bi, k, j: (bi, k, 0)),
        pl.BlockSpec((1, 1, 3 * d), lambda bi, k, j: (bi, 0, 0)),
        pl.BlockSpec((1, 3 * d), lambda bi, k, j: (0, 0)),
        pl.BlockSpec((1, d), lambda bi, k, j: (0, 0)),
        pl.BlockSpec((d, nc), lambda bi, k, j: (0, 0), pipeline_mode=pl.Buffered(1)),
    ]
    args = [xall, modb, modc, g.reshape(1, d), w]
    rope_tiles = ()
    if rope is not None:
        cos_t, sin_t, k_tile, q_tile = rope
        rope_tiles = (k_tile, q_tile)
        table = pl.BlockSpec((1, tm, LANES), lambda bi, k, j: (jnp.where(j == q_tile, 1, 0), k, 0))
        in_specs += [table, table]
        args += [cos_t, sin_t]
    out_specs = [pl.BlockSpec((1, tm, tn), lambda bi, k, j: (bi, k, j))]
    out_shape = [jax.ShapeDtypeStruct((b, r, nc), BF16)]
    if has_lr:
        in_specs.append(pl.BlockSpec((d, LANES), lambda bi, k, j: (0, 0)))
        args.append(w_lr)
        out_specs.append(pl.BlockSpec((1, tm, LANES), lambda bi, k, j: (bi, k, 0)))
        out_shape.append(jax.ShapeDtypeStruct((b, r, LANES), F32))
    res = pl.pallas_call(
        functools.partial(_inproj_kernel, tm=tm, d=d, n_lat=n_lat, rope_tiles=rope_tiles, has_lr=has_lr),
        grid=(b, ROW_SPLIT, nc // tn),
        in_specs=in_specs,
        out_specs=out_specs,
        out_shape=out_shape,
        scratch_shapes=[pltpu.VMEM((tm, d), BF16)] + ([pltpu.VMEM((tm, tn), F32)] if rope_tiles else []),
        compiler_params=_cparams(("arbitrary", "arbitrary", "arbitrary")),
        name="inproj",
    )(*args)
    return res if has_lr else (res[0], None)


def _outproj_kernel(ua_ref, ub_ref, w1_ref, w2_ref, x_ref, mb_ref, mc_ref, g_ref, o_ref, *, tm, d, n_lat,
                    has_ctx):
    k = pl.program_id(1)
    y = jnp.dot(ua_ref[0], w1_ref[...], preferred_element_type=F32)
    y = y + jnp.dot(ub_ref[0], w2_ref[...], preferred_element_type=F32)
    y = y * lax.rsqrt(jnp.mean(y * y, axis=-1, keepdims=True) + EPS)
    gate_b = g_ref[...] * mb_ref[0][:, 2 * d:3 * d]
    if not has_ctx:
        o_ref[0] = x_ref[0] + gate_b * y
        return

    n_blk = pl.num_programs(1)

    @pl.when(k < n_blk - 1)
    def _():
        o_ref[0] = x_ref[0] + gate_b * y

    @pl.when(k == n_blk - 1)
    def _():
        is_ctx = k * tm + lax.broadcasted_iota(jnp.int32, (tm, 1), 0) >= n_lat
        gate_c = g_ref[...] * mc_ref[:, 2 * d:3 * d]
        o_ref[0] = x_ref[0] + jnp.where(is_ctx, gate_c, gate_b) * y


def _outproj(ua, ub, w, xall, modb, modc, g, *, n_lat, latent_only):
    b, r, d = xall.shape
    bw = ua.shape[2]
    tm = n_lat // ROW_SPLIT if latent_only else r // ROW_SPLIT
    return pl.pallas_call(
        functools.partial(_outproj_kernel, tm=tm, d=d, n_lat=n_lat, has_ctx=not latent_only),
        grid=(b, ROW_SPLIT),
        in_specs=[
            pl.BlockSpec((1, tm, bw), lambda bi, k: (bi, k, 0)),
            pl.BlockSpec((1, tm, bw), lambda bi, k: (bi, k, 0)),
            pl.BlockSpec((bw, d), lambda bi, k: (0, 0)),
            pl.BlockSpec((bw, d), lambda bi, k: (1, 0)),
            pl.BlockSpec((1, tm, d), lambda bi, k: (bi, k, 0)),
            pl.BlockSpec((1, 1, 3 * d), lambda bi, k: (bi, 0, 0)),
            pl.BlockSpec((1, 3 * d), lambda bi, k: (0, 0)),
            pl.BlockSpec((1, d), lambda bi, k: (0, 0)),
        ],
        out_specs=pl.BlockSpec((1, tm, d), lambda bi, k: (bi, k, 0)),
        out_shape=jax.ShapeDtypeStruct((b, n_lat if latent_only else r, d), F32),
        input_output_aliases={} if latent_only else {4: 0},
        compiler_params=_cparams(("arbitrary", "arbitrary")),
        name="outproj",
    )(ua, ub, w, w, xall, modb, modc, g.reshape(1, d))


def _dattn_lambda(lam_ref, lam_init):
    lp = lam_ref[...]
    return (jnp.exp(jnp.sum(lp[0:1] * lp[1:2], axis=-1, keepdims=True))
            - jnp.exp(jnp.sum(lp[2:3] * lp[3:4], axis=-1, keepdims=True)) + lam_init)


def _dattn_split_maps(q):
    first = (lax.broadcasted_iota(jnp.int32, (1, LANES), 1) & (DA_QK // 2)) == 0
    zero = jnp.zeros_like(q)
    return jnp.concatenate([jnp.where(first, q, zero), jnp.where(first, zero, q)], axis=0)


def _dattn_finish(o, sub_ref, gate, lam_init):
    o = o * lax.rsqrt(jnp.mean(o * o, axis=-1, keepdims=True) + EPS) * sub_ref[...] * (1.0 - lam_init)
    return (o * _silu(gate.astype(F32))).astype(BF16)


def _dattn_kernel(lam_ref, sub_ref, qn_ref, k_ref, vt_ref, g_ref, ua_ref, o_ref, sa_ref, ma_ref, sb_ref, mb_ref,
                  acc_ref, den_ref, *, n_blocks, lam_init):
    del ua_ref
    t = pl.program_id(0)
    lam = _dattn_lambda(lam_ref, lam_init)
    nt = (((1,), (1,)), ((), ()))
    tq = qn_ref.shape[1]

    def step(cur, prev, fin):
        if fin:
            o = acc_ref[...] * (1.0 / den_ref[...])
            o_ref[0] = _dattn_finish((o[:, :tq] - lam * o[:, tq:]).T, sub_ref, g_ref[0], lam_init)
        r = k_ref.shape[1]
        bounds = list(range(0, r - ATT_KCHUNK + 1, ATT_KCHUNK)) + [r]
        if cur is not None:
            q2 = _dattn_split_maps(qn_ref[0])
        m_cur = l_prev = o_prev = None
        for lo, hi in zip(bounds[:-1], bounds[1:]):
            if cur is not None:
                s = lax.dot_general(k_ref[0, lo:hi, :], q2, nt, preferred_element_type=F32)
                cur[0][lo:hi, :] = s
                m = jnp.max(s, axis=0, keepdims=True)
                m_cur = m if m_cur is None else jnp.maximum(m_cur, m)
            if prev is not None:
                e = jnp.exp2(prev[0][lo:hi, :] - prev[1][...])
                l = jnp.sum(e, axis=0, keepdims=True)
                o = jnp.dot(vt_ref[0, 0, :, lo:hi], e.astype(BF16), preferred_element_type=F32)
                l_prev = l if l_prev is None else l_prev + l
                o_prev = o if o_prev is None else o_prev + o
        if cur is not None:
            cur[1][...] = m_cur
        if prev is not None:
            acc_ref[...] = o_prev
            den_ref[...] = l_prev

    bufs = ((sa_ref, ma_ref), (sb_ref, mb_ref))
    last = bufs[(n_blocks - 1) % 2]
    pl.when(t == 0)(lambda: step(bufs[0], None, False))
    pl.when(t == 1)(lambda: step(bufs[1], bufs[0], False))
    pl.when((t >= 2) & (t < n_blocks) & (t % 2 == 0))(lambda: step(bufs[0], bufs[1], True))
    pl.when((t >= 2) & (t < n_blocks) & (t % 2 == 1))(lambda: step(bufs[1], bufs[0], True))
    pl.when(t == n_blocks)(lambda: step(None, last, True))
    pl.when(t == n_blocks + 1)(lambda: step(None, None, True))


def _dattn_ctx_kernel(lam_ref, sub_ref, q_ref, k_ref, v_ref, g_ref, ua_ref, o_ref, *, lam_init):
    del ua_ref
    lam = _dattn_lambda(lam_ref, lam_init)
    nt = (((1,), (1,)), ((), ()))
    n = q_ref.shape[1]
    for hd in range(q_ref.shape[2] // LANES):
        cols = slice(hd * LANES, (hd + 1) * LANES)
        s = lax.dot_general(_dattn_split_maps(q_ref[0, :, cols]), k_ref[0, :, cols], nt,
                            preferred_element_type=F32)
        e = jnp.exp2(s - jnp.max(s, axis=-1, keepdims=True))
        p = e / jnp.sum(e, axis=-1, keepdims=True)
        a = (p[:n] - lam * p[n:]).astype(BF16)
        o = jnp.dot(a, v_ref[0, :, cols], preferred_element_type=F32)
        o_ref[0, :, cols] = _dattn_finish(o, sub_ref, g_ref[0, :, cols], lam_init)


def _diff_attention(proj, vt, lam_p, subln, *, n_lat, lam_init):
    b, r, _ = proj.shape
    h = DA_HEADS
    tq = ATT_TQ
    nq = n_lat // tq
    n_ctx = r - n_lat
    sub = subln.reshape(1, LANES)
    small = [pl.BlockSpec(lam_p.shape, lambda *_: (0, 0)), pl.BlockSpec((1, LANES), lambda *_: (0, 0))]
    n_blocks = b * h * nq

    def unit(t, lag):
        u = jnp.clip(t - lag, 0, n_blocks - 1)
        return u // (h * nq), (u // nq) % h, u % nq

    def q_map(t):
        bi, hi, i = unit(t, 0)
        return bi, i, 2 * h + hi

    def k_map(t):
        bi, hi, _ = unit(t, 0)
        return bi, 0, hi

    def vt_map(t):
        bi, hi, _ = unit(t, 1)
        return bi, hi, 0, 0

    def g_map(t):
        bi, hi, i = unit(t, 2)
        return bi, i, 3 * h + hi

    def o_map(t):
        bi, hi, i = unit(t, 2)
        return bi, i, hi

    ua = pl.pallas_call(
        functools.partial(_dattn_kernel, n_blocks=n_blocks, lam_init=lam_init),
        grid=(n_blocks + 2,),
        in_specs=small + [
            pl.BlockSpec((1, tq, LANES), q_map),
            pl.BlockSpec((1, r, LANES), k_map),
            pl.BlockSpec((1, 1, LANES, r), vt_map),
            pl.BlockSpec((1, tq, LANES), g_map),
            pl.BlockSpec(memory_space=pl.ANY),
        ],
        out_specs=pl.BlockSpec((1, tq, LANES), o_map),
        out_shape=jax.ShapeDtypeStruct((b, r, h * LANES), BF16),
        scratch_shapes=[pltpu.VMEM((r, 2 * tq), F32), pltpu.VMEM((1, 2 * tq), F32)] * 2
        + [pltpu.VMEM((LANES, 2 * tq), F32), pltpu.VMEM((1, 2 * tq), F32)],
        input_output_aliases={6: 0},
        compiler_params=_cparams(("arbitrary",)),
        name="diff_attention",
    )(lam_p, sub, proj, proj, vt, proj, jnp.zeros((b, r, h * LANES), BF16))
    cb = n_lat // n_ctx
    ctx_rows = lambda grp: pl.BlockSpec((1, n_ctx, h * LANES), lambda bi: (bi, cb, grp))
    return pl.pallas_call(
        functools.partial(_dattn_ctx_kernel, lam_init=lam_init),
        grid=(b,),
        in_specs=small + [ctx_rows(2), ctx_rows(0), ctx_rows(1), ctx_rows(3),
                          pl.BlockSpec(memory_space=pl.ANY)],
        out_specs=ctx_rows(0),
        out_shape=jax.ShapeDtypeStruct(ua.shape, ua.dtype),
        input_output_aliases={6: 0},
        compiler_params=_cparams(("arbitrary",)),
        name="diff_attention_ctx",
    )(lam_p, sub, proj, proj, proj, proj, ua)


def _sconv_kernel(h_ref, b_ref, c_ref, g_ref, w_ref, o_ref, *, n_lat):
    r = h_ref.shape[1]
    p = c_ref[0].astype(F32) * h_ref[0].astype(F32)
    row = lax.broadcasted_iota(jnp.int32, (r, 1), 0)
    prev = jnp.where((row == 0) | (row == n_lat), 0.0, pltpu.roll(p, 1, 0))
    nxt = jnp.where((row == n_lat - 1) | (row == r - 1), 0.0, pltpu.roll(p, r - 1, 0))
    w = w_ref[...]
    conv = w[0:1] * prev + w[1:2] * p + w[2:3] * nxt
    o_ref[0] = (b_ref[0].astype(F32) * conv * _silu(g_ref[0].astype(F32))).astype(o_ref.dtype)


def _short_conv(proj, conv_w, *, n_lat):
    b, r, _ = proj.shape
    nb = conv_w.shape[1] // LANES
    base = 4 * nb
    spec = lambda off: pl.BlockSpec((1, r, LANES), lambda bi, j: (bi, 0, off + j))
    return pl.pallas_call(
        functools.partial(_sconv_kernel, n_lat=n_lat),
        grid=(b, nb),
        in_specs=[spec(base), spec(base + nb), spec(base + 2 * nb), spec(base + 3 * nb),
                  pl.BlockSpec((SC_WIDTH, LANES), lambda bi, j: (0, j))],
        out_specs=pl.BlockSpec((1, r, LANES), lambda bi, j: (bi, 0, j)),
        out_shape=jax.ShapeDtypeStruct((b, r, nb * LANES), BF16),
        compiler_params=_cparams(("arbitrary", "arbitrary")),
        name="short_conv",
    )(proj, proj, proj, proj, conv_w)


def _natten_kernel(q_ref, k_ref, v_ref, g_ref, tb_ref, o_ref, *, n_lat, n_rows):
    lane = lax.broadcasted_iota(jnp.int32, (1, LANES), 1)
    first = lane < NA_DIM
    nt = (((1,), (1,)), ((), ()))
    win = WIN_ROWS * GRID_W
    kc = k_ref[0, n_lat:, :]
    vc = v_ref[0, n_lat:, :]

    def stack_heads(q):
        q = q * (NA_DIM ** -0.5)
        zero = jnp.zeros_like(q)
        return jnp.concatenate([jnp.where(first, q, zero), jnp.where(first, zero, q)], axis=0)

    def softmax(parts):
        m = functools.reduce(jnp.maximum, [jnp.max(s, axis=-1, keepdims=True) for s in parts])
        es = [jnp.exp2(s - m) for s in parts]
        l = functools.reduce(jnp.add, [jnp.sum(e, axis=-1, keepdims=True) for e in es])
        return [e.astype(BF16) for e in es], 1.0 / l

    def unstack_heads(o2):
        n = o2.shape[0] // 2
        return jnp.where(first, o2[:n], o2[n:])

    def group_body(gi, carry):
        rows = [gi * NA_GROUP + t for t in range(NA_GROUP)]
        starts = [jnp.clip(rr - WIN_ROWS // 2, 0, n_rows - WIN_ROWS) for rr in rows]
        qsl = [pl.ds(pl.multiple_of(rr * GRID_W, GRID_W), GRID_W) for rr in rows]
        ksl = [pl.ds(pl.multiple_of(rs * GRID_W, GRID_W), win) for rs in starts]
        scores = {}
        weights = {}
        for i in range(NA_GROUP + 2):
            if i < NA_GROUP:
                q2 = stack_heads(q_ref[0, qsl[i], :])
                s_loc = lax.dot_general(q2, k_ref[0, ksl[i], :], nt, preferred_element_type=F32)
                off = starts[i] - rows[i] + (WIN_ROWS - 1)
                s_loc = s_loc + jnp.concatenate([tb_ref[0, off + 2 * p] for p in range(WIN_ROWS // 2)], axis=1)
                scores[i] = [s_loc, lax.dot_general(q2, kc, nt, preferred_element_type=F32)]
            if 0 <= i - 1 < NA_GROUP:
                weights[i - 1] = softmax(scores.pop(i - 1))
            if 0 <= i - 2 < NA_GROUP:
                es, rl = weights.pop(i - 2)
                qs, ks = qsl[i - 2], ksl[i - 2]
                o2 = jnp.dot(es[0], v_ref[0, ks, :], preferred_element_type=F32)
                o2 = o2 + jnp.dot(es[1], vc, preferred_element_type=F32)
                out = unstack_heads(o2 * rl)
                o_ref[0, qs, :] = (out * _silu(g_ref[0, qs, :].astype(F32))).astype(o_ref.dtype)
        return carry

    lax.fori_loop(0, n_rows // NA_GROUP, group_body, 0)

    s = lax.dot_general(stack_heads(q_ref[0, n_lat:, :]), kc, nt, preferred_element_type=F32)
    es, rl = softmax([s])
    outc = unstack_heads(jnp.dot(es[0], vc, preferred_element_type=F32) * rl)
    o_ref[0, n_lat:, :] = (outc * _silu(g_ref[0, n_lat:, :].astype(F32))).astype(o_ref.dtype)


def _neighbourhood_attention(proj, tb, *, n_lat, cols):
    b, r, _ = proj.shape
    nb = NA_HEADS // 2
    kb, vb, qb, gb = (c // LANES for c in cols)
    spec = lambda off: pl.BlockSpec((1, r, LANES), lambda bi, j: (bi, 0, off + j))
    return pl.pallas_call(
        functools.partial(_natten_kernel, n_lat=n_lat, n_rows=n_lat // GRID_W),
        grid=(b, nb),
        in_specs=[spec(qb), spec(kb), spec(vb), spec(gb),
                  pl.BlockSpec((1,) + tb.shape[1:], lambda bi, j: (j, 0, 0, 0))],
        out_specs=pl.BlockSpec((1, r, LANES), lambda bi, j: (bi, 0, j)),
        out_shape=jax.ShapeDtypeStruct((b, r, nb * LANES), BF16),
        compiler_params=_cparams(("arbitrary", "arbitrary")),
        name="neighbourhood_attention",
    )(proj, proj, proj, proj, tb)


def _natten_bias_table(rpb):
    qcol = np.arange(GRID_W)[:, None]
    kcol = np.arange(GRID_W)[None, :]
    win_start = np.clip(qcol - WIN_COLS // 2, 0, GRID_W - WIN_COLS)
    valid = (kcol >= win_start) & (kcol < win_start + WIN_COLS)
    pad = GRID_W - WIN_COLS
    padded = jnp.pad(rpb.astype(F32), ((0, 0), (0, 0), (pad, pad)))
    toe = jnp.stack([padded[:, :, GRID_W - 1 - q:2 * GRID_W - 1 - q] for q in range(GRID_W)], axis=2)
    toe = jnp.where(jnp.asarray(valid)[None, None], toe * LOG2E, -jnp.inf)
    t = jnp.concatenate([toe[:, :-1], toe[:, 1:]], axis=-1)
    nh, n_o = t.shape[:2]
    t = jnp.transpose(t.reshape(nh // 2, 2, n_o, GRID_W, 2 * GRID_W), (0, 2, 1, 3, 4))
    return t.reshape(nh // 2, n_o, 2 * GRID_W, 2 * GRID_W)


def _gla_kernel(q_ref, k_ref, v_ref, g_ref, lr_ref, gu_ref, gb_ref, gn_ref, o_ref, of_scr, st_scr,
                *, n_lat, n_ctx, qscale):
    ch = GLA_CHUNK
    mid = ch // 2
    n_heads, dv, dk = st_scr.shape
    nt = (((1,), (1,)), ((), ()))
    tn = (((0,), (0,)), ((), ()))
    trow = lax.broadcasted_iota(jnp.int32, (ch, ch), 0)
    tcol = lax.broadcasted_iota(jnp.int32, (ch, ch), 1)
    srow = lax.broadcasted_iota(jnp.int32, (ch, 1), 0)

    def cumsum_rows(g, reverse):
        acc = g
        sh = 1
        while sh < ch:
            if reverse:
                acc = acc + jnp.where(srow < ch - sh, pltpu.roll(acc, ch - sh, 0), 0.0)
            else:
                acc = acc + jnp.where(srow >= sh, pltpu.roll(acc, sh, 0), 0.0)
            sh *= 2
        return acc

    def group(starts, dirn):
        reverse = dirn == 1
        keep = (tcol >= trow) if reverse else (tcol <= trow)
        rows_l = [pl.ds(s if isinstance(s, int) else pl.multiple_of(s, ch), ch) for s in starts]
        items = [(rows, hh, slice(hh * dk, (hh + 1) * dk), slice(hh * dv, (hh + 1) * dv))
                 for rows in rows_l for hh in range(n_heads)]
        zs = [jnp.dot(lr_ref[0, rows, :].astype(BF16), gu_ref[dirn, hh], preferred_element_type=F32)
              for rows, hh, _, _ in items]
        ops = []
        for (rows, hh, kc, _), z in zip(items, zs):
            g = jax.nn.log_sigmoid(z + gb_ref[dirn, hh]) * (1.0 / GLA_TAU)
            bcum = cumsum_rows(g, reverse)
            b_end = bcum[0:1] if reverse else bcum[ch - 1:ch]
            rho = bcum[mid:mid + 1]
            up = jnp.exp(bcum - rho)
            dn = jnp.exp(rho - bcum)
            qf = q_ref[0, rows, kc].astype(F32) * qscale
            kf = k_ref[0, rows, kc].astype(F32)
            ops.append(dict(
                q_state=(qf * (up * jnp.exp(rho))).astype(BF16),
                q_local=(qf * up).astype(BF16),
                k_local=(kf * dn).astype(BF16),
                k_state=(kf * (dn * jnp.exp(b_end - rho))).astype(BF16),
                decay=jnp.exp(b_end)))
        atts = [lax.dot_general(o["q_local"], o["k_local"], nt, preferred_element_type=F32) for o in ops]
        st_adds = [lax.dot_general(v_ref[0, rows, vc], o["k_state"], tn, preferred_element_type=F32)
                   for (rows, _, _, vc), o in zip(items, ops)]
        o_locals = [jnp.dot(jnp.where(keep, att, 0.0).astype(BF16), v_ref[0, rows, vc], preferred_element_type=F32)
                    for (rows, _, _, vc), att in zip(items, atts)]
        sts = [st_scr[hh] for hh in range(n_heads)]
        states = []
        for (_, hh, _, _), o, st_add in zip(items, ops, st_adds):
            states.append(sts[hh].astype(BF16))
            sts[hh] = sts[hh] * o["decay"] + st_add
        for hh in range(n_heads):
            st_scr[hh] = sts[hh]
        outs = [o_local + lax.dot_general(o["q_state"], s_in, nt, preferred_element_type=F32)
                for o, o_local, s_in in zip(ops, o_locals, states)]
        for (rows, hh, _, vc), o in zip(items, outs):
            if dirn == 0:
                of_scr[rows, vc] = o
            else:
                o = o + of_scr[rows, vc]
                o = o * lax.rsqrt(jnp.mean(o * o, axis=-1, keepdims=True) + EPS) * gn_ref[...]
                o_ref[0, rows, vc] = (o * _silu(g_ref[0, rows, vc].astype(F32))).astype(o_ref.dtype)

    n_cc = n_ctx // ch
    n_lg = n_lat // (ch * GLA_GROUP)
    gsz = ch * GLA_GROUP

    def latent_group(gi, dirn):
        if dirn == 0:
            group([gi * gsz + t * ch for t in range(GLA_GROUP)], 0)
        else:
            group([(n_lg - 1 - gi) * gsz + t * ch for t in range(GLA_GROUP - 1, -1, -1)], 1)

    st_scr[...] = jnp.zeros_like(st_scr)
    group([n_lat + c * ch for c in range(n_cc)], 0)
    lax.fori_loop(0, n_lg, lambda gi, carry: (latent_group(gi, 0), carry)[1], 0)
    st_scr[...] = jnp.zeros_like(st_scr)
    group([n_lat + c * ch for c in range(n_cc - 1, -1, -1)], 1)
    lax.fori_loop(0, n_lg, lambda gi, carry: (latent_group(gi, 1), carry)[1], 0)


def _gla(proj, lr, gu, gb, gnorm, *, n_lat, cols):
    b, r, _ = proj.shape
    nh = GLA_HEADS
    dk = LANES
    dv = 2 * LANES
    kcol, vcol, qcol, gcol = cols
    hs = GLA_HEADS_PER_STEP
    wk, wv = hs * dk, hs * dv
    return pl.pallas_call(
        functools.partial(_gla_kernel, n_lat=n_lat, n_ctx=r - n_lat, qscale=dk ** -0.5),
        grid=(b, nh // hs),
        in_specs=[
            pl.BlockSpec((1, r, wk), lambda bi, hi: (bi, 0, qcol // wk + hi)),
            pl.BlockSpec((1, r, wk), lambda bi, hi: (bi, 0, kcol // wk + hi)),
            pl.BlockSpec((1, r, wv), lambda bi, hi: (bi, 0, vcol // wv + hi)),
            pl.BlockSpec((1, r, wv), lambda bi, hi: (bi, 0, gcol // wv + hi)),
            pl.BlockSpec((1, r, LANES), lambda bi, hi: (bi, 0, 0)),
            pl.BlockSpec((2, hs, LANES, dk), lambda bi, hi: (0, hi, 0, 0)),
            pl.BlockSpec((2, hs, 1, dk), lambda bi, hi: (0, hi, 0, 0)),
            pl.BlockSpec((1, dv), lambda bi, hi: (0, 0)),
        ],
        out_specs=pl.BlockSpec((1, r, wv), lambda bi, hi: (bi, 0, hi)),
        out_shape=jax.ShapeDtypeStruct((b, r, nh * dv), BF16),
        scratch_shapes=[pltpu.VMEM((r, wv), F32), pltpu.VMEM((hs, dv, dk), F32)],
        compiler_params=_cparams(("arbitrary", "arbitrary")),
        name="gla",
    )(proj, proj, proj, proj, lr, gu, gb, gnorm.reshape(1, dv))


def _rope_tables(n_lat, n_ctx):
    half = DA_QK // 2
    inv = 1.0 / (ROPE_BASE ** (jnp.arange(0, half, 2, dtype=F32) / half))
    t = jnp.arange(n_lat)
    row = (t // GRID_W).astype(F32)[:, None] * inv
    col = (t % GRID_W).astype(F32)[:, None] * inv
    ang = jnp.concatenate([row, row, col, col], axis=-1)
    lane = np.arange(LANES)
    dim = 32 * ((lane >> 4) & 1) + 16 * (lane >> 6) + (lane & 15)
    sign = jnp.asarray(np.where((lane >> 6) == 0, -1.0, 1.0), F32)
    cos = jnp.concatenate([jnp.cos(ang)[:, dim], jnp.ones((n_ctx, LANES), F32)], axis=0)
    sin = jnp.concatenate([jnp.sin(ang)[:, dim] * sign, jnp.zeros((n_ctx, LANES), F32)], axis=0)
    qf = LOG2E * DA_QK ** -0.5
    return jnp.stack([cos, cos * qf]), jnp.stack([sin, sin * qf])


def _pair_layout(w):
    d = w.shape[0]
    w = w.reshape(d, -1, 2, 2, 2, DA_QK // 4)
    return jnp.transpose(w, (0, 1, 4, 2, 3, 5)).reshape(d, -1)


def kernel(x, c, ctx, c_ctx, w_mod, b_mod, g_pre, g_post, w_out, ev_w_in, ev_lambda, ev_subln, ev_conv,
           od_w_in, od_rpb, od_gate_up, od_gate_bias, od_gnorm):
    b, s, d = x.shape
    n_ctx = ctx.shape[1]
    depth = w_mod.shape[0]
    bw = EXPAND * d // 2
    assert bw == DA_HEADS * LANES == NA_HEADS * NA_DIM and (s + n_ctx) % (ROW_SPLIT * 16) == 0
    assert s % ATT_TQ == 0 and s % n_ctx == 0 and s % GRID_W == 0 and n_ctx % GLA_CHUNK == 0
    assert (s // GRID_W) % NA_GROUP == 0 and s % (GLA_CHUNK * GLA_GROUP) == 0
    tm = (s + n_ctx) // ROW_SPLIT
    assert tm % ROPE_ROWS == 0 and (ROW_SPLIT - 1) * tm <= s and s % (ROW_SPLIT * 16) == 0

    xall = jnp.concatenate([x, ctx], axis=1)
    nrow = -(-(b + 1) // 8) * 8
    cc = jnp.zeros((nrow, d), F32).at[:b].set(c).at[b].set(c_ctx)
    mod = _modulation(cc, w_mod, b_mod)
    cos_t, sin_t = _rope_tables(s, n_ctx)

    gk = bw // 2
    od_lr0 = 3 * bw + gk
    od_cols = np.cumsum([0, bw, bw, gk, bw, bw, gk, bw])
    na_cols = (int(od_cols[0]), int(od_cols[1]), int(od_cols[4]), int(od_cols[6]))
    gla_cols = (int(od_cols[2]), int(od_cols[3]), int(od_cols[5]), int(od_cols[7]))

    for l in range(depth):
        j = l // 2
        modb = mod[l, :b].reshape(b, 1, 3 * d)
        modc = mod[l, b:b + 1]
        if l % 2 == 0:
            lam_init = 0.8 - 0.6 * math.exp(-0.3 * l)
            w = ev_w_in[j]
            w = jnp.concatenate([_pair_layout(w[:, :bw]), w[:, bw:2 * bw], _pair_layout(w[:, 2 * bw:3 * bw]),
                                 w[:, 3 * bw:]], axis=1).astype(BF16)
            proj, _ = _inproj(xall, modb, modc, g_pre[l], w, (cos_t, sin_t, 0, 2), None, n_lat=s, tn=bw)
            vt = jnp.swapaxes(proj[:, :, bw:2 * bw], 1, 2).reshape(b, DA_HEADS, LANES, s + n_ctx)
            ua = _diff_attention(proj, vt, ev_lambda[j], ev_subln[j], n_lat=s, lam_init=lam_init)
            ub = _short_conv(proj, ev_conv[j], n_lat=s)
        else:
            w = od_w_in[j]
            w_lr = jnp.pad(w[:, od_lr0:od_lr0 + 2 * GLA_RANK], ((0, 0), (0, LANES - 2 * GLA_RANK))).astype(BF16)
            w = jnp.concatenate([w[:, :od_lr0], w[:, od_lr0 + 2 * GLA_RANK:]], axis=1)
            col_scale = np.ones((1, w.shape[1]), np.float32)
            col_scale[:, na_cols[2]:na_cols[2] + bw] = LOG2E
            w = (w * col_scale).astype(BF16)
            proj, lr = _inproj(xall, modb, modc, g_pre[l], w, None, w_lr, n_lat=s, tn=bw)
            tb = _natten_bias_table(od_rpb[j])
            ua = _neighbourhood_attention(proj, tb, n_lat=s, cols=na_cols)
            gu = jnp.zeros((2, GLA_HEADS, LANES, LANES), F32)
            gup = od_gate_up[j].reshape(2, GLA_RANK, GLA_HEADS, LANES).transpose(0, 2, 1, 3)
            gu = gu.at[0, :, :GLA_RANK].set(gup[0]).at[1, :, GLA_RANK:2 * GLA_RANK].set(gup[1]).astype(BF16)
            gb = od_gate_bias[j].reshape(2, GLA_HEADS, 1, LANES)
            ub = _gla(proj, lr, gu, gb, od_gnorm[j], n_lat=s, cols=gla_cols)
        xall = _outproj(ua, ub, w_out[l].astype(BF16), xall, modb, modc, g_post[l], n_lat=s,
                        latent_only=l == depth - 1)
    return xall
```

```python
import functools
import math

import numpy as np
import jax
import jax.numpy as jnp
from jax import lax
from jax.experimental import pallas as pl
from jax.experimental.pallas import tpu as pltpu

F32 = jnp.float32
BF16 = jnp.bfloat16

LANES = 128
GRID_W = 64
EXPAND = 2
DA_HEADS = 8
DA_QK = 64
SC_WIDTH = 3
NA_HEADS = 16
NA_DIM = 64
WIN_ROWS = 8
WIN_COLS = 16
GLA_HEADS = 4
GLA_RANK = 16
GLA_TAU = 16.0
GLA_CHUNK = 64
ROPE_BASE = 10000.0
EPS = 1e-6
LOG2E = math.log2(math.e)
ROW_SPLIT = 4
ROPE_ROWS = 64
ATT_TQ = 512
ATT_KCHUNK = 256
NA_GROUP = 64
GLA_GROUP = 16
GLA_HEADS_PER_STEP = 2
VMEM_LIMIT = 56 * 1024 * 1024


def _cparams(sem):
    return pltpu.CompilerParams(dimension_semantics=sem, vmem_limit_bytes=VMEM_LIMIT)


def _silu(t):
    return t * jax.nn.sigmoid(t)


def _mod_kernel(c_ref, w_ref, b_ref, o_ref):
    s = _silu(c_ref[...])
    o_ref[0] = jnp.dot(s, w_ref[0], preferred_element_type=F32) + b_ref[0]


def _modulation(cc, w_mod, b_mod):
    depth, d, d3 = w_mod.shape
    nrow = cc.shape[0]
    tn = d
    return pl.pallas_call(
        _mod_kernel,
        grid=(depth, d3 // tn),
        in_specs=[
            pl.BlockSpec((nrow, d), lambda l, j: (0, 0)),
            pl.BlockSpec((1, d, tn), lambda l, j: (l, 0, j)),
            pl.BlockSpec((1, 1, tn), lambda l, j: (l, 0, j)),
        ],
        out_specs=pl.BlockSpec((1, nrow, tn), lambda l, j: (l, 0, j)),
        out_shape=jax.ShapeDtypeStruct((depth, nrow, d3), F32),
        compiler_params=_cparams(("arbitrary", "arbitrary")),
        name="modulation",
    )(cc, w_mod, b_mod.reshape(depth, 1, d3))


def _inproj_kernel(*refs, tm, d, n_lat, rope_tiles, has_lr):
    x_ref, mb_ref, mc_ref, g_ref, w_ref = refs[:5]
    rest = list(refs[5:])
    cos_ref, sin_ref = (rest.pop(0), rest.pop(0)) if rope_tiles else (None, None)
    wlr_ref = rest.pop(0) if has_lr else None
    o_ref = rest.pop(0)
    lr_ref = rest.pop(0) if has_lr else None
    h_scr = rest.pop(0)
    acc_scr = rest.pop(0) if rope_tiles else None
    k = pl.program_id(1)
    j = pl.program_id(2)
    n_blk = pl.num_programs(1)

    @pl.when(j == 0)
    def _():
        xv = x_ref[0]
        xn = xv * lax.rsqrt(jnp.mean(xv * xv, axis=-1, keepdims=True) + EPS)
        mb = mb_ref[0]
        gain_b = g_ref[...] * (1.0 + mb[:, d:2 * d])
        shift_b = mb[:, 0:d]

        def emit(gain, shift):
            h_scr[...] = (xn * gain + shift).astype(BF16)
            if has_lr:
                lr_ref[0] = jnp.dot(h_scr[...], wlr_ref[...], preferred_element_type=F32)

        pl.when(k < n_blk - 1)(lambda: emit(gain_b, shift_b))

        @pl.when(k == n_blk - 1)
        def _():
            is_ctx = k * tm + lax.broadcasted_iota(jnp.int32, (tm, 1), 0) >= n_lat
            gain_c = g_ref[...] * (1.0 + mc_ref[:, d:2 * d])
            emit(jnp.where(is_ctx, gain_c, gain_b), jnp.where(is_ctx, mc_ref[:, 0:d], shift_b))

    tn = o_ref.shape[2]

    def project():
        w_tile = w_ref[:, pl.ds(pl.multiple_of(j * tn, tn), tn)]
        return jnp.dot(h_scr[...], w_tile, preferred_element_type=F32)

    if rope_tiles:
        is_rope = functools.reduce(jnp.logical_or, [j == t for t in rope_tiles])

        @pl.when(is_rope)
        def _():
            acc_scr[...] = project()

            def rows_body(c, carry):
                rows = pl.ds(pl.multiple_of(c * ROPE_ROWS, ROPE_ROWS), ROPE_ROWS)
                cos = cos_ref[0, rows, :]
                sin = sin_ref[0, rows, :]
                for hd in range(tn // LANES):
                    a = acc_scr[rows, hd * LANES:(hd + 1) * LANES]
                    o_ref[0, rows, hd * LANES:(hd + 1) * LANES] = (
                        a * cos + pltpu.roll(a, LANES // 2, 1) * sin).astype(o_ref.dtype)
                return carry

            lax.fori_loop(0, tm // ROPE_ROWS, rows_body, 0, unroll=True)

        @pl.when(jnp.logical_not(is_rope))
        def _():
            o_ref[0] = project().astype(o_ref.dtype)
    else:
        o_ref[0] = project().astype(o_ref.dtype)


def _inproj(xall, modb, modc, g, w, rope, w_lr, *, n_lat, tn):
    b, r, d = xall.shape
    nc = w.shape[1]
    tm = r // ROW_SPLIT
    has_lr = w_lr is not None
    in_specs = [
        pl.BlockSpec((1, tm, d), lambda bi, k, j: (bi, k, 0)),
        pl.BlockSpec((1, 1, 3 * d), lambda bi, k, j: (bi, 0, 0)),
        pl.BlockSpec((1, 3 * d), lambda bi, k, j: (0, 0)),
        pl.BlockSpec((1, d), lambda bi, k, j: (0, 0)),
        pl.BlockSpec((d, nc), lambda bi, k, j: (0, 0), pipeline_mode=pl.Buffered(1)),
    ]
    args = [xall, modb, modc, g.reshape(1, d), w]
    rope_tiles = ()
    if rope is not None:
        cos_t, sin_t, k_tile, q_tile = rope
        rope_tiles = (k_tile, q_tile)
        table = pl.BlockSpec((1, tm, LANES), lambda bi, k, j: (jnp.where(j == q_tile, 1, 0), k, 0))
        in_specs += [table, table]
        args += [cos_t, sin_t]
    out_specs = [pl.BlockSpec((1, tm, tn), lambda bi, k, j: (bi, k, j))]
    out_shape = [jax.ShapeDtypeStruct((b, r, nc), BF16)]
    if has_lr:
        in_specs.append(pl.BlockSpec((d, LANES), lambda bi, k, j: (0, 0)))
        args.append(w_lr)
        out_specs.append(pl.BlockSpec((1, tm, LANES), lambda bi, k, j: (bi, k, 0)))
        out_shape.append(jax.ShapeDtypeStruct((b, r, LANES), F32))
    res = pl.pallas_call(
        functools.partial(_inproj_kernel, tm=tm, d=d, n_lat=n_lat, rope_tiles=rope_tiles, has_lr=has_lr),
        grid=(b, ROW_SPLIT, nc // tn),
        in_specs=in_specs,
        out_specs=out_specs,
        out_shape=out_shape,
        scratch_shapes=[pltpu.VMEM((tm, d), BF16)] + ([pltpu.VMEM((tm, tn), F32)] if rope_tiles else []),
        compiler_params=_cparams(("arbitrary", "arbitrary", "arbitrary")),
        name="inproj",
    )(*args)
    return res if has_lr else (res[0], None)


def _outproj_kernel(ua_ref, ub_ref, w1_ref, w2_ref, x_ref, mb_ref, mc_ref, g_ref, o_ref, *, tm, d, n_lat,
                    has_ctx):
    k = pl.program_id(1)
    y = jnp.dot(ua_ref[0], w1_ref[...], preferred_element_type=F32)
    y = y + jnp.dot(ub_ref[0], w2_ref[...], preferred_element_type=F32)
    y = y * lax.rsqrt(jnp.mean(y * y, axis=-1, keepdims=True) + EPS)
    gate_b = g_ref[...] * mb_ref[0][:, 2 * d:3 * d]
    if not has_ctx:
        o_ref[0] = x_ref[0] + gate_b * y
        return

    n_blk = pl.num_programs(1)

    @pl.when(k < n_blk - 1)
    def _():
        o_ref[0] = x_ref[0] + gate_b * y

    @pl.when(k == n_blk - 1)
    def _():
        is_ctx = k * tm + lax.broadcasted_iota(jnp.int32, (tm, 1), 0) >= n_lat
        gate_c = g_ref[...] * mc_ref[:, 2 * d:3 * d]
        o_ref[0] = x_ref[0] + jnp.where(is_ctx, gate_c, gate_b) * y


def _outproj(ua, ub, w, xall, modb, modc, g, *, n_lat, latent_only):
    b, r, d = xall.shape
    bw = ua.shape[2]
    tm = n_lat // ROW_SPLIT if latent_only else r // ROW_SPLIT
    return pl.pallas_call(
        functools.partial(_outproj_kernel, tm=tm, d=d, n_lat=n_lat, has_ctx=not latent_only),
        grid=(b, ROW_SPLIT),
        in_specs=[
            pl.BlockSpec((1, tm, bw), lambda bi, k: (bi, k, 0)),
            pl.BlockSpec((1, tm, bw), lambda bi, k: (bi, k, 0)),
            pl.BlockSpec((bw, d), lambda bi, k: (0, 0)),
            pl.BlockSpec((bw, d), lambda bi, k: (1, 0)),
            pl.BlockSpec((1, tm, d), lambda bi, k: (bi, k, 0)),
            pl.BlockSpec((1, 1, 3 * d), lambda bi, k: (bi, 0, 0)),
            pl.BlockSpec((1, 3 * d), lambda bi, k: (0, 0)),
            pl.BlockSpec((1, d), lambda bi, k: (0, 0)),
        ],
        out_specs=pl.BlockSpec((1, tm, d), lambda bi, k: (bi, k, 0)),
        out_shape=jax.ShapeDtypeStruct((b, n_lat if latent_only else r, d), F32),
        input_output_aliases={} if latent_only else {4: 0},
        compiler_params=_cparams(("arbitrary", "arbitrary")),
        name="outproj",
    )(ua, ub, w, w, xall, modb, modc, g.reshape(1, d))


def _dattn_lambda(lam_ref, lam_init):
    lp = lam_ref[...]
    return (jnp.exp(jnp.sum(lp[0:1] * lp[1:2], axis=-1, keepdims=True))
            - jnp.exp(jnp.sum(lp[2:3] * lp[3:4], axis=-1, keepdims=True)) + lam_init)


def _dattn_split_maps(q):
    first = (lax.broadcasted_iota(jnp.int32, (1, LANES), 1) & (DA_QK // 2)) == 0
    zero = jnp.zeros_like(q)
    return jnp.concatenate([jnp.where(first, q, zero), jnp.where(first, zero, q)], axis=0)


def _dattn_finish(o, sub_ref, gate, lam_init):
    o = o * lax.rsqrt(jnp.mean(o * o, axis=-1, keepdims=True) + EPS) * sub_ref[...] * (1.0 - lam_init)
    return (o * _silu(gate.astype(F32))).astype(BF16)


def _dattn_kernel(lam_ref, sub_ref, qn_ref, k_ref, vt_ref, g_ref, ua_ref, o_ref, sa_ref, ma_ref, sb_ref, mb_ref,
                  acc_ref, den_ref, *, n_blocks, lam_init):
    del ua_ref
    t = pl.program_id(0)
    lam = _dattn_lambda(lam_ref, lam_init)
    nt = (((1,), (1,)), ((), ()))
    tq = qn_ref.shape[1]

    def step(cur, prev, fin):
        if fin:
            o = acc_ref[...] * (1.0 / den_ref[...])
            o_ref[0] = _dattn_finish((o[:, :tq] - lam * o[:, tq:]).T, sub_ref, g_ref[0], lam_init)
        r = k_ref.shape[1]
        bounds = list(range(0, r - ATT_KCHUNK + 1, ATT_KCHUNK)) + [r]
        if cur is not None:
            q2 = _dattn_split_maps(qn_ref[0])
        m_cur = l_prev = o_prev = None
        for lo, hi in zip(bounds[:-1], bounds[1:]):
            if cur is not None:
                s = lax.dot_general(k_ref[0, lo:hi, :], q2, nt, preferred_element_type=F32)
                cur[0][lo:hi, :] = s
                m = jnp.max(s, axis=0, keepdims=True)
                m_cur = m if m_cur is None else jnp.maximum(m_cur, m)
            if prev is not None:
                e = jnp.exp2(prev[0][lo:hi, :] - prev[1][...])
                l = jnp.sum(e, axis=0, keepdims=True)
                o = jnp.dot(vt_ref[0, 0, :, lo:hi], e.astype(BF16), preferred_element_type=F32)
                l_prev = l if l_prev is None else l_prev + l
                o_prev = o if o_prev is None else o_prev + o
        if cur is not None:
            cur[1][...] = m_cur
        if prev is not None:
            acc_ref[...] = o_prev
            den_ref[...] = l_prev

    bufs = ((sa_ref, ma_ref), (sb_ref, mb_ref))
    last = bufs[(n_blocks - 1) % 2]
    pl.when(t == 0)(lambda: step(bufs[0], None, False))
    pl.when(t == 1)(lambda: step(bufs[1], bufs[0], False))
    pl.when((t >= 2) & (t < n_blocks) & (t % 2 == 0))(lambda: step(bufs[0], bufs[1], True))
    pl.when((t >= 2) & (t < n_blocks) & (t % 2 == 1))(lambda: step(bufs[1], bufs[0], True))
    pl.when(t == n_blocks)(lambda: step(None, last, True))
    pl.when(t == n_blocks + 1)(lambda: step(None, None, True))


def _dattn_ctx_kernel(lam_ref, sub_ref, q_ref, k_ref, v_ref, g_ref, ua_ref, o_ref, *, lam_init):
    del ua_ref
    lam = _dattn_lambda(lam_ref, lam_init)
    nt = (((1,), (1,)), ((), ()))
    n = q_ref.shape[1]
    for hd in range(q_ref.shape[2] // LANES):
        cols = slice(hd * LANES, (hd + 1) * LANES)
        s = lax.dot_general(_dattn_split_maps(q_ref[0, :, cols]), k_ref[0, :, cols], nt,
                            preferred_element_type=F32)
        e = jnp.exp2(s - jnp.max(s, axis=-1, keepdims=True))
        p = e / jnp.sum(e, axis=-1, keepdims=True)
        a = (p[:n] - lam * p[n:]).astype(BF16)
        o = jnp.dot(a, v_ref[0, :, cols], preferred_element_type=F32)
        o_ref[0, :, cols] = _dattn_finish(o, sub_ref, g_ref[0, :, cols], lam_init)


def _diff_attention(proj, vt, lam_p, subln, *, n_lat, lam_init):
    b, r, _ = proj.shape
    h = DA_HEADS
    tq = ATT_TQ
    nq = n_lat // tq
    n_ctx = r - n_lat
    sub = subln.reshape(1, LANES)
    small = [pl.BlockSpec(lam_p.shape, lambda *_: (0, 0)), pl.BlockSpec((1, LANES), lambda *_: (0, 0))]
    n_blocks = b * h * nq

    def unit(t, lag):
        u = jnp.clip(t - lag, 0, n_blocks - 1)
        return u // (h * nq), (u // nq) % h, u % nq

    def q_map(t):
        bi, hi, i = unit(t, 0)
        return bi, i, 2 * h + hi

    def k_map(t):
        bi, hi, _ = unit(t, 0)
        return bi, 0, hi

    def vt_map(t):
        bi, hi, _ = unit(t, 1)
        return bi, hi, 0, 0

    def g_map(t):
        bi, hi, i = unit(t, 2)
        return bi, i, 3 * h + hi

    def o_map(t):
        bi, hi, i = unit(t, 2)
        return bi, i, hi

    ua = pl.pallas_call(
        functools.partial(_dattn_kernel, n_blocks=n_blocks, lam_init=lam_init),
        grid=(n_blocks + 2,),
        in_specs=small + [
            pl.BlockSpec((1, tq, LANES), q_map),
            pl.BlockSpec((1, r, LANES), k_map),
            pl.BlockSpec((1, 1, LANES, r), vt_map),
            pl.BlockSpec((1, tq, LANES), g_map),
            pl.BlockSpec(memory_space=pl.ANY),
        ],
        out_specs=pl.BlockSpec((1, tq, LANES), o_map),
        out_shape=jax.ShapeDtypeStruct((b, r, h * LANES), BF16),
        scratch_shapes=[pltpu.VMEM((r, 2 * tq), F32), pltpu.VMEM((1, 2 * tq), F32)] * 2
        + [pltpu.VMEM((LANES, 2 * tq), F32), pltpu.VMEM((1, 2 * tq), F32)],
        input_output_aliases={6: 0},
        compiler_params=_cparams(("arbitrary",)),
        name="diff_attention",
    )(lam_p, sub, proj, proj, vt, proj, jnp.zeros((b, r, h * LANES), BF16))
    cb = n_lat // n_ctx
    ctx_rows = lambda grp: pl.BlockSpec((1, n_ctx, h * LANES), lambda bi: (bi, cb, grp))
    return pl.pallas_call(
        functools.partial(_dattn_ctx_kernel, lam_init=lam_init),
        grid=(b,),
        in_specs=small + [ctx_rows(2), ctx_rows(0), ctx_rows(1), ctx_rows(3),
                          pl.BlockSpec(memory_space=pl.ANY)],
        out_specs=ctx_rows(0),
        out_shape=jax.ShapeDtypeStruct(ua.shape, ua.dtype),
        input_output_aliases={6: 0},
        compiler_params=_cparams(("arbitrary",)),
        name="diff_attention_ctx",
    )(lam_p, sub, proj, proj, proj, proj, ua)


def _sconv_kernel(h_ref, b_ref, c_ref, g_ref, w_ref, o_ref, *, n_lat):
    r = h_ref.shape[1]
    p = c_ref[0].astype(F32) * h_ref[0].astype(F32)
    row = lax.broadcasted_iota(jnp.int32, (r, 1), 0)
    prev = jnp.where((row == 0) | (row == n_lat), 0.0, pltpu.roll(p, 1, 0))
    nxt = jnp.where((row == n_lat - 1) | (row == r - 1), 0.0, pltpu.roll(p, r - 1, 0))
    w = w_ref[...]
    conv = w[0:1] * prev + w[1:2] * p + w[2:3] * nxt
    o_ref[0] = (b_ref[0].astype(F32) * conv * _silu(g_ref[0].astype(F32))).astype(o_ref.dtype)


def _short_conv(proj, conv_w, *, n_lat):
    b, r, _ = proj.shape
    nb = conv_w.shape[1] // LANES
    base = 4 * nb
    spec = lambda off: pl.BlockSpec((1, r, LANES), lambda bi, j: (bi, 0, off + j))
    return pl.pallas_call(
        functools.partial(_sconv_kernel, n_lat=n_lat),
        grid=(b, nb),
        in_specs=[spec(base), spec(base + nb), spec(base + 2 * nb), spec(base + 3 * nb),
                  pl.BlockSpec((SC_WIDTH, LANES), lambda bi, j: (0, j))],
        out_specs=pl.BlockSpec((1, r, LANES), lambda bi, j: (bi, 0, j)),
        out_shape=jax.ShapeDtypeStruct((b, r, nb * LANES), BF16),
        compiler_params=_cparams(("arbitrary", "arbitrary")),
        name="short_conv",
    )(proj, proj, proj, proj, conv_w)


def _natten_kernel(q_ref, k_ref, v_ref, g_ref, tb_ref, o_ref, *, n_lat, n_rows):
    lane = lax.broadcasted_iota(jnp.int32, (1, LANES), 1)
    first = lane < NA_DIM
    nt = (((1,), (1,)), ((), ()))
    win = WIN_ROWS * GRID_W
    kc = k_ref[0, n_lat:, :]
    vc = v_ref[0, n_lat:, :]

    def stack_heads(q):
        q = q * (NA_DIM ** -0.5)
        zero = jnp.zeros_like(q)
        return jnp.concatenate([jnp.where(first, q, zero), jnp.where(first, zero, q)], axis=0)

    def softmax(parts):
        m = functools.reduce(jnp.maximum, [jnp.max(s, axis=-1, keepdims=True) for s in parts])
        es = [jnp.exp2(s - m) for s in parts]
        l = functools.reduce(jnp.add, [jnp.sum(e, axis=-1, keepdims=True) for e in es])
        return [e.astype(BF16) for e in es], 1.0 / l

    def unstack_heads(o2):
        n = o2.shape[0] // 2
        return jnp.where(first, o2[:n], o2[n:])

    def group_body(gi, carry):
        rows = [gi * NA_GROUP + t for t in range(NA_GROUP)]
        starts = [jnp.clip(rr - WIN_ROWS // 2, 0, n_rows - WIN_ROWS) for rr in rows]
        qsl = [pl.ds(pl.multiple_of(rr * GRID_W, GRID_W), GRID_W) for rr in rows]
        ksl = [pl.ds(pl.multiple_of(rs * GRID_W, GRID_W), win) for rs in starts]
        scores = {}
        weights = {}
        for i in range(NA_GROUP + 2):
            if i < NA_GROUP:
                q2 = stack_heads(q_ref[0, qsl[i], :])
                s_loc = lax.dot_general(q2, k_ref[0, ksl[i], :], nt, preferred_element_type=F32)
                off = starts[i] - rows[i] + (WIN_ROWS - 1)
                s_loc = s_loc + jnp.concatenate([tb_ref[0, off + 2 * p] for p in range(WIN_ROWS // 2)], axis=1)
                scores[i] = [s_loc, lax.dot_general(q2, kc, nt, preferred_element_type=F32)]
            if 0 <= i - 1 < NA_GROUP:
                weights[i - 1] = softmax(scores.pop(i - 1))
            if 0 <= i - 2 < NA_GROUP:
                es, rl = weights.pop(i - 2)
                qs, ks = qsl[i - 2], ksl[i - 2]
                o2 = jnp.dot(es[0], v_ref[0, ks, :], preferred_element_type=F32)
                o2 = o2 + jnp.dot(es[1], vc, preferred_element_type=F32)
                out = unstack_heads(o2 * rl)
                o_ref[0, qs, :] = (out * _silu(g_ref[0, qs, :].astype(F32))).astype(o_ref.dtype)
        return carry

    lax.fori_loop(0, n_rows // NA_GROUP, group_body, 0)

    s = lax.dot_general(stack_heads(q_ref[0, n_lat:, :]), kc, nt, preferred_element_type=F32)
    es, rl = softmax([s])
    outc = unstack_heads(jnp.dot(es[0], vc, preferred_element_type=F32) * rl)
    o_ref[0, n_lat:, :] = (outc * _silu(g_ref[0, n_lat:, :].astype(F32))).astype(o_ref.dtype)


def _neighbourhood_attention(proj, tb, *, n_lat, cols):
    b, r, _ = proj.shape
    nb = NA_HEADS // 2
    kb, vb, qb, gb = (c // LANES for c in cols)
    spec = lambda off: pl.BlockSpec((1, r, LANES), lambda bi, j: (bi, 0, off + j))
    return pl.pallas_call(
        functools.partial(_natten_kernel, n_lat=n_lat, n_rows=n_lat // GRID_W),
        grid=(b, nb),
        in_specs=[spec(qb), spec(kb), spec(vb), spec(gb),
                  pl.BlockSpec((1,) + tb.shape[1:], lambda bi, j: (j, 0, 0, 0))],
        out_specs=pl.BlockSpec((1, r, LANES), lambda bi, j: (bi, 0, j)),
        out_shape=jax.ShapeDtypeStruct((b, r, nb * LANES), BF16),
        compiler_params=_cparams(("arbitrary", "arbitrary")),
        name="neighbourhood_attention",
    )(proj, proj, proj, proj, tb)


def _natten_bias_table(rpb):
    qcol = np.arange(GRID_W)[:, None]
    kcol = np.arange(GRID_W)[None, :]
    win_start = np.clip(qcol - WIN_COLS // 2, 0, GRID_W - WIN_COLS)
    valid = (kcol >= win_start) & (kcol < win_start + WIN_COLS)
    pad = GRID_W - WIN_COLS
    padded = jnp.pad(rpb.astype(F32), ((0, 0), (0, 0), (pad, pad)))
    toe = jnp.stack([padded[:, :, GRID_W - 1 - q:2 * GRID_W - 1 - q] for q in range(GRID_W)], axis=2)
    toe = jnp.where(jnp.asarray(valid)[None, None], toe * LOG2E, -jnp.inf)
    t = jnp.concatenate([toe[:, :-1], toe[:, 1:]], axis=-1)
    nh, n_o = t.shape[:2]
    t = jnp.transpose(t.reshape(nh // 2, 2, n_o, GRID_W, 2 * GRID_W), (0, 2, 1, 3, 4))
    return t.reshape(nh // 2, n_o, 2 * GRID_W, 2 * GRID_W)


def _gla_kernel(q_ref, k_ref, v_ref, g_ref, lr_ref, gu_ref, gb_ref, gn_ref, o_ref, of_scr, st_scr,
                *, n_lat, n_ctx, qscale):
    ch = GLA_CHUNK
    mid = ch // 2
    n_heads, dv, dk = st_scr.shape
    nt = (((1,), (1,)), ((), ()))
    tn = (((0,), (0,)), ((), ()))
    trow = lax.broadcasted_iota(jnp.int32, (ch, ch), 0)
    tcol = lax.broadcasted_iota(jnp.int32, (ch, ch), 1)
    srow = lax.broadcasted_iota(jnp.int32, (ch, 1), 0)

    def cumsum_rows(g, reverse):
        acc = g
        sh = 1
        while sh < ch:
            if reverse:
                acc = acc + jnp.where(srow < ch - sh, pltpu.roll(acc, ch - sh, 0), 0.0)
            else:
                acc = acc + jnp.where(srow >= sh, pltpu.roll(acc, sh, 0), 0.0)
            sh *= 2
        return acc

    def group(starts, dirn):
        reverse = dirn == 1
        keep = (tcol >= trow) if reverse else (tcol <= trow)
        rows_l = [pl.ds(s if isinstance(s, int) else pl.multiple_of(s, ch), ch) for s in starts]
        items = [(rows, hh, slice(hh * dk, (hh + 1) * dk), slice(hh * dv, (hh + 1) * dv))
                 for rows in rows_l for hh in range(n_heads)]
        zs = [jnp.dot(lr_ref[0, rows, :].astype(BF16), gu_ref[dirn, hh], preferred_element_type=F32)
              for rows, hh, _, _ in items]
        ops = []
        for (rows, hh, kc, _), z in zip(items, zs):
            g = jax.nn.log_sigmoid(z + gb_ref[dirn, hh]) * (1.0 / GLA_TAU)
            bcum = cumsum_rows(g, reverse)
            b_end = bcum[0:1] if reverse else bcum[ch - 1:ch]
            rho = bcum[mid:mid + 1]
            up = jnp.exp(bcum - rho)
            dn = jnp.exp(rho - bcum)
            qf = q_ref[0, rows, kc].astype(F32) * qscale
            kf = k_ref[0, rows, kc].astype(F32)
            ops.append(dict(
                q_state=(qf * (up * jnp.exp(rho))).astype(BF16),
                q_local=(qf * up).astype(BF16),
                k_local=(kf * dn).astype(BF16),
                k_state=(kf * (dn * jnp.exp(b_end - rho))).astype(BF16),
                decay=jnp.exp(b_end)))
        atts = [lax.dot_general(o["q_local"], o["k_local"], nt, preferred_element_type=F32) for o in ops]
        st_adds = [lax.dot_general(v_ref[0, rows, vc], o["k_state"], tn, preferred_element_type=F32)
                   for (rows, _, _, vc), o in zip(items, ops)]
        o_locals = [jnp.dot(jnp.where(keep, att, 0.0).astype(BF16), v_ref[0, rows, vc], preferred_element_type=F32)
                    for (rows, _, _, vc), att in zip(items, atts)]
        sts = [st_scr[hh] for hh in range(n_heads)]
        states = []
        for (_, hh, _, _), o, st_add in zip(items, ops, st_adds):
            states.append(sts[hh].astype(BF16))
            sts[hh] = sts[hh] * o["decay"] + st_add
        for hh in range(n_heads):
            st_scr[hh] = sts[hh]
        outs = [o_local + lax.dot_general(o["q_state"], s_in, nt, preferred_element_type=F32)
                for o, o_local, s_in in zip(ops, o_locals, states)]
        for (rows, hh, _, vc), o in zip(items, outs):
            if dirn == 0:
                of_scr[rows, vc] = o
            else:
                o = o + of_scr[rows, vc]
                o = o * lax.rsqrt(jnp.mean(o * o, axis=-1, keepdims=True) + EPS) * gn_ref[...]
                o_ref[0, rows, vc] = (o * _silu(g_ref[0, rows, vc].astype(F32))).astype(o_ref.dtype)

    n_cc = n_ctx // ch
    n_lg = n_lat // (ch * GLA_GROUP)
    gsz = ch * GLA_GROUP

    def latent_group(gi, dirn):
        if dirn == 0:
            group([gi * gsz + t * ch for t in range(GLA_GROUP)], 0)
        else:
            group([(n_lg - 1 - gi) * gsz + t * ch for t in range(GLA_GROUP - 1, -1, -1)], 1)

    st_scr[...] = jnp.zeros_like(st_scr)
    group([n_lat + c * ch for c in range(n_cc)], 0)
    lax.fori_loop(0, n_lg, lambda gi, carry: (latent_group(gi, 0), carry)[1], 0)
    st_scr[...] = jnp.zeros_like(st_scr)
    group([n_lat + c * ch for c in range(n_cc - 1, -1, -1)], 1)
    lax.fori_loop(0, n_lg, lambda gi, carry: (latent_group(gi, 1), carry)[1], 0)


def _gla(proj, lr, gu, gb, gnorm, *, n_lat, cols):
    b, r, _ = proj.shape
    nh = GLA_HEADS
    dk = LANES
    dv = 2 * LANES
    kcol, vcol, qcol, gcol = cols
    hs = GLA_HEADS_PER_STEP
    wk, wv = hs * dk, hs * dv
    return pl.pallas_call(
        functools.partial(_gla_kernel, n_lat=n_lat, n_ctx=r - n_lat, qscale=dk ** -0.5),
        grid=(b, nh // hs),
        in_specs=[
            pl.BlockSpec((1, r, wk), lambda bi, hi: (bi, 0, qcol // wk + hi)),
            pl.BlockSpec((1, r, wk), lambda bi, hi: (bi, 0, kcol // wk + hi)),
            pl.BlockSpec((1, r, wv), lambda bi, hi: (bi, 0, vcol // wv + hi)),
            pl.BlockSpec((1, r, wv), lambda bi, hi: (bi, 0, gcol // wv + hi)),
            pl.BlockSpec((1, r, LANES), lambda bi, hi: (bi, 0, 0)),
            pl.BlockSpec((2, hs, LANES, dk), lambda bi, hi: (0, hi, 0, 0)),
            pl.BlockSpec((2, hs, 1, dk), lambda bi, hi: (0, hi, 0, 0)),
            pl.BlockSpec((1, dv), lambda bi, hi: (0, 0)),
        ],
        out_specs=pl.BlockSpec((1, r, wv), lambda bi, hi: (bi, 0, hi)),
        out_shape=jax.ShapeDtypeStruct((b, r, nh * dv), BF16),
        scratch_shapes=[pltpu.VMEM((r, wv), F32), pltpu.VMEM((hs, dv, dk), F32)],
        compiler_params=_cparams(("arbitrary", "arbitrary")),
        name="gla",
    )(proj, proj, proj, proj, lr, gu, gb, gnorm.reshape(1, dv))


def _rope_tables(n_lat, n_ctx):
    half = DA_QK // 2
    inv = 1.0 / (ROPE_BASE ** (jnp.arange(0, half, 2, dtype=F32) / half))
    t = jnp.arange(n_lat)
    row = (t // GRID_W).astype(F32)[:, None] * inv
    col = (t % GRID_W).astype(F32)[:, None] * inv
    ang = jnp.concatenate([row, row, col, col], axis=-1)
    lane = np.arange(LANES)
    dim = 32 * ((lane >> 4) & 1) + 16 * (lane >> 6) + (lane & 15)
    sign = jnp.asarray(np.where((lane >> 6) == 0, -1.0, 1.0), F32)
    cos = jnp.concatenate([jnp.cos(ang)[:, dim], jnp.ones((n_ctx, LANES), F32)], axis=0)
    sin = jnp.concatenate([jnp.sin(ang)[:, dim] * sign, jnp.zeros((n_ctx, LANES), F32)], axis=0)
    qf = LOG2E * DA_QK ** -0.5
    return jnp.stack([cos, cos * qf]), jnp.stack([sin, sin * qf])


def _pair_layout(w):
    d = w.shape[0]
    w = w.reshape(d, -1, 2, 2, 2, DA_QK // 4)
    return jnp.transpose(w, (0, 1, 4, 2, 3, 5)).reshape(d, -1)


def kernel(x, c, ctx, c_ctx, w_mod, b_mod, g_pre, g_post, w_out, ev_w_in, ev_lambda, ev_subln, ev_conv,
           od_w_in, od_rpb, od_gate_up, od_gate_bias, od_gnorm):
    b, s, d = x.shape
    n_ctx = ctx.shape[1]
    depth = w_mod.shape[0]
    bw = EXPAND * d // 2
    assert bw == DA_HEADS * LANES == NA_HEADS * NA_DIM and (s + n_ctx) % (ROW_SPLIT * 16) == 0
    assert s % ATT_TQ == 0 and s % n_ctx == 0 and s % GRID_W == 0 and n_ctx % GLA_CHUNK == 0
    assert (s // GRID_W) % NA_GROUP == 0 and s % (GLA_CHUNK * GLA_GROUP) == 0
    tm = (s + n_ctx) // ROW_SPLIT
    assert tm % ROPE_ROWS == 0 and (ROW_SPLIT - 1) * tm <= s and s % (ROW_SPLIT * 16) == 0

    xall = jnp.concatenate([x, ctx], axis=1)
    nrow = -(-(b + 1) // 8) * 8
    cc = jnp.zeros((nrow, d), F32).at[:b].set(c).at[b].set(c_ctx)
    mod = _modulation(cc, w_mod, b_mod)
    cos_t, sin_t = _rope_tables(s, n_ctx)

    gk = bw // 2
    od_lr0 = 3 * bw + gk
    od_cols = np.cumsum([0, bw, bw, gk, bw, bw, gk, bw])
    na_cols = (int(od_cols[0]), int(od_cols[1]), int(od_cols[4]), int(od_cols[6]))
    gla_cols = (int(od_cols[2]), int(od_cols[3]), int(od_cols[5]), int(od_cols[7]))

    for l in range(depth):
        j = l // 2
        modb = mod[l, :b].reshape(b, 1, 3 * d)
        modc = mod[l, b:b + 1]
        if l % 2 == 0:
            lam_init = 0.8 - 0.6 * math.exp(-0.3 * l)
            w = ev_w_in[j]
            w = jnp.concatenate([_pair_layout(w[:, :bw]), w[:, bw:2 * bw], _pair_layout(w[:, 2 * bw:3 * bw]),
                                 w[:, 3 * bw:]], axis=1).astype(BF16)
            proj, _ = _inproj(xall, modb, modc, g_pre[l], w, (cos_t, sin_t, 0, 2), None, n_lat=s, tn=bw)
            vt = jnp.swapaxes(proj[:, :, bw:2 * bw], 1, 2).reshape(b, DA_HEADS, LANES, s + n_ctx)
            ua = _diff_attention(proj, vt, ev_lambda[j], ev_subln[j], n_lat=s, lam_init=lam_init)
            ub = _short_conv(proj, ev_conv[j], n_lat=s)
        else:
            w = od_w_in[j]
            w_lr = jnp.pad(w[:, od_lr0:od_lr0 + 2 * GLA_RANK], ((0, 0), (0, LANES - 2 * GLA_RANK))).astype(BF16)
            w = jnp.concatenate([w[:, :od_lr0], w[:, od_lr0 + 2 * GLA_RANK:]], axis=1)
            col_scale = np.ones((1, w.shape[1]), np.float32)
            col_scale[:, na_cols[2]:na_cols[2] + bw] = LOG2E
            w = (w * col_scale).astype(BF16)
            proj, lr = _inproj(xall, modb, modc, g_pre[l], w, None, w_lr, n_lat=s, tn=bw)
            tb = _natten_bias_table(od_rpb[j])
            ua = _neighbourhood_attention(proj, tb, n_lat=s, cols=na_cols)
            gu = jnp.zeros((2, GLA_HEADS, LANES, LANES), F32)
            gup = od_gate_up[j].reshape(2, GLA_RANK, GLA_HEADS, LANES).transpose(0, 2, 1, 3)
            gu = gu.at[0, :, :GLA_RANK].set(gup[0]).at[1, :, GLA_RANK:2 * GLA_RANK].set(gup[1]).astype(BF16)
            gb = od_gate_bias[j].reshape(2, GLA_HEADS, 1, LANES)
            ub = _gla(proj, lr, gu, gb, od_gnorm[j], n_lat=s, cols=gla_cols)
        xall = _outproj(ua, ub, w_out[l].astype(BF16), xall, modb, modc, g_post[l], n_lat=s,
                        latent_only=l == depth - 1)
    return xall
```
